```python
import jax, jax.numpy as jnp
from jax import lax
import numpy as np

D_MODEL = 1024
BATCH = 4
SEQ = 4096
DEPTH = 2

RWKV_HEADS = 8
RWKV_HEAD_DIM = 64
RWKV_DIM = RWKV_HEADS * RWKV_HEAD_DIM
DECAY_LORA = 64
ICLR_LORA = 64
GATE_LORA = 128
LNX_EPS = 64e-5
MLA_HEADS = 8
QK_NOPE_DIM = 64
QK_ROPE_DIM = 32
V_HEAD_DIM = 64
Q_LORA_RANK = 256
KV_LORA_RANK = 128
ROPE_THETA = 10000.0
Q_BLOCK = 128
NEG_INF = -1e30
FFN_DIM = 2816
CONV_WIDTH = 3
NORM_EPS = 1e-6
MAX_POS_OFFSET = 1024

SHIFT_COLS = 3 * RWKV_DIM + DECAY_LORA + ICLR_LORA + GATE_LORA
MLA_COLS = Q_LORA_RANK + KV_LORA_RANK + QK_ROPE_DIM
GATE_COLS = 2 * D_MODEL
IN_COLS = SHIFT_COLS + MLA_COLS + GATE_COLS

kernel_name = 'hybrid_rwkv7_mla_convffn'


def _rmsnorm(x, g, eps=NORM_EPS):
    xf = x.astype(jnp.float32)
    xf = xf * lax.rsqrt(jnp.mean(xf * xf, axis=-1, keepdims=True) + eps)
    return xf.astype(x.dtype) * g


def _token_shift(x):
    return jnp.pad(x, ((0, 0), (1, 0), (0, 0)))[:, :-1]


def _rope_tables(positions, dtype):
    inv_freq = jnp.power(ROPE_THETA, -jnp.arange(0, QK_ROPE_DIM, 2, dtype=jnp.float32) / QK_ROPE_DIM)
    ang = positions.astype(jnp.float32)[..., None] * inv_freq
    return jnp.cos(ang).astype(dtype), jnp.sin(ang).astype(dtype)


def _rope(x, cos, sin):
    half = x.shape[-1] // 2
    x1, x2 = x[..., :half], x[..., half:]
    return jnp.concatenate([x1 * cos - x2 * sin, x1 * sin + x2 * cos], axis=-1)


def _wkv7_scan(r, w, k, v, a, b):
    B, S, H, N = r.shape

    def step(state, inp):
        r_t, w_t, k_t, v_t, a_t, b_t = inp
        sa = jnp.einsum('bhij,bhj->bhi', state, a_t)
        state = (state * w_t[:, :, None, :]
                 + sa[..., None] * b_t[:, :, None, :]
                 + v_t[..., None] * k_t[:, :, None, :])
        y_t = jnp.einsum('bhij,bhj->bhi', state, r_t)
        return state, y_t

    xs = tuple(jnp.moveaxis(t, 1, 0) for t in (r, w, k, v, a, b))
    s0 = jnp.zeros((B, H, N, N), jnp.float32)
    _, y = lax.scan(step, s0, xs)
    return jnp.moveaxis(y, 0, 1)


def _rwkv7_time_mix(p_r, p_k, p_v, p_w, p_a, p_g, decay_base, w_decay_up, iclr_base,
                    w_iclr_up, w_gate_up, k_k, k_a, r_k, lnx_w, lnx_b):
    B, S, _ = p_r.shape
    H, N = RWKV_HEADS, RWKV_HEAD_DIM
    dt = p_r.dtype

    def heads(t):
        return t.reshape(B, S, H, N)

    log_w = -jax.nn.softplus(-(decay_base + jnp.tanh(p_w) @ w_decay_up)) - 0.5
    decay = jnp.exp(-jnp.exp(log_w.astype(jnp.float32)))
    iclr = jax.nn.sigmoid(iclr_base + p_a @ w_iclr_up)
    gate = jax.nn.sigmoid(p_g) @ w_gate_up
    kk = heads((p_k * k_k).astype(jnp.float32))
    kk = kk * lax.rsqrt(jnp.sum(kk * kk, axis=-1, keepdims=True) + 1e-12)
    k = p_k * (1.0 + (iclr - 1.0) * k_a)
    r_h, k_h, v_h = heads(p_r), heads(k), heads(p_v)
    a_h = heads(iclr).astype(jnp.float32)
    y = _wkv7_scan(r_h.astype(jnp.float32), heads(decay), k_h.astype(jnp.float32),
                   v_h.astype(jnp.float32), -kk, kk * a_h)
    mu = jnp.mean(y, axis=-1, keepdims=True)
    var = jnp.mean(jnp.square(y - mu), axis=-1, keepdims=True)
    y = ((y - mu) * lax.rsqrt(var + LNX_EPS)).reshape(B, S, RWKV_DIM).astype(dt) * lnx_w + lnx_b
    bonus = (jnp.sum(r_h * k_h * r_k, axis=-1, keepdims=True) * v_h).reshape(B, S, RWKV_DIM)
    return (y + bonus) * gate


def _mla(q_lat, kv_lat, k_pe, cos, sin, q_norm, w_q_up, kv_norm, w_kv_up):
    B, S, _ = q_lat.shape
    H = MLA_HEADS
    q = (_rmsnorm(q_lat, q_norm) @ w_q_up).reshape(B, S, H, QK_NOPE_DIM + QK_ROPE_DIM)
    q_nope, q_pe = q[..., :QK_NOPE_DIM], q[..., QK_NOPE_DIM:]
    q_pe = _rope(q_pe, cos[:, :, None, :], sin[:, :, None, :])
    k_pe = _rope(k_pe, cos, sin)
    kv = (_rmsnorm(kv_lat, kv_norm) @ w_kv_up).reshape(B, S, H, QK_NOPE_DIM + V_HEAD_DIM)
    k_nope, v = kv[..., :QK_NOPE_DIM], kv[..., QK_NOPE_DIM:]

    n_blocks = S // Q_BLOCK
    qn_blocks = jnp.moveaxis(q_nope.reshape(B, n_blocks, Q_BLOCK, H, QK_NOPE_DIM), 1, 0)
    qr_blocks = jnp.moveaxis(q_pe.reshape(B, n_blocks, Q_BLOCK, H, QK_ROPE_DIM), 1, 0)
    starts = jnp.arange(n_blocks, dtype=jnp.int32) * Q_BLOCK
    key_pos = jnp.arange(S, dtype=jnp.int32)
    scale = (QK_NOPE_DIM + QK_ROPE_DIM) ** -0.5

    def attend(args):
        qn, qr, q0 = args
        s = (jnp.einsum('bqhd,bkhd->bhqk', qn, k_nope)
             + jnp.einsum('bqhr,bkr->bhqk', qr, k_pe)).astype(jnp.float32) * scale
        q_pos = q0 + jnp.arange(Q_BLOCK, dtype=jnp.int32)
        s = jnp.where(key_pos[None, :] <= q_pos[:, None], s, NEG_INF)
        p = jax.nn.softmax(s, axis=-1).astype(v.dtype)
        return jnp.einsum('bhqk,bkhd->bqhd', p, v)

    o = lax.map(attend, (qn_blocks, qr_blocks, starts))
    return jnp.moveaxis(o, 0, 1).reshape(B, S, H * V_HEAD_DIM)


def _mixer_block(n, cos, sin, w_in, mu_shift, decay_base, w_decay_up, iclr_base, w_iclr_up,
                 w_gate_up, k_k, k_a, r_k, lnx_w, lnx_b, w_out_rwkv, q_norm, w_q_up, kv_norm,
                 w_kv_up, w_out_mla, w_out):
    p = n @ w_in
    p_rw = p[..., :SHIFT_COLS]
    p_rw = p_rw + (_token_shift(p_rw) - p_rw) * mu_shift
    rw_splits = [RWKV_DIM, 2 * RWKV_DIM, 3 * RWKV_DIM, 3 * RWKV_DIM + DECAY_LORA,
                 3 * RWKV_DIM + DECAY_LORA + ICLR_LORA]
    p_r, p_k, p_v, p_w, p_a, p_g = jnp.split(p_rw, rw_splits, axis=-1)
    p_mla = p[..., SHIFT_COLS:SHIFT_COLS + MLA_COLS]
    q_lat, kv_lat, k_pe = jnp.split(p_mla, [Q_LORA_RANK, Q_LORA_RANK + KV_LORA_RANK], axis=-1)
    gate_a, gate_b = jnp.split(p[..., SHIFT_COLS + MLA_COLS:], 2, axis=-1)

    y_a = _rwkv7_time_mix(p_r, p_k, p_v, p_w, p_a, p_g, decay_base, w_decay_up, iclr_base,
                          w_iclr_up, w_gate_up, k_k, k_a, r_k, lnx_w, lnx_b) @ w_out_rwkv
    y_b = _mla(q_lat, kv_lat, k_pe, cos, sin, q_norm, w_q_up, kv_norm, w_kv_up) @ w_out_mla
    merged = jax.nn.sigmoid(gate_a) * y_a + jax.nn.sigmoid(gate_b) * y_b
    return merged @ w_out


def _conv_ffn(n, w_ffn_up, conv_w, conv_b, w_ffn_down):
    S = n.shape[1]
    u_gate, u_val = jnp.split(n @ w_ffn_up, 2, axis=-1)
    up = jnp.pad(u_gate, ((0, 0), (CONV_WIDTH - 1, 0), (0, 0)))
    c = conv_b
    for j in range(CONV_WIDTH):
        c = c + conv_w[j] * up[:, j:j + S]
    return (jax.nn.gelu(c, approximate=False) * u_val) @ w_ffn_down


def setup_inputs(seed: int = 0) -> dict:
    key = jax.random.key(seed)
    ks = jax.random.split(key, 32)
    f32 = jnp.float32

    def nrm(k, shape, fan_in, s=1.0):
        return jax.random.normal(k, shape, f32) * (s * fan_in ** -0.5)

    def gain(k, shape):
        return 1.0 + 0.02 * jax.random.normal(k, shape, f32)

    H, N = RWKV_HEADS, RWKV_HEAD_DIM
    offs = jax.random.randint(ks[1], (BATCH, 1), 0, MAX_POS_OFFSET, dtype=jnp.int32)
    positions = offs + jnp.arange(SEQ, dtype=jnp.int32)[None, :]
    return {
        'x': jax.random.normal(ks[0], (BATCH, SEQ, D_MODEL), f32),
        'positions': positions,
        'attn_norm': gain(ks[2], (DEPTH, D_MODEL)),
        'w_in': nrm(ks[3], (DEPTH, D_MODEL, IN_COLS), D_MODEL),
        'mu_shift': jax.random.uniform(ks[4], (DEPTH, SHIFT_COLS), f32),
        'decay_base': jax.random.uniform(ks[5], (DEPTH, RWKV_DIM), f32, minval=-4.0, maxval=1.0),
        'w_decay_up': nrm(ks[6], (DEPTH, DECAY_LORA, RWKV_DIM), DECAY_LORA),
        'iclr_base': 0.1 * jax.random.normal(ks[7], (DEPTH, RWKV_DIM), f32),
        'w_iclr_up': nrm(ks[8], (DEPTH, ICLR_LORA, RWKV_DIM), ICLR_LORA),
        'w_gate_up': nrm(ks[9], (DEPTH, GATE_LORA, RWKV_DIM), GATE_LORA),
        'k_k': 0.85 + 0.05 * jax.random.normal(ks[10], (DEPTH, RWKV_DIM), f32),
        'k_a': 1.0 + 0.05 * jax.random.normal(ks[11], (DEPTH, RWKV_DIM), f32),
        'r_k': 0.1 * jax.random.normal(ks[12], (DEPTH, H, N), f32),
        'lnx_w': gain(ks[13], (DEPTH, RWKV_DIM)),
        'lnx_b': 0.02 * jax.random.normal(ks[14], (DEPTH, RWKV_DIM), f32),
        'w_out_rwkv': nrm(ks[15], (DEPTH, RWKV_DIM, D_MODEL), RWKV_DIM),
        'q_norm': gain(ks[16], (DEPTH, Q_LORA_RANK)),
        'w_q_up': nrm(ks[17], (DEPTH, Q_LORA_RANK, MLA_HEADS * (QK_NOPE_DIM + QK_ROPE_DIM)), Q_LORA_RANK),
        'kv_norm': gain(ks[18], (DEPTH, KV_LORA_RANK)),
        'w_kv_up': nrm(ks[19], (DEPTH, KV_LORA_RANK, MLA_HEADS * (QK_NOPE_DIM + V_HEAD_DIM)), KV_LORA_RANK),
        'w_out_mla': nrm(ks[20], (DEPTH, MLA_HEADS * V_HEAD_DIM, D_MODEL), MLA_HEADS * V_HEAD_DIM),
        'w_out': nrm(ks[21], (DEPTH, D_MODEL, D_MODEL), D_MODEL),
        'ffn_norm': gain(ks[22], (DEPTH, D_MODEL)),
        'w_ffn_up': nrm(ks[23], (DEPTH, D_MODEL, 2 * FFN_DIM), D_MODEL),
        'conv_w': nrm(ks[24], (DEPTH, CONV_WIDTH, FFN_DIM), CONV_WIDTH),
        'conv_b': 0.02 * jax.random.normal(ks[25], (DEPTH, FFN_DIM), f32),
        'w_ffn_down': nrm(ks[26], (DEPTH, FFN_DIM, D_MODEL), FFN_DIM),
        'final_norm': gain(ks[27], (D_MODEL,)),
    }


def reference(x, positions, attn_norm, w_in, mu_shift, decay_base, w_decay_up, iclr_base,
              w_iclr_up, w_gate_up, k_k, k_a, r_k, lnx_w, lnx_b, w_out_rwkv, q_norm, w_q_up,
              kv_norm, w_kv_up, w_out_mla, w_out, ffn_norm, w_ffn_up, conv_w, conv_b,
              w_ffn_down, final_norm):
    cos, sin = _rope_tables(positions, x.dtype)
    h = x
    for l in range(DEPTH):
        n = _rmsnorm(h, attn_norm[l])
        h = h + _mixer_block(n, cos, sin, w_in[l], mu_shift[l], decay_base[l], w_decay_up[l],
                             iclr_base[l], w_iclr_up[l], w_gate_up[l], k_k[l], k_a[l], r_k[l],
                             lnx_w[l], lnx_b[l], w_out_rwkv[l], q_norm[l], w_q_up[l], kv_norm[l],
                             w_kv_up[l], w_out_mla[l], w_out[l])
        n = _rmsnorm(h, ffn_norm[l])
        h = h + _conv_ffn(n, w_ffn_up[l], conv_w[l], conv_b[l], w_ffn_down[l])
    return _rmsnorm(h, final_norm)
```

```python
import functools

import jax
import jax.numpy as jnp
import numpy as np
from jax import lax
from jax.experimental import pallas as pl
from jax.experimental.pallas import tpu as pltpu

F32 = jnp.float32
BF16 = jnp.bfloat16

D_MODEL = 1024
RWKV_HEADS = 8
RWKV_HEAD_DIM = 64
RWKV_DIM = RWKV_HEADS * RWKV_HEAD_DIM
DECAY_LORA = 64
ICLR_LORA = 64
GATE_LORA = 128
LNX_EPS = 64e-5
MLA_HEADS = 8
QK_NOPE_DIM = 64
QK_ROPE_DIM = 32
V_HEAD_DIM = 64
Q_LORA_RANK = 256
KV_LORA_RANK = 128
ROPE_THETA = 10000.0
NEG_INF = -1e30
FFN_DIM = 2816
NORM_EPS = 1e-6
SHIFT_COLS = 3 * RWKV_DIM + DECAY_LORA + ICLR_LORA + GATE_LORA
MLA_COLS = Q_LORA_RANK + KV_LORA_RANK + QK_ROPE_DIM

LANES = 128
SUBLANES = 8
VMEM_LIMIT_BYTES = 56 * 1024 * 1024

MLA_PAD_COLS = 4 * LANES
ROPE_LANE0 = QK_NOPE_DIM
WKV_CHUNK = 64
PAIR = 2 * RWKV_HEAD_DIM


def _params(*semantics):
    return pltpu.CompilerParams(dimension_semantics=semantics, vmem_limit_bytes=VMEM_LIMIT_BYTES)


def _dot(a, b):
    return jnp.dot(a.astype(BF16), b.astype(BF16), preferred_element_type=F32)


def _dot_nt(a, b):
    return lax.dot_general(a.astype(BF16), b.astype(BF16), (((1,), (1,)), ((), ())),
                           preferred_element_type=F32)


def _split_bf16(x, terms):
    parts = []
    rest = x
    for _ in range(terms):
        p = rest.astype(BF16)
        parts.append(p)
        rest = rest - p.astype(F32)
    return parts


def _dot_split_lhs(a, b_bf16, terms):
    out = None
    for p in _split_bf16(a, terms):
        d = jnp.dot(p, b_bf16, preferred_element_type=F32)
        out = d if out is None else out + d
    return out


def _dot_x3(a, b_hi, b_lo):
    a_hi, a_lo = _split_bf16(a, 2)
    return (jnp.dot(a_hi, b_hi, preferred_element_type=F32)
            + jnp.dot(a_hi, b_lo, preferred_element_type=F32)
            + jnp.dot(a_lo, b_hi, preferred_element_type=F32))


def _rms(x, g):
    ms = jnp.mean(x * x, axis=-1, keepdims=True)
    return (x * lax.rsqrt(ms + NORM_EPS)) * g


def _const_spec(shape):
    return pl.BlockSpec(shape, lambda *_: (0,) * len(shape))


def _rope_tables_kernel(pos_ref, invf_ref, cos_ref, sina_ref, sinb_ref):
    ang = pos_ref[...] * invf_ref[...]
    lane = lax.broadcasted_iota(jnp.int32, ang.shape, 1)
    half = QK_ROPE_DIM // 2
    first = (lane >= ROPE_LANE0) & (lane < ROPE_LANE0 + half)
    second = (lane >= ROPE_LANE0 + half) & (lane < ROPE_LANE0 + QK_ROPE_DIM)
    c = jnp.cos(ang)
    s = jnp.sin(ang)
    cos_ref[...] = jnp.where(first | second, c, 1.0)
    sina_ref[...] = jnp.where(first, -s, 0.0)
    sinb_ref[...] = jnp.where(second, s, 0.0)


def _rope_tables(positions, tm):
    t = positions.size
    half = QK_ROPE_DIM // 2
    inv_freq = jnp.power(ROPE_THETA, -jnp.arange(0, QK_ROPE_DIM, 2, dtype=F32) / QK_ROPE_DIM)
    invf = jnp.zeros((1, LANES), F32)
    invf = invf.at[0, ROPE_LANE0:ROPE_LANE0 + half].set(inv_freq)
    invf = invf.at[0, ROPE_LANE0 + half:ROPE_LANE0 + QK_ROPE_DIM].set(inv_freq)
    posb = jnp.broadcast_to(positions.reshape(t, 1).astype(F32), (t, LANES))
    spec = pl.BlockSpec((tm, LANES), lambda i: (i, 0))
    out = jax.ShapeDtypeStruct((t, LANES), F32)
    return pl.pallas_call(
        _rope_tables_kernel,
        grid=(t // tm,),
        in_specs=[spec, _const_spec((1, LANES))],
        out_specs=[spec, spec, spec],
        out_shape=[out, out, out],
        compiler_params=_params("parallel"),
        name="rope_tables",
    )(posb, invf)


def _rope(tile, cos, sina, sinb):
    half = QK_ROPE_DIM // 2
    return (tile * cos + pltpu.roll(tile, LANES - half, axis=1) * sina
            + pltpu.roll(tile, half, axis=1) * sinb)


def _norm_inproj_kernel(x_ref, g_ref, wrw_ref, wmla_ref, wg_ref, prw_ref, pmla_ref, pg_ref):
    n = _rms(x_ref[...], g_ref[...]).astype(BF16)
    prw_ref[...] = jnp.dot(n, wrw_ref[...], preferred_element_type=F32)
    pmla_ref[...] = jnp.dot(n, wmla_ref[...], preferred_element_type=F32)
    pg_ref[...] = jnp.dot(n, wg_ref[...], preferred_element_type=F32)


def _norm_inproj(h, g, w_rw, w_mla, w_g, tm):
    t = h.shape[0]
    row = lambda c: pl.BlockSpec((tm, c), lambda i: (i, 0))
    return pl.pallas_call(
        _norm_inproj_kernel,
        grid=(t // tm,),
        in_specs=[row(D_MODEL), _const_spec((1, D_MODEL)), _const_spec(w_rw.shape),
                  _const_spec(w_mla.shape), _const_spec(w_g.shape)],
        out_specs=[row(SHIFT_COLS), row(MLA_PAD_COLS), row(2 * D_MODEL)],
        out_shape=[jax.ShapeDtypeStruct((t, SHIFT_COLS), F32),
                   jax.ShapeDtypeStruct((t, MLA_PAD_COLS), F32),
                   jax.ShapeDtypeStruct((t, 2 * D_MODEL), F32)],
        compiler_params=_params("parallel"),
        name="norm_inproj",
    )(h, g, w_rw, w_mla, w_g)


def _rwkv_prep_kernel(blocks_per_seq, p_ref, pprev_ref, mu_ref, wl_hi_ref, wl_lo_ref, wg_hi_ref,
                      wg_lo_ref, dbase_ref, ibase_ref, kk_ref, ka_ref, rk_ref, ones_ref,
                      r_ref, lw_ref, k_ref, v_ref, a_ref, b_ref, gate_ref, bonus_ref):
    i = pl.program_id(0)
    x = p_ref[...]
    tm = x.shape[0]
    prev = pprev_ref[SUBLANES - 1:SUBLANES, :]
    prev = jnp.where(i % blocks_per_seq == 0, jnp.zeros_like(prev), prev)
    row = lax.broadcasted_iota(jnp.int32, (tm, 1), 0)
    shifted = jnp.where(row == 0, prev, pltpu.roll(x, 1, axis=0))
    xs = x + (shifted - x) * mu_ref[...]

    c0 = 3 * RWKV_DIM
    p_r = xs[:, 0:RWKV_DIM]
    p_k = xs[:, RWKV_DIM:2 * RWKV_DIM]
    p_v = xs[:, 2 * RWKV_DIM:c0]
    p_wa = xs[:, c0:c0 + DECAY_LORA + ICLR_LORA]
    p_g = xs[:, c0 + DECAY_LORA + ICLR_LORA:SHIFT_COLS]

    lane = lax.broadcasted_iota(jnp.int32, p_wa.shape, 1)
    lora_in = jnp.where(lane < DECAY_LORA, jnp.tanh(p_wa), p_wa)
    lora = _dot_x3(lora_in, wl_hi_ref[...], wl_lo_ref[...])
    log_w = -jax.nn.softplus(-(dbase_ref[...] + lora[:, 0:RWKV_DIM])) - 0.5
    iclr = jax.nn.sigmoid(ibase_ref[...] + lora[:, RWKV_DIM:2 * RWKV_DIM])
    gate = _dot_x3(jax.nn.sigmoid(p_g), wg_hi_ref[...], wg_lo_ref[...])

    ones = ones_ref[...]
    kk = p_k * kk_ref[...]
    kk = kk * lax.rsqrt(_dot_split_lhs(kk * kk, ones, 2) + 1e-12)
    k = p_k * (1.0 + (iclr - 1.0) * ka_ref[...])

    r_ref[...] = p_r
    lw_ref[...] = -jnp.exp(log_w)
    k_ref[...] = k
    v_ref[...] = p_v
    a_ref[...] = -kk
    b_ref[...] = kk * iclr
    gate_ref[...] = gate
    bonus_ref[...] = _dot_split_lhs(p_r * k * rk_ref[...], ones, 2) * p_v


def _rwkv_prep(p_rw, prm, seq, tm):
    t = p_rw.shape[0]
    row = pl.BlockSpec((tm, RWKV_DIM), lambda i: (i, 0))
    out = jax.ShapeDtypeStruct((t, RWKV_DIM), F32)
    prev_spec = pl.BlockSpec((SUBLANES, SHIFT_COLS),
                             lambda i: (jnp.maximum(i * (tm // SUBLANES) - 1, 0), 0))
    consts = [prm["mu"], prm["wl_hi"], prm["wl_lo"], prm["wg_hi"], prm["wg_lo"], prm["decay_base"],
              prm["iclr_base"], prm["k_k"], prm["k_a"], prm["r_k"], prm["head_ones"]]
    return pl.pallas_call(
        functools.partial(_rwkv_prep_kernel, seq // tm),
        grid=(t // tm,),
        in_specs=[pl.BlockSpec((tm, SHIFT_COLS), lambda i: (i, 0)), prev_spec]
        + [_const_spec(c.shape) for c in consts],
        out_specs=[row] * 8,
        out_shape=[out] * 8,
        compiler_params=_params("parallel"),
        name="rwkv_prep",
    )(p_rw, p_rw, *consts)


def _wkv_pair(r, lw, k, v, a, b, state, ltri, eye):
    c = WKV_CHUNK
    lane = lax.broadcasted_iota(jnp.int32, (1, PAIR), 1)
    lo = lane < RWKV_HEAD_DIM

    cum = _dot_split_lhs_left(ltri, lw)
    cum_last = cum[c - 1:c, :]
    g = jnp.exp(cum)
    g_inv = jnp.exp(-cum)
    g_prev = jnp.exp(cum - lw)
    g_out = jnp.exp(cum_last - cum)
    a_t = a * g_prev
    r_t = r * g
    zero = jnp.zeros_like(a_t)
    lhs = jnp.concatenate([jnp.where(lo, a_t, zero), jnp.where(lo, zero, a_t),
                           jnp.where(lo, r_t, zero), jnp.where(lo, zero, r_t)], axis=0)
    b_t = b * g_inv
    k_t = k * g_inv
    a_b = _dot_nt(lhs, jnp.concatenate([b_t, b_t], axis=0))
    a_k = _dot_nt(lhs, jnp.concatenate([k_t, k_t], axis=0))

    rr = lax.broadcasted_iota(jnp.int32, (2 * c, 2 * c), 0)
    cc = lax.broadcasted_iota(jnp.int32, (2 * c, 2 * c), 1)
    same_head = (rr // c) == (cc // c)
    strict = same_head & ((cc % c) < (rr % c))
    incl = same_head & ((cc % c) <= (rr % c))
    l_bd = jnp.where(strict, a_b[0:2 * c], 0.0)
    aak_bd = jnp.where(strict, a_k[0:2 * c], 0.0)
    arb_bd = jnp.where(incl, a_b[2 * c:4 * c], 0.0)
    ark_bd = jnp.where(incl, a_k[2 * c:4 * c], 0.0)

    t_bd = eye + l_bd
    p = l_bd
    n = 1
    while 2 * n < c:
        p = _dot(p, p)
        t_bd = t_bd + _dot(p, t_bd)
        n *= 2

    v_st = jnp.concatenate([jnp.where(lo, v, zero), jnp.where(lo, zero, v)], axis=0)
    q_st = _dot(aak_bd, v_st)
    tq_ta = _dot(t_bd, jnp.concatenate([lhs[0:2 * c], q_st], axis=1))
    rq_ra = _dot(arb_bd, tq_ta)
    ta_st = tq_ta[:, 0:PAIR]
    tq_st = tq_ta[:, PAIR:2 * PAIR]
    rh_st = lhs[2 * c:4 * c] + rq_ra[:, 0:PAIR]
    yh_st = rq_ra[:, PAIR:2 * PAIR] + _dot(ark_bd, v_st)
    unstack = lambda x: x[0:c] + x[c:2 * c]
    ta = unstack(ta_st)
    rh = unstack(rh_st)
    tq = unstack(tq_st)
    yh = unstack(yh_st)

    z = _dot_nt(state, jnp.concatenate([ta, rh], axis=0))
    z = z + jnp.concatenate([tq, yh], axis=0).T
    v_t = jnp.concatenate([v, v], axis=0).T
    lane2 = lax.broadcasted_iota(jnp.int32, (1, 2 * c), 1)
    upd = _dot(jnp.where(lane2 < c, z, v_t), jnp.concatenate([b * g_out, k * g_out], axis=0))
    new_state = state * jnp.exp(cum_last) + jnp.where(same_head, upd, 0.0)
    y = z.T[c:2 * c, :]
    return y, new_state


def _dot_split_lhs_left(m_bf16, x):
    out = None
    for p in _split_bf16(x, 3):
        d = jnp.dot(m_bf16, p, preferred_element_type=F32)
        out = d if out is None else out + d
    return out


def _wkv_kernel(r_ref, lw_ref, k_ref, v_ref, a_ref, b_ref, ltri_ref, eye_ref, y_ref, state_ref):
    @pl.when(pl.program_id(1) == 0)
    def _():
        state_ref[...] = jnp.zeros_like(state_ref)

    ltri = ltri_ref[...]
    eye = eye_ref[...]
    for p in range(RWKV_DIM // PAIR):
        sl = slice(p * PAIR, (p + 1) * PAIR)
        y, s_new = _wkv_pair(r_ref[0, :, sl], lw_ref[0, :, sl], k_ref[0, :, sl], v_ref[0, :, sl],
                             a_ref[0, :, sl], b_ref[0, :, sl], state_ref[p], ltri, eye)
        y_ref[0, :, sl] = y
        state_ref[p] = s_new


def _wkv(r, lw, k, v, a, b, batch, seq):
    c = WKV_CHUNK
    shp = (batch, seq, RWKV_DIM)
    args = [x.reshape(shp) for x in (r, lw, k, v, a, b)]
    ltri = jnp.tril(jnp.ones((c, c), F32)).astype(BF16)
    eye = jnp.eye(2 * c, dtype=F32)
    blk = pl.BlockSpec((1, c, RWKV_DIM), lambda bi, ci: (bi, ci, 0))
    y = pl.pallas_call(
        _wkv_kernel,
        grid=(batch, seq // c),
        in_specs=[blk] * 6 + [_const_spec((c, c)), _const_spec((2 * c, 2 * c))],
        out_specs=blk,
        out_shape=jax.ShapeDtypeStruct(shp, F32),
        scratch_shapes=[pltpu.VMEM((RWKV_DIM // PAIR, PAIR, PAIR), F32)],
        compiler_params=_params("arbitrary", "arbitrary"),
        name="wkv",
    )(*args, ltri, eye)
    return y.reshape(batch * seq, RWKV_DIM)


def _mla_prep_kernel(p_ref, cos_ref, sina_ref, sinb_ref, qn_ref, kvn_ref, wq_ref, wk_ref, wv_ref,
                     q_ref, k_ref, v_ref):
    x = p_ref[...]
    cos, sina, sinb = cos_ref[...], sina_ref[...], sinb_ref[...]
    scale = (QK_NOPE_DIM + QK_ROPE_DIM) ** -0.5
    qn = _rms(x[:, 0:Q_LORA_RANK], qn_ref[...]).astype(BF16)
    kvn = _rms(x[:, Q_LORA_RANK:Q_LORA_RANK + KV_LORA_RANK], kvn_ref[...]).astype(BF16)
    k_pe = _rope(x[:, Q_LORA_RANK + KV_LORA_RANK:MLA_PAD_COLS], cos, sina, sinb)
    q = jnp.dot(qn, wq_ref[...], preferred_element_type=F32)
    kf = jnp.dot(kvn, wk_ref[...], preferred_element_type=F32)
    v_ref[...] = jnp.dot(kvn, wv_ref[...], preferred_element_type=F32).astype(BF16)
    for h in range(MLA_HEADS):
        sl = slice(h * LANES, (h + 1) * LANES)
        q_ref[0, h] = (_rope(q[:, sl], cos, sina, sinb) * scale).astype(BF16)
        k_ref[0, h] = (kf[:, sl] + k_pe).astype(BF16)


def _mla_prep(p_mla, tables, prm, batch, seq, tm):
    t = p_mla.shape[0]
    per_seq = seq // tm
    tab = pl.BlockSpec((tm, LANES), lambda i: (i, 0))
    head_spec = pl.BlockSpec((1, MLA_HEADS, tm, LANES), lambda i: (i // per_seq, 0, i % per_seq, 0))
    head_shape = jax.ShapeDtypeStruct((batch, MLA_HEADS, seq, LANES), BF16)
    consts = [prm["q_norm"], prm["kv_norm"], prm["w_q"], prm["w_k"], prm["w_v"]]
    return pl.pallas_call(
        _mla_prep_kernel,
        grid=(t // tm,),
        in_specs=[pl.BlockSpec((tm, MLA_PAD_COLS), lambda i: (i, 0)), tab, tab, tab]
        + [_const_spec(c.shape) for c in consts],
        out_specs=[head_spec, head_spec, pl.BlockSpec((tm, MLA_HEADS * V_HEAD_DIM), lambda i: (i, 0))],
        out_shape=[head_shape, head_shape, jax.ShapeDtypeStruct((t, MLA_HEADS * V_HEAD_DIM), BF16)],
        compiler_params=_params("parallel"),
        name="mla_prep",
    )(p_mla, *tables, *consts)


def _attn_kernel(q_ref, k_ref, v_ref, o_ref, m_ref, l_ref, acc_ref):
    i = pl.program_id(2)
    j = pl.program_id(3)
    tq = q_ref.shape[2]
    tk = k_ref.shape[2]

    @pl.when(j == 0)
    def _():
        m_ref[...] = jnp.full_like(m_ref, -jnp.inf)
        l_ref[...] = jnp.zeros_like(l_ref)
        acc_ref[...] = jnp.zeros_like(acc_ref)

    def step(masked):
        v = v_ref[0]
        for h in range(2):
            s = lax.dot_general(q_ref[0, h], k_ref[0, h], (((1,), (1,)), ((), ())),
                                preferred_element_type=F32)
            if masked:
                q_pos = i * tq + lax.broadcasted_iota(jnp.int32, (tq, tk), 0)
                k_pos = j * tk + lax.broadcasted_iota(jnp.int32, (tq, tk), 1)
                s = jnp.where(k_pos <= q_pos, s, NEG_INF)
            m_old = m_ref[h]
            m_new = jnp.maximum(m_old, jnp.max(s, axis=1, keepdims=True))
            alpha = jnp.exp(m_old - m_new)
            p = jnp.exp(s - m_new)
            l_ref[h] = alpha * l_ref[h] + jnp.sum(p, axis=1, keepdims=True)
            acc_ref[h] = alpha * acc_ref[h] + jnp.dot(p.astype(BF16), v, preferred_element_type=F32)
            m_ref[h] = m_new

    @pl.when(j < i)
    def _():
        step(False)

    @pl.when(j == i)
    def _():
        step(True)
        lane = lax.broadcasted_iota(jnp.int32, (1, LANES), 1)
        o_ref[0] = jnp.where(lane < V_HEAD_DIM, acc_ref[0] / l_ref[0], acc_ref[1] / l_ref[1]).astype(BF16)


def _mla_attn(q, k, v, batch, seq, blk):
    n = seq // blk
    qspec = pl.BlockSpec((1, 2, blk, LANES), lambda b, p, i, j: (b, p, i, 0))
    kspec = pl.BlockSpec((1, 2, blk, LANES), lambda b, p, i, j: (b, p, jnp.minimum(j, i), 0))
    vspec = pl.BlockSpec((1, blk, LANES), lambda b, p, i, j: (b, jnp.minimum(j, i), p))
    ospec = pl.BlockSpec((1, blk, LANES), lambda b, p, i, j: (b, i, p))
    hv = MLA_HEADS * V_HEAD_DIM
    o = pl.pallas_call(
        _attn_kernel,
        grid=(batch, MLA_HEADS // 2, n, n),
        in_specs=[qspec, kspec, vspec],
        out_specs=ospec,
        out_shape=jax.ShapeDtypeStruct((batch, seq, hv), BF16),
        scratch_shapes=[pltpu.VMEM((2, blk, 1), F32), pltpu.VMEM((2, blk, 1), F32),
                        pltpu.VMEM((2, blk, LANES), F32)],
        compiler_params=_params("parallel", "parallel", "arbitrary", "arbitrary"),
        name="mla_attn",
    )(q, k, v.reshape(batch, seq, hv))
    return o.reshape(batch * seq, hv)


def _merge_kernel(y_ref, bonus_ref, gate_ref, o_ref, g_ref, h_ref, lnw_ref, lnb_ref, ones_ref,
                  woa_ref, wob_ref, wout_ref, out_ref):
    ones = ones_ref[...]
    inv_n = 1.0 / RWKV_HEAD_DIM
    y = y_ref[...]
    mu = _dot_split_lhs(y, ones, 2) * inv_n
    d = y - mu
    var = _dot_split_lhs(d * d, ones, 2) * inv_n
    yn = (d * lax.rsqrt(var + LNX_EPS)) * lnw_ref[...] + lnb_ref[...]
    z = (yn + bonus_ref[...]) * gate_ref[...]
    y_a = _dot(z, woa_ref[...])
    y_b = jnp.dot(o_ref[...], wob_ref[...], preferred_element_type=F32)
    g = g_ref[...]
    merged = jax.nn.sigmoid(g[:, 0:D_MODEL]) * y_a + jax.nn.sigmoid(g[:, D_MODEL:2 * D_MODEL]) * y_b
    out_ref[...] = h_ref[...] + _dot(merged, wout_ref[...])


def _merge(y, bonus, gate, o_mla, gates, h, prm, tm):
    t = h.shape[0]
    row = lambda c: pl.BlockSpec((tm, c), lambda i: (i, 0))
    consts = [prm["lnx_w"], prm["lnx_b"], prm["head_ones"], prm["w_oa"], prm["w_ob"], prm["w_out"]]
    return pl.pallas_call(
        _merge_kernel,
        grid=(t // tm,),
        in_specs=[row(RWKV_DIM)] * 4 + [row(2 * D_MODEL), row(D_MODEL)]
        + [_const_spec(c.shape) for c in consts],
        out_specs=row(D_MODEL),
        out_shape=jax.ShapeDtypeStruct((t, D_MODEL), F32),
        compiler_params=_params("parallel"),
        name="merge",
    )(y, bonus, gate, o_mla, gates, h, *consts)


def _ffn_kernel(blocks_per_seq, h_ref, g_ref, wg_ref, wv_ref, cw_ref, cb_ref, wd_ref, out_ref,
                n_ref, acc_ref, carry_ref):
    i = pl.program_id(0)
    f = pl.program_id(1)
    tm = h_ref.shape[0]

    @pl.when(f == 0)
    def _():
        n_ref[...] = _rms(h_ref[...], g_ref[...]).astype(BF16)
        acc_ref[...] = jnp.zeros_like(acc_ref)

    n = n_ref[...]
    u_gate = jnp.dot(n, wg_ref[...], preferred_element_type=F32)
    u_val = jnp.dot(n, wv_ref[...], preferred_element_type=F32)
    tail = carry_ref[f]
    tail = jnp.where(i % blocks_per_seq == 0, jnp.zeros_like(tail), tail)
    row = lax.broadcasted_iota(jnp.int32, (tm, 1), 0)
    back1 = jnp.where(row == 0, tail[SUBLANES - 1:SUBLANES], pltpu.roll(u_gate, 1, axis=0))
    back2 = jnp.where(row == 0, tail[SUBLANES - 2:SUBLANES - 1],
                      jnp.where(row == 1, tail[SUBLANES - 1:SUBLANES], pltpu.roll(u_gate, 2, axis=0)))
    cw = cw_ref[0]
    c = cb_ref[0] + cw[0:1] * back2
    c = c + cw[1:2] * back1
    c = c + cw[2:3] * u_gate
    carry_ref[f] = u_gate[tm - SUBLANES:tm]
    sqrt_half = np.sqrt(0.5).astype(np.float32)
    act = 0.5 * c * (1.0 + lax.erf(c * sqrt_half))
    acc_ref[...] += jnp.dot((act * u_val).astype(BF16), wd_ref[...], preferred_element_type=F32)

    @pl.when(f == pl.num_programs(1) - 1)
    def _():
        out_ref[...] = h_ref[...] + acc_ref[...]


def _conv_ffn(h, prm, seq, tm, fc):
    t = h.shape[0]
    nf = FFN_DIM // fc
    row = pl.BlockSpec((tm, D_MODEL), lambda i, f: (i, 0))
    return pl.pallas_call(
        functools.partial(_ffn_kernel, seq // tm),
        grid=(t // tm, nf),
        in_specs=[row, _const_spec((1, D_MODEL)),
                  pl.BlockSpec((D_MODEL, fc), lambda i, f: (0, f)),
                  pl.BlockSpec((D_MODEL, fc), lambda i, f: (0, nf + f)),
                  pl.BlockSpec((1, SUBLANES, fc), lambda i, f: (f, 0, 0)),
                  pl.BlockSpec((1, 1, fc), lambda i, f: (f, 0, 0)),
                  pl.BlockSpec((fc, D_MODEL), lambda i, f: (f, 0))],
        out_specs=row,
        out_shape=jax.ShapeDtypeStruct((t, D_MODEL), F32),
        scratch_shapes=[pltpu.VMEM((tm, D_MODEL), BF16), pltpu.VMEM((tm, D_MODEL), F32),
                        pltpu.VMEM((nf, SUBLANES, fc), F32)],
        compiler_params=_params("arbitrary", "arbitrary"),
        name="conv_ffn",
    )(h, prm["ffn_norm"], prm["w_up"], prm["w_up"], prm["conv_w"], prm["conv_b"], prm["w_down"])


def _final_norm_kernel(h_ref, g_ref, o_ref):
    o_ref[...] = _rms(h_ref[...], g_ref[...])


def _final_norm(h, g, tm):
    t = h.shape[0]
    row = pl.BlockSpec((tm, D_MODEL), lambda i: (i, 0))
    return pl.pallas_call(
        _final_norm_kernel,
        grid=(t // tm,),
        in_specs=[row, _const_spec((1, D_MODEL))],
        out_specs=row,
        out_shape=jax.ShapeDtypeStruct((t, D_MODEL), F32),
        compiler_params=_params("parallel"),
        name="final_norm",
    )(h, g.reshape(1, D_MODEL))


def _hi_lo(w):
    hi = w.astype(BF16)
    return hi, (w - hi.astype(F32)).astype(BF16)


def _layer_params(l, fc, attn_norm, w_in, mu_shift, decay_base, w_decay_up, iclr_base, w_iclr_up,
                  w_gate_up, k_k, k_a, r_k, lnx_w, lnx_b, w_out_rwkv, q_norm, w_q_up, kv_norm,
                  w_kv_up, w_out_mla, w_out, ffn_norm, w_ffn_up, conv_w, conv_b, w_ffn_down):
    row = lambda x: x.reshape(1, -1)
    w = w_in[l]
    w_mla_src = w[:, SHIFT_COLS:SHIFT_COLS + MLA_COLS]
    lat = Q_LORA_RANK + KV_LORA_RANK
    w_mla = jnp.zeros((D_MODEL, MLA_PAD_COLS), F32)
    w_mla = w_mla.at[:, 0:lat].set(w_mla_src[:, 0:lat])
    w_mla = w_mla.at[:, lat + ROPE_LANE0:lat + ROPE_LANE0 + QK_ROPE_DIM].set(w_mla_src[:, lat:])
    w_lora = jnp.zeros((DECAY_LORA + ICLR_LORA, 2 * RWKV_DIM), F32)
    w_lora = w_lora.at[0:DECAY_LORA, 0:RWKV_DIM].set(w_decay_up[l])
    w_lora = w_lora.at[DECAY_LORA:, RWKV_DIM:].set(w_iclr_up[l])
    wl_hi, wl_lo = _hi_lo(w_lora)
    wg_hi, wg_lo = _hi_lo(w_gate_up[l])
    head = jnp.arange(RWKV_DIM) // RWKV_HEAD_DIM
    head_ones = (head[:, None] == head[None, :]).astype(BF16)
    qk = QK_NOPE_DIM + QK_ROPE_DIM
    w_q = jnp.pad(w_q_up[l].reshape(Q_LORA_RANK, MLA_HEADS, qk), ((0, 0), (0, 0), (0, LANES - qk)))
    w_kv = w_kv_up[l].reshape(KV_LORA_RANK, MLA_HEADS, QK_NOPE_DIM + V_HEAD_DIM)
    w_k = jnp.pad(w_kv[:, :, :QK_NOPE_DIM], ((0, 0), (0, 0), (0, LANES - QK_NOPE_DIM)))
    nf = FFN_DIM // fc
    return dict(
        attn_norm=row(attn_norm[l]),
        w_rw=w[:, :SHIFT_COLS].astype(BF16),
        w_mla=w_mla.astype(BF16),
        w_g=w[:, SHIFT_COLS + MLA_COLS:].astype(BF16),
        mu=row(mu_shift[l]), wl_hi=wl_hi, wl_lo=wl_lo, wg_hi=wg_hi, wg_lo=wg_lo,
        decay_base=row(decay_base[l]), iclr_base=row(iclr_base[l]), k_k=row(k_k[l]), k_a=row(k_a[l]),
        r_k=row(r_k[l]), head_ones=head_ones, lnx_w=row(lnx_w[l]), lnx_b=row(lnx_b[l]),
        q_norm=row(q_norm[l]), kv_norm=row(kv_norm[l]),
        w_q=w_q.reshape(Q_LORA_RANK, MLA_HEADS * LANES).astype(BF16),
        w_k=w_k.reshape(KV_LORA_RANK, MLA_HEADS * LANES).astype(BF16),
        w_v=w_kv[:, :, QK_NOPE_DIM:].reshape(KV_LORA_RANK, MLA_HEADS * V_HEAD_DIM).astype(BF16),
        w_oa=w_out_rwkv[l].astype(BF16), w_ob=w_out_mla[l].astype(BF16), w_out=w_out[l].astype(BF16),
        ffn_norm=row(ffn_norm[l]), w_up=w_ffn_up[l].astype(BF16), w_down=w_ffn_down[l].astype(BF16),
        conv_w=jnp.pad(conv_w[l].reshape(3, nf, fc).transpose(1, 0, 2), ((0, 0), (0, SUBLANES - 3), (0, 0))),
        conv_b=conv_b[l].reshape(nf, 1, fc),
    )


def _tile_plan(seq):
    pick = lambda want: min(want, seq)
    return dict(proj=pick(256), prep=pick(256), mla=pick(256), attn=pick(512), merge=pick(256),
                ffn=pick(1024), ffn_cols=256, rope=pick(512), final=pick(512))


def kernel(x, positions, attn_norm, w_in, mu_shift, decay_base, w_decay_up, iclr_base, w_iclr_up, w_gate_up, k_k, k_a, r_k, lnx_w, lnx_b, w_out_rwkv, q_norm, w_q_up, kv_norm, w_kv_up, w_out_mla, w_out, ffn_norm, w_ffn_up, conv_w, conv_b, w_ffn_down, final_norm):
    batch, seq, _ = x.shape
    depth = w_in.shape[0]
    assert seq % WKV_CHUNK == 0 and x.shape[2] == D_MODEL
    plan = _tile_plan(seq)
    tables = _rope_tables(positions, plan["rope"])
    h = x.reshape(batch * seq, D_MODEL)
    for l in range(depth):
        prm = _layer_params(l, plan["ffn_cols"], attn_norm, w_in, mu_shift, decay_base, w_decay_up,
                            iclr_base, w_iclr_up, w_gate_up, k_k, k_a, r_k, lnx_w, lnx_b, w_out_rwkv,
                            q_norm, w_q_up, kv_norm, w_kv_up, w_out_mla, w_out, ffn_norm, w_ffn_up,
                            conv_w, conv_b, w_ffn_down)
        p_rw, p_mla, gates = _norm_inproj(h, prm["attn_norm"], prm["w_rw"], prm["w_mla"], prm["w_g"],
                                          plan["proj"])
        r, lw, k, v, a, b, gate, bonus = _rwkv_prep(p_rw, prm, seq, plan["prep"])
        y = _wkv(r, lw, k, v, a, b, batch, seq)
        q_h, k_h, v_h = _mla_prep(p_mla, tables, prm, batch, seq, plan["mla"])
        o_mla = _mla_attn(q_h, k_h, v_h, batch, seq, plan["attn"])
        h = _merge(y, bonus, gate, o_mla, gates, h, prm, plan["merge"])
        h = _conv_ffn(h, prm, seq, plan["ffn"], plan["ffn_cols"])
    out = _final_norm(h, final_norm, plan["final"])
    return out.reshape(batch, seq, D_MODEL)
```

```python
import functools

import jax
import jax.numpy as jnp
import numpy as np
from jax import lax
from jax.experimental import pallas as pl
from jax.experimental.pallas import tpu as pltpu

F32 = jnp.float32
BF16 = jnp.bfloat16

D_MODEL = 1024
RWKV_HEADS = 8
RWKV_HEAD_DIM = 64
RWKV_DIM = RWKV_HEADS * RWKV_HEAD_DIM
DECAY_LORA = 64
ICLR_LORA = 64
GATE_LORA = 128
LNX_EPS = 64e-5
MLA_HEADS = 8
QK_NOPE_DIM = 64
QK_ROPE_DIM = 32
V_HEAD_DIM = 64
Q_LORA_RANK = 256
KV_LORA_RANK = 128
ROPE_THETA = 10000.0
NEG_INF = -1e30
FFN_DIM = 2816
NORM_EPS = 1e-6
SHIFT_COLS = 3 * RWKV_DIM + DECAY_LORA + ICLR_LORA + GATE_LORA
MLA_COLS = Q_LORA_RANK + KV_LORA_RANK + QK_ROPE_DIM

LANES = 128
SUBLANES = 8
VMEM_LIMIT_BYTES = 56 * 1024 * 1024

MLA_PAD_COLS = 4 * LANES
ROPE_LANE0 = QK_NOPE_DIM
WKV_CHUNK = 64
PAIR = 2 * RWKV_HEAD_DIM


def _params(*semantics, flags=None):
    return pltpu.CompilerParams(dimension_semantics=semantics, vmem_limit_bytes=VMEM_LIMIT_BYTES,
                                flags=flags)


def _dot(a, b):
    return jnp.dot(a.astype(BF16), b.astype(BF16), preferred_element_type=F32)


def _dot_nt(a, b):
    return lax.dot_general(a.astype(BF16), b.astype(BF16), (((1,), (1,)), ((), ())),
                           preferred_element_type=F32)


def _split_bf16(x, terms):
    parts = []
    rest = x
    for _ in range(terms):
        p = rest.astype(BF16)
        parts.append(p)
        rest = rest - p.astype(F32)
    return parts


def _dot_split_lhs(a, b_bf16, terms):
    out = None
    for p in _split_bf16(a, terms):
        d = jnp.dot(p, b_bf16, preferred_element_type=F32)
        out = d if out is None else out + d
    return out


def _dot_x3(a, b_hi, b_lo):
    a_hi, a_lo = _split_bf16(a, 2)
    return (jnp.dot(a_hi, b_hi, preferred_element_type=F32)
            + jnp.dot(a_hi, b_lo, preferred_element_type=F32)
            + jnp.dot(a_lo, b_hi, preferred_element_type=F32))


def _rms(x, g):
    ms = jnp.mean(x * x, axis=-1, keepdims=True)
    return (x * lax.rsqrt(ms + NORM_EPS)) * g


def _const_spec(shape):
    return pl.BlockSpec(shape, lambda *_: (0,) * len(shape))


def _rope_tables_kernel(pos_ref, invf_ref, cos_ref, sina_ref, sinb_ref):
    ang = pos_ref[...] * invf_ref[...]
    lane = lax.broadcasted_iota(jnp.int32, ang.shape, 1)
    half = QK_ROPE_DIM // 2
    first = (lane >= ROPE_LANE0) & (lane < ROPE_LANE0 + half)
    second = (lane >= ROPE_LANE0 + half) & (lane < ROPE_LANE0 + QK_ROPE_DIM)
    c = jnp.cos(ang)
    s = jnp.sin(ang)
    cos_ref[...] = jnp.where(first | second, c, 1.0)
    sina_ref[...] = jnp.where(first, -s, 0.0)
    sinb_ref[...] = jnp.where(second, s, 0.0)


def _rope_tables(positions, tm):
    t = positions.size
    half = QK_ROPE_DIM // 2
    inv_freq = jnp.power(ROPE_THETA, -jnp.arange(0, QK_ROPE_DIM, 2, dtype=F32) / QK_ROPE_DIM)
    invf = jnp.zeros((1, LANES), F32)
    invf = invf.at[0, ROPE_LANE0:ROPE_LANE0 + half].set(inv_freq)
    invf = invf.at[0, ROPE_LANE0 + half:ROPE_LANE0 + QK_ROPE_DIM].set(inv_freq)
    posb = jnp.broadcast_to(positions.reshape(t, 1).astype(F32), (t, LANES))
    spec = pl.BlockSpec((tm, LANES), lambda i: (i, 0))
    out = jax.ShapeDtypeStruct((t, LANES), F32)
    return pl.pallas_call(
        _rope_tables_kernel,
        grid=(t // tm,),
        in_specs=[spec, _const_spec((1, LANES))],
        out_specs=[spec, spec, spec],
        out_shape=[out, out, out],
        compiler_params=_params("parallel"),
        name="rope_tables",
    )(posb, invf)


def _rope(tile, cos, sina, sinb):
    half = QK_ROPE_DIM // 2
    return (tile * cos + pltpu.roll(tile, LANES - half, axis=1) * sina
            + pltpu.roll(tile, half, axis=1) * sinb)


def _norm_inproj_kernel(x_ref, g_ref, wrw_ref, wmla_ref, wg_ref, prw_ref, pmla_ref, pg_ref):
    n = _rms(x_ref[...], g_ref[...]).astype(BF16)
    prw_ref[...] = jnp.dot(n, wrw_ref[...], preferred_element_type=F32)
    pmla_ref[...] = jnp.dot(n, wmla_ref[...], preferred_element_type=F32)
    pg_ref[...] = jnp.dot(n, wg_ref[...], preferred_element_type=F32)


def _norm_inproj(h, g, w_rw, w_mla, w_g, tm):
    t = h.shape[0]
    row = lambda c: pl.BlockSpec((tm, c), lambda i: (i, 0))
    return pl.pallas_call(
        _norm_inproj_kernel,
        grid=(t // tm,),
        in_specs=[row(D_MODEL), _const_spec((1, D_MODEL)), _const_spec(w_rw.shape),
                  _const_spec(w_mla.shape), _const_spec(w_g.shape)],
        out_specs=[row(SHIFT_COLS), row(MLA_PAD_COLS), row(2 * D_MODEL)],
        out_shape=[jax.ShapeDtypeStruct((t, SHIFT_COLS), F32),
                   jax.ShapeDtypeStruct((t, MLA_PAD_COLS), F32),
                   jax.ShapeDtypeStruct((t, 2 * D_MODEL), F32)],
        compiler_params=_params("parallel"),
        name="norm_inproj",
    )(h, g, w_rw, w_mla, w_g)


def _rwkv_prep_kernel(blocks_per_seq, p_ref, pprev_ref, mu_ref, wl_hi_ref, wl_lo_ref, wg_hi_ref,
                      wg_lo_ref, dbase_ref, ibase_ref, kk_ref, ka_ref, rk_ref, ones_ref,
                      r_ref, lw_ref, k_ref, v_ref, a_ref, b_ref, gate_ref, bonus_ref):
    i = pl.program_id(0)
    x = p_ref[...]
    tm = x.shape[0]
    prev = pprev_ref[SUBLANES - 1:SUBLANES, :]
    prev = jnp.where(i % blocks_per_seq == 0, jnp.zeros_like(prev), prev)
    row = lax.broadcasted_iota(jnp.int32, (tm, 1), 0)
    shifted = jnp.where(row == 0, prev, pltpu.roll(x, 1, axis=0))
    xs = x + (shifted - x) * mu_ref[...]

    c0 = 3 * RWKV_DIM
    p_r = xs[:, 0:RWKV_DIM]
    p_k = xs[:, RWKV_DIM:2 * RWKV_DIM]
    p_v = xs[:, 2 * RWKV_DIM:c0]
    p_wa = xs[:, c0:c0 + DECAY_LORA + ICLR_LORA]
    p_g = xs[:, c0 + DECAY_LORA + ICLR_LORA:SHIFT_COLS]

    lane = lax.broadcasted_iota(jnp.int32, p_wa.shape, 1)
    lora_in = jnp.where(lane < DECAY_LORA, jnp.tanh(p_wa), p_wa)
    lora = _dot_x3(lora_in, wl_hi_ref[...], wl_lo_ref[...])
    log_w = -jax.nn.softplus(-(dbase_ref[...] + lora[:, 0:RWKV_DIM])) - 0.5
    iclr = jax.nn.sigmoid(ibase_ref[...] + lora[:, RWKV_DIM:2 * RWKV_DIM])
    gate = _dot_x3(jax.nn.sigmoid(p_g), wg_hi_ref[...], wg_lo_ref[...])

    ones = ones_ref[...]
    kk = p_k * kk_ref[...]
    kk = kk * lax.rsqrt(_dot_split_lhs(kk * kk, ones, 2) + 1e-12)
    k = p_k * (1.0 + (iclr - 1.0) * ka_ref[...])

    r_ref[...] = p_r
    lw_ref[...] = -jnp.exp(log_w)
    k_ref[...] = k
    v_ref[...] = p_v
    a_ref[...] = -kk
    b_ref[...] = kk * iclr
    gate_ref[...] = gate
    bonus_ref[...] = _dot_split_lhs(p_r * k * rk_ref[...], ones, 2) * p_v


def _rwkv_prep(p_rw, prm, seq, tm):
    t = p_rw.shape[0]
    row = pl.BlockSpec((tm, RWKV_DIM), lambda i: (i, 0))
    out = jax.ShapeDtypeStruct((t, RWKV_DIM), F32)
    prev_spec = pl.BlockSpec((SUBLANES, SHIFT_COLS),
                             lambda i: (jnp.maximum(i * (tm // SUBLANES) - 1, 0), 0))
    consts = [prm["mu"], prm["wl_hi"], prm["wl_lo"], prm["wg_hi"], prm["wg_lo"], prm["decay_base"],
              prm["iclr_base"], prm["k_k"], prm["k_a"], prm["r_k"], prm["head_ones"]]
    return pl.pallas_call(
        functools.partial(_rwkv_prep_kernel, seq // tm),
        grid=(t // tm,),
        in_specs=[pl.BlockSpec((tm, SHIFT_COLS), lambda i: (i, 0)), prev_spec]
        + [_const_spec(c.shape) for c in consts],
        out_specs=[row] * 8,
        out_shape=[out] * 8,
        compiler_params=_params("parallel"),
        name="rwkv_prep",
    )(p_rw, p_rw, *consts)


def _wkv_pair(load, state_ref, idx, store_y, ltri, eye):
    c = WKV_CHUNK
    lane = lax.broadcasted_iota(jnp.int32, (1, PAIR), 1)
    lo = lane < RWKV_HEAD_DIM
    r, lw, k, v, a, b = load()

    cum = _dot_split_lhs_left(ltri, lw)
    yield
    cum_last = cum[c - 1:c, :]
    g_inv = jnp.exp(-cum)
    a_t = a * jnp.exp(cum - lw)
    r_t = r * jnp.exp(cum)
    zero = jnp.zeros_like(a_t)
    lhs = jnp.concatenate([jnp.where(lo, a_t, zero), jnp.where(lo, zero, a_t),
                           jnp.where(lo, r_t, zero), jnp.where(lo, zero, r_t)], axis=0)
    b_t = b * g_inv
    k_t = k * g_inv
    a_b = _dot_nt(lhs, jnp.concatenate([b_t, b_t], axis=0))
    a_k = _dot_nt(lhs, jnp.concatenate([k_t, k_t], axis=0))
    yield

    rr = lax.broadcasted_iota(jnp.int32, (2 * c, 2 * c), 0)
    cc = lax.broadcasted_iota(jnp.int32, (2 * c, 2 * c), 1)
    same_head = (rr // c) == (cc // c)
    strict = same_head & ((cc % c) < (rr % c))
    incl = same_head & ((cc % c) <= (rr % c))
    l_bd = jnp.where(strict, a_b[0:2 * c], 0.0)
    aak_bd = jnp.where(strict, a_k[0:2 * c], 0.0)
    arb_bd = jnp.where(incl, a_b[2 * c:4 * c], 0.0)
    ark_bd = jnp.where(incl, a_k[2 * c:4 * c], 0.0)
    v_st = jnp.concatenate([jnp.where(lo, v, zero), jnp.where(lo, zero, v)], axis=0)
    av = _dot(jnp.concatenate([aak_bd, ark_bd], axis=0), v_st)
    q_st = av[0:2 * c]
    arkv_st = av[2 * c:4 * c]

    t_bd = eye + l_bd
    p = _dot(l_bd, l_bd)
    yield
    n = 2
    while 2 * n < c:
        t_bd = t_bd + _dot(p, t_bd)
        p = _dot(p, p)
        yield
        n *= 2
    t_bd = t_bd + _dot(p, t_bd)
    yield

    tq_ta = _dot(t_bd, jnp.concatenate([lhs[0:2 * c], q_st], axis=1))
    yield
    rq_ra = _dot(arb_bd, tq_ta)
    yield
    ta_st = tq_ta[:, 0:PAIR]
    tq_st = tq_ta[:, PAIR:2 * PAIR]
    rh_st = lhs[2 * c:4 * c] + rq_ra[:, 0:PAIR]
    yh_st = rq_ra[:, PAIR:2 * PAIR] + arkv_st
    unstack = lambda x: x[0:c] + x[c:2 * c]
    ta = unstack(ta_st)
    rh = unstack(rh_st)
    tq = unstack(tq_st)
    yh = unstack(yh_st)

    state = state_ref[idx]
    z = _dot_nt(state, jnp.concatenate([ta, rh], axis=0))
    yield
    z = z + jnp.concatenate([tq, yh], axis=0).T
    v_t = jnp.concatenate([v, v], axis=0).T
    lane2 = lax.broadcasted_iota(jnp.int32, (1, 2 * c), 1)
    g_out = jnp.exp(cum_last - cum)
    upd = _dot(jnp.where(lane2 < c, z, v_t), jnp.concatenate([b * g_out, k * g_out], axis=0))
    yield
    state_ref[idx] = state * jnp.exp(cum_last) + jnp.where(same_head, upd, 0.0)
    store_y(z.T[c:2 * c, :])


def _dot_split_lhs_left(m_bf16, x):
    out = None
    for p in _split_bf16(x, 3):
        d = jnp.dot(m_bf16, p, preferred_element_type=F32)
        out = d if out is None else out + d
    return out


def _wkv_kernel(r_ref, lw_ref, k_ref, v_ref, a_ref, b_ref, ltri_ref, eye_ref, y_ref, state_ref):
    @pl.when(pl.program_id(0) == 0)
    def _():
        state_ref[...] = jnp.zeros_like(state_ref)

    ltri = ltri_ref[...]
    eye = eye_ref[...]
    pairs = RWKV_DIM // PAIR
    chains = []
    for bi in range(r_ref.shape[0]):
        for p in range(pairs):
            sl = slice(p * PAIR, (p + 1) * PAIR)

            def load(bi=bi, sl=sl):
                return tuple(ref[bi, :, sl] for ref in (r_ref, lw_ref, k_ref, v_ref, a_ref, b_ref))

            def store_y(y, bi=bi, sl=sl):
                y_ref[bi, :, sl] = y

            chains.append(_wkv_pair(load, state_ref, bi * pairs + p, store_y, ltri, eye))
    while chains:
        chains = [ch for ch in chains if next(ch, StopIteration) is not StopIteration]


def _wkv(r, lw, k, v, a, b, batch, seq):
    c = WKV_CHUNK
    shp = (batch, seq, RWKV_DIM)
    args = [x.reshape(shp) for x in (r, lw, k, v, a, b)]
    ltri = jnp.tril(jnp.ones((c, c), F32)).astype(BF16)
    eye = jnp.eye(2 * c, dtype=F32)
    blk = pl.BlockSpec((batch, c, RWKV_DIM), lambda ci: (0, ci, 0))
    y = pl.pallas_call(
        _wkv_kernel,
        grid=(seq // c,),
        in_specs=[blk] * 6 + [_const_spec((c, c)), _const_spec((2 * c, 2 * c))],
        out_specs=blk,
        out_shape=jax.ShapeDtypeStruct(shp, F32),
        scratch_shapes=[pltpu.VMEM((batch * (RWKV_DIM // PAIR), PAIR, PAIR), F32)],
        compiler_params=_params("arbitrary"),
        name="wkv",
    )(*args, ltri, eye)
    return y.reshape(batch * seq, RWKV_DIM)


def _mla_prep_kernel(p_ref, cos_ref, sina_ref, sinb_ref, qn_ref, kvn_ref, wq_ref, wk_ref, wv_ref,
                     qt_ref, k_ref, vt_ref):
    x = p_ref[...]
    cos, sina, sinb = cos_ref[...], sina_ref[...], sinb_ref[...]
    scale = (QK_NOPE_DIM + QK_ROPE_DIM) ** -0.5
    qn = _rms(x[:, 0:Q_LORA_RANK], qn_ref[...]).astype(BF16)
    kvn = _rms(x[:, Q_LORA_RANK:Q_LORA_RANK + KV_LORA_RANK], kvn_ref[...]).astype(BF16)
    k_pe = _rope(x[:, Q_LORA_RANK + KV_LORA_RANK:MLA_PAD_COLS], cos, sina, sinb)
    q = jnp.dot(qn, wq_ref[...], preferred_element_type=F32)
    kf = jnp.dot(kvn, wk_ref[...], preferred_element_type=F32)
    v = jnp.dot(kvn, wv_ref[...], preferred_element_type=F32)
    for h in range(MLA_HEADS):
        sl = slice(h * LANES, (h + 1) * LANES)
        qt_ref[0, h] = (_rope(q[:, sl], cos, sina, sinb) * scale).T.astype(BF16)
        k_ref[0, h, 0] = (kf[:, sl] + k_pe).astype(BF16)
    for p in range(MLA_HEADS // 2):
        vt_ref[0, p, 0] = v[:, p * LANES:(p + 1) * LANES].T.astype(BF16)


def _mla_prep(p_mla, tables, prm, batch, seq, blk):
    t = p_mla.shape[0]
    n = seq // blk
    tab = pl.BlockSpec((blk, LANES), lambda i: (i, 0))
    consts = [prm["q_norm"], prm["kv_norm"], prm["w_q"], prm["w_k"], prm["w_v"]]
    return pl.pallas_call(
        _mla_prep_kernel,
        grid=(t // blk,),
        in_specs=[pl.BlockSpec((blk, MLA_PAD_COLS), lambda i: (i, 0)), tab, tab, tab]
        + [_const_spec(c.shape) for c in consts],
        out_specs=[pl.BlockSpec((1, MLA_HEADS, LANES, blk), lambda i: (i // n, 0, 0, i % n)),
                   pl.BlockSpec((1, MLA_HEADS, 1, blk, LANES), lambda i: (i // n, 0, i % n, 0, 0)),
                   pl.BlockSpec((1, MLA_HEADS // 2, 1, LANES, blk), lambda i: (i // n, 0, i % n, 0, 0))],
        out_shape=[jax.ShapeDtypeStruct((batch, MLA_HEADS, LANES, seq), BF16),
                   jax.ShapeDtypeStruct((batch, MLA_HEADS, n, blk, LANES), BF16),
                   jax.ShapeDtypeStruct((batch, MLA_HEADS // 2, n, LANES, blk), BF16)],
        compiler_params=_params("parallel"),
        name="mla_prep",
    )(p_mla, *tables, *consts)


def _attn_kernel(qt_ref, k_ref, vt_ref, o_ref, m_ref, l_ref, acc_ref):
    i = pl.program_id(2)
    blk = qt_ref.shape[3]
    m_ref[...] = jnp.full_like(m_ref, -jnp.inf)
    l_ref[...] = jnp.zeros_like(l_ref)
    acc_ref[...] = jnp.zeros_like(acc_ref)

    def block(j, masked):
        s = [jnp.dot(k_ref[0, h, j], qt_ref[0, h], preferred_element_type=F32) for h in range(2)]
        for h in range(2):
            sh = s[h]
            if masked:
                k_pos = lax.broadcasted_iota(jnp.int32, (blk, blk), 0)
                q_pos = lax.broadcasted_iota(jnp.int32, (blk, blk), 1)
                sh = jnp.where(k_pos <= q_pos, sh, NEG_INF)
            m_old = m_ref[h]
            m_new = jnp.maximum(m_old, jnp.max(sh, axis=0, keepdims=True))
            alpha = jnp.exp(m_old - m_new)
            p = jnp.exp(sh - m_new)
            l_ref[h] = alpha * l_ref[h] + jnp.sum(p, axis=0, keepdims=True)
            vt = vt_ref[0, 0, j, h * V_HEAD_DIM:(h + 1) * V_HEAD_DIM, :]
            acc_ref[h] = alpha * acc_ref[h] + jnp.dot(vt, p.astype(BF16), preferred_element_type=F32)
            m_ref[h] = m_new

    def body(j, carry):
        block(j, False)
        return carry

    lax.fori_loop(0, i, body, 0)
    block(i, True)
    o_t = jnp.concatenate([acc_ref[0] / l_ref[0], acc_ref[1] / l_ref[1]], axis=0)
    o_ref[0] = o_t.T.astype(BF16)


def _mla_attn(qt, k, vt, batch, seq, blk):
    n = seq // blk
    hv = MLA_HEADS * V_HEAD_DIM
    o = pl.pallas_call(
        _attn_kernel,
        grid=(batch, MLA_HEADS // 2, n),
        in_specs=[pl.BlockSpec((1, 2, LANES, blk), lambda b, p, i: (b, p, 0, i)),
                  pl.BlockSpec((1, 2, n, blk, LANES), lambda b, p, i: (b, p, 0, 0, 0)),
                  pl.BlockSpec((1, 1, n, LANES, blk), lambda b, p, i: (b, p, 0, 0, 0))],
        out_specs=pl.BlockSpec((1, blk, LANES), lambda b, p, i: (b, i, p)),
        out_shape=jax.ShapeDtypeStruct((batch, seq, hv), BF16),
        scratch_shapes=[pltpu.VMEM((2, 1, blk), F32), pltpu.VMEM((2, 1, blk), F32),
                        pltpu.VMEM((2, V_HEAD_DIM, blk), F32)],
        compiler_params=_params("parallel", "parallel", "arbitrary"),
        name="mla_attn",
    )(qt, k, vt)
    return o.reshape(batch * seq, hv)


def _merge_kernel(y_ref, bonus_ref, gate_ref, o_ref, g_ref, h_ref, lnw_ref, lnb_ref, ones_ref,
                  woa_ref, wob_ref, wout_ref, out_ref):
    ones = ones_ref[...]
    inv_n = 1.0 / RWKV_HEAD_DIM
    y = y_ref[...]
    mu = _dot_split_lhs(y, ones, 2) * inv_n
    d = y - mu
    var = _dot_split_lhs(d * d, ones, 2) * inv_n
    yn = (d * lax.rsqrt(var + LNX_EPS)) * lnw_ref[...] + lnb_ref[...]
    z = (yn + bonus_ref[...]) * gate_ref[...]
    y_a = _dot(z, woa_ref[...])
    y_b = jnp.dot(o_ref[...], wob_ref[...], preferred_element_type=F32)
    g = g_ref[...]
    merged = jax.nn.sigmoid(g[:, 0:D_MODEL]) * y_a + jax.nn.sigmoid(g[:, D_MODEL:2 * D_MODEL]) * y_b
    out_ref[...] = h_ref[...] + _dot(merged, wout_ref[...])


def _merge(y, bonus, gate, o_mla, gates, h, prm, tm):
    t = h.shape[0]
    row = lambda c: pl.BlockSpec((tm, c), lambda i: (i, 0))
    consts = [prm["lnx_w"], prm["lnx_b"], prm["head_ones"], prm["w_oa"], prm["w_ob"], prm["w_out"]]
    return pl.pallas_call(
        _merge_kernel,
        grid=(t // tm,),
        in_specs=[row(RWKV_DIM)] * 4 + [row(2 * D_MODEL), row(D_MODEL)]
        + [_const_spec(c.shape) for c in consts],
        out_specs=row(D_MODEL),
        out_shape=jax.ShapeDtypeStruct((t, D_MODEL), F32),
        compiler_params=_params("parallel"),
        name="merge",
    )(y, bonus, gate, o_mla, gates, h, *consts)


def _ffn_kernel(blocks_per_seq, h_ref, g_ref, wg_ref, wv_ref, cw_ref, cb_ref, wd_ref, out_ref,
                n_ref, acc_ref, carry_ref):
    i = pl.program_id(0)
    f = pl.program_id(1)
    tm = h_ref.shape[0]

    @pl.when(f == 0)
    def _():
        n_ref[...] = _rms(h_ref[...], g_ref[...]).astype(BF16)
        acc_ref[...] = jnp.zeros_like(acc_ref)

    n = n_ref[...]
    u_gate = jnp.dot(n, wg_ref[...], preferred_element_type=F32)
    u_val = jnp.dot(n, wv_ref[...], preferred_element_type=F32)
    tail = carry_ref[f]
    tail = jnp.where(i % blocks_per_seq == 0, jnp.zeros_like(tail), tail)
    row = lax.broadcasted_iota(jnp.int32, (tm, 1), 0)
    back1 = jnp.where(row == 0, tail[SUBLANES - 1:SUBLANES], pltpu.roll(u_gate, 1, axis=0))
    back2 = jnp.where(row == 0, tail[SUBLANES - 2:SUBLANES - 1],
                      jnp.where(row == 1, tail[SUBLANES - 1:SUBLANES], pltpu.roll(u_gate, 2, axis=0)))
    cw = cw_ref[0]
    c = cb_ref[0] + cw[0:1] * back2
    c = c + cw[1:2] * back1
    c = c + cw[2:3] * u_gate
    carry_ref[f] = u_gate[tm - SUBLANES:tm]
    sqrt_half = np.sqrt(0.5).astype(np.float32)
    act = 0.5 * c * (1.0 + lax.erf(c * sqrt_half))
    acc_ref[...] += jnp.dot((act * u_val).astype(BF16), wd_ref[...], preferred_element_type=F32)

    @pl.when(f == pl.num_programs(1) - 1)
    def _():
        out_ref[...] = h_ref[...] + acc_ref[...]


def _conv_ffn(h, prm, seq, tm, fc):
    t = h.shape[0]
    nf = FFN_DIM // fc
    row = pl.BlockSpec((tm, D_MODEL), lambda i, f: (i, 0))
    return pl.pallas_call(
        functools.partial(_ffn_kernel, seq // tm),
        grid=(t // tm, nf),
        in_specs=[row, _const_spec((1, D_MODEL)),
                  pl.BlockSpec((D_MODEL, fc), lambda i, f: (0, f)),
                  pl.BlockSpec((D_MODEL, fc), lambda i, f: (0, nf + f)),
                  pl.BlockSpec((1, SUBLANES, fc), lambda i, f: (f, 0, 0)),
                  pl.BlockSpec((1, 1, fc), lambda i, f: (f, 0, 0)),
                  pl.BlockSpec((fc, D_MODEL), lambda i, f: (f, 0))],
        out_specs=row,
        out_shape=jax.ShapeDtypeStruct((t, D_MODEL), F32),
        scratch_shapes=[pltpu.VMEM((tm, D_MODEL), BF16), pltpu.VMEM((tm, D_MODEL), F32),
                        pltpu.VMEM((nf, SUBLANES, fc), F32)],
        compiler_params=_params("arbitrary", "arbitrary"),
        name="conv_ffn",
    )(h, prm["ffn_norm"], prm["w_up"], prm["w_up"], prm["conv_w"], prm["conv_b"], prm["w_down"])


def _final_norm_kernel(h_ref, g_ref, o_ref):
    o_ref[...] = _rms(h_ref[...], g_ref[...])


def _final_norm(h, g, tm):
    t = h.shape[0]
    row = pl.BlockSpec((tm, D_MODEL), lambda i: (i, 0))
    return pl.pallas_call(
        _final_norm_kernel,
        grid=(t // tm,),
        in_specs=[row, _const_spec((1, D_MODEL))],
        out_specs=row,
        out_shape=jax.ShapeDtypeStruct((t, D_MODEL), F32),
        compiler_params=_params("parallel"),
        name="final_norm",
    )(h, g.reshape(1, D_MODEL))


def _hi_lo(w):
    hi = w.astype(BF16)
    return hi, (w - hi.astype(F32)).astype(BF16)


def _layer_params(l, fc, attn_norm, w_in, mu_shift, decay_base, w_decay_up, iclr_base, w_iclr_up,
                  w_gate_up, k_k, k_a, r_k, lnx_w, lnx_b, w_out_rwkv, q_norm, w_q_up, kv_norm,
                  w_kv_up, w_out_mla, w_out, ffn_norm, w_ffn_up, conv_w, conv_b, w_ffn_down):
    row = lambda x: x.reshape(1, -1)
    w = w_in[l]
    w_mla_src = w[:, SHIFT_COLS:SHIFT_COLS + MLA_COLS]
    lat = Q_LORA_RANK + KV_LORA_RANK
    w_mla = jnp.zeros((D_MODEL, MLA_PAD_COLS), F32)
    w_mla = w_mla.at[:, 0:lat].set(w_mla_src[:, 0:lat])
    w_mla = w_mla.at[:, lat + ROPE_LANE0:lat + ROPE_LANE0 + QK_ROPE_DIM].set(w_mla_src[:, lat:])
    w_lora = jnp.zeros((DECAY_LORA + ICLR_LORA, 2 * RWKV_DIM), F32)
    w_lora = w_lora.at[0:DECAY_LORA, 0:RWKV_DIM].set(w_decay_up[l])
    w_lora = w_lora.at[DECAY_LORA:, RWKV_DIM:].set(w_iclr_up[l])
    wl_hi, wl_lo = _hi_lo(w_lora)
    wg_hi, wg_lo = _hi_lo(w_gate_up[l])
    head = jnp.arange(RWKV_DIM) // RWKV_HEAD_DIM
    head_ones = (head[:, None] == head[None, :]).astype(BF16)
    qk = QK_NOPE_DIM + QK_ROPE_DIM
    w_q = jnp.pad(w_q_up[l].reshape(Q_LORA_RANK, MLA_HEADS, qk), ((0, 0), (0, 0), (0, LANES - qk)))
    w_kv = w_kv_up[l].reshape(KV_LORA_RANK, MLA_HEADS, QK_NOPE_DIM + V_HEAD_DIM)
    w_k = jnp.pad(w_kv[:, :, :QK_NOPE_DIM], ((0, 0), (0, 0), (0, LANES - QK_NOPE_DIM)))
    nf = FFN_DIM // fc
    return dict(
        attn_norm=row(attn_norm[l]),
        w_rw=w[:, :SHIFT_COLS].astype(BF16),
        w_mla=w_mla.astype(BF16),
        w_g=w[:, SHIFT_COLS + MLA_COLS:].astype(BF16),
        mu=row(mu_shift[l]), wl_hi=wl_hi, wl_lo=wl_lo, wg_hi=wg_hi, wg_lo=wg_lo,
        decay_base=row(decay_base[l]), iclr_base=row(iclr_base[l]), k_k=row(k_k[l]), k_a=row(k_a[l]),
        r_k=row(r_k[l]), head_ones=head_ones, lnx_w=row(lnx_w[l]), lnx_b=row(lnx_b[l]),
        q_norm=row(q_norm[l]), kv_norm=row(kv_norm[l]),
        w_q=w_q.reshape(Q_LORA_RANK, MLA_HEADS * LANES).astype(BF16),
        w_k=w_k.reshape(KV_LORA_RANK, MLA_HEADS * LANES).astype(BF16),
        w_v=w_kv[:, :, QK_NOPE_DIM:].reshape(KV_LORA_RANK, MLA_HEADS * V_HEAD_DIM).astype(BF16),
        w_oa=w_out_rwkv[l].astype(BF16), w_ob=w_out_mla[l].astype(BF16), w_out=w_out[l].astype(BF16),
        ffn_norm=row(ffn_norm[l]), w_up=w_ffn_up[l].astype(BF16), w_down=w_ffn_down[l].astype(BF16),
        conv_w=jnp.pad(conv_w[l].reshape(3, nf, fc).transpose(1, 0, 2), ((0, 0), (0, SUBLANES - 3), (0, 0))),
        conv_b=conv_b[l].reshape(nf, 1, fc),
    )


def _tile_plan(seq):
    pick = lambda want: min(want, seq)
    return dict(proj=pick(256), prep=pick(256), attn=pick(512), merge=pick(256),
                ffn=pick(1024), ffn_cols=256, rope=pick(512), final=pick(512))


def kernel(x, positions, attn_norm, w_in, mu_shift, decay_base, w_decay_up, iclr_base, w_iclr_up, w_gate_up, k_k, k_a, r_k, lnx_w, lnx_b, w_out_rwkv, q_norm, w_q_up, kv_norm, w_kv_up, w_out_mla, w_out, ffn_norm, w_ffn_up, conv_w, conv_b, w_ffn_down, final_norm):
    batch, seq, _ = x.shape
    depth = w_in.shape[0]
    assert seq % WKV_CHUNK == 0 and x.shape[2] == D_MODEL
    plan = _tile_plan(seq)
    tables = _rope_tables(positions, plan["rope"])
    h = x.reshape(batch * seq, D_MODEL)
    for l in range(depth):
        prm = _layer_params(l, plan["ffn_cols"], attn_norm, w_in, mu_shift, decay_base, w_decay_up,
                            iclr_base, w_iclr_up, w_gate_up, k_k, k_a, r_k, lnx_w, lnx_b, w_out_rwkv,
                            q_norm, w_q_up, kv_norm, w_kv_up, w_out_mla, w_out, ffn_norm, w_ffn_up,
                            conv_w, conv_b, w_ffn_down)
        p_rw, p_mla, gates = _norm_inproj(h, prm["attn_norm"], prm["w_rw"], prm["w_mla"], prm["w_g"],
                                          plan["proj"])
        r, lw, k, v, a, b, gate, bonus = _rwkv_prep(p_rw, prm, seq, plan["prep"])
        y = _wkv(r, lw, k, v, a, b, batch, seq)
        q_h, k_h, v_h = _mla_prep(p_mla, tables, prm, batch, seq, plan["attn"])
        o_mla = _mla_attn(q_h, k_h, v_h, batch, seq, plan["attn"])
        h = _merge(y, bonus, gate, o_mla, gates, h, prm, plan["merge"])
        h = _conv_ffn(h, prm, seq, plan["ffn"], plan["ffn_cols"])
    out = _final_norm(h, final_norm, plan["final"])
    return out.reshape(batch, seq, D_MODEL)
```

```python
import functools

import jax
import jax.numpy as jnp
import numpy as np
from jax import lax
from jax.experimental import pallas as pl
from jax.experimental.pallas import tpu as pltpu

F32 = jnp.float32
BF16 = jnp.bfloat16

D_MODEL = 1024
RWKV_HEADS = 8
RWKV_HEAD_DIM = 64
RWKV_DIM = RWKV_HEADS * RWKV_HEAD_DIM
DECAY_LORA = 64
ICLR_LORA = 64
GATE_LORA = 128
LNX_EPS = 64e-5
MLA_HEADS = 8
QK_NOPE_DIM = 64
QK_ROPE_DIM = 32
V_HEAD_DIM = 64
Q_LORA_RANK = 256
KV_LORA_RANK = 128
ROPE_THETA = 10000.0
NEG_INF = -1e30
FFN_DIM = 2816
NORM_EPS = 1e-6
SHIFT_COLS = 3 * RWKV_DIM + DECAY_LORA + ICLR_LORA + GATE_LORA
MLA_COLS = Q_LORA_RANK + KV_LORA_RANK + QK_ROPE_DIM

LANES = 128
SUBLANES = 8
VMEM_LIMIT_BYTES = 56 * 1024 * 1024

MLA_PAD_COLS = 4 * LANES
ROPE_LANE0 = QK_NOPE_DIM
WKV_CHUNK = 64
PAIR = 2 * RWKV_HEAD_DIM
LOG2_E = float(np.log2(np.e))
V_ONES_ROWS = 16
V_EXT_ROWS = V_HEAD_DIM + V_ONES_ROWS
FFN_SUB_ROWS = 256


def _params(*semantics, flags=None):
    return pltpu.CompilerParams(dimension_semantics=semantics, vmem_limit_bytes=VMEM_LIMIT_BYTES,
                                flags=flags)


def _dot(a, b):
    return jnp.dot(a.astype(BF16), b.astype(BF16), preferred_element_type=F32)


def _dot_nt(a, b):
    return lax.dot_general(a.astype(BF16), b.astype(BF16), (((1,), (1,)), ((), ())),
                           preferred_element_type=F32)


def _split_bf16(x, terms):
    parts = []
    rest = x
    for _ in range(terms):
        p = rest.astype(BF16)
        parts.append(p)
        rest = rest - p.astype(F32)
    return parts


def _dot_split_lhs(a, b_bf16, terms):
    out = None
    for p in _split_bf16(a, terms):
        d = jnp.dot(p, b_bf16, preferred_element_type=F32)
        out = d if out is None else out + d
    return out


def _dot_x3(a, b_hi, b_lo):
    a_hi, a_lo = _split_bf16(a, 2)
    return (jnp.dot(a_hi, b_hi, preferred_element_type=F32)
            + jnp.dot(a_hi, b_lo, preferred_element_type=F32)
            + jnp.dot(a_lo, b_hi, preferred_element_type=F32))


def _rms(x, g):
    ms = jnp.mean(x * x, axis=-1, keepdims=True)
    return (x * lax.rsqrt(ms + NORM_EPS)) * g


def _const_spec(shape):
    return pl.BlockSpec(shape, lambda *_: (0,) * len(shape))


def _rope_tables_kernel(pos_ref, invf_ref, cos_ref, sina_ref, sinb_ref):
    ang = pos_ref[...] * invf_ref[...]
    lane = lax.broadcasted_iota(jnp.int32, ang.shape, 1)
    half = QK_ROPE_DIM // 2
    first = (lane >= ROPE_LANE0) & (lane < ROPE_LANE0 + half)
    second = (lane >= ROPE_LANE0 + half) & (lane < ROPE_LANE0 + QK_ROPE_DIM)
    c = jnp.cos(ang)
    s = jnp.sin(ang)
    cos_ref[...] = jnp.where(first | second, c, 1.0)
    sina_ref[...] = jnp.where(first, -s, 0.0)
    sinb_ref[...] = jnp.where(second, s, 0.0)


def _rope_tables(positions, tm):
    t = positions.size
    half = QK_ROPE_DIM // 2
    inv_freq = jnp.power(ROPE_THETA, -jnp.arange(0, QK_ROPE_DIM, 2, dtype=F32) / QK_ROPE_DIM)
    invf = jnp.zeros((1, LANES), F32)
    invf = invf.at[0, ROPE_LANE0:ROPE_LANE0 + half].set(inv_freq)
    invf = invf.at[0, ROPE_LANE0 + half:ROPE_LANE0 + QK_ROPE_DIM].set(inv_freq)
    posb = jnp.broadcast_to(positions.reshape(t, 1).astype(F32), (t, LANES))
    spec = pl.BlockSpec((tm, LANES), lambda i: (i, 0))
    out = jax.ShapeDtypeStruct((t, LANES), F32)
    return pl.pallas_call(
        _rope_tables_kernel,
        grid=(t // tm,),
        in_specs=[spec, _const_spec((1, LANES))],
        out_specs=[spec, spec, spec],
        out_shape=[out, out, out],
        compiler_params=_params("parallel"),
        name="rope_tables",
    )(posb, invf)


def _rope(tile, cos, sina, sinb):
    half = QK_ROPE_DIM // 2
    return (tile * cos + pltpu.roll(tile, LANES - half, axis=1) * sina
            + pltpu.roll(tile, half, axis=1) * sinb)


def _norm_inproj_kernel(x_ref, g_ref, wrw_ref, wmla_ref, wg_ref, prw_ref, pmla_ref, pg_ref):
    n = _rms(x_ref[...], g_ref[...]).astype(BF16)
    prw_ref[...] = jnp.dot(n, wrw_ref[...], preferred_element_type=F32)
    pmla_ref[...] = jnp.dot(n, wmla_ref[...], preferred_element_type=F32)
    pg_ref[...] = jnp.dot(n, wg_ref[...], preferred_element_type=F32)


def _norm_inproj(h, g, w_rw, w_mla, w_g, tm):
    t = h.shape[0]
    row = lambda c: pl.BlockSpec((tm, c), lambda i: (i, 0))
    return pl.pallas_call(
        _norm_inproj_kernel,
        grid=(t // tm,),
        in_specs=[row(D_MODEL), _const_spec((1, D_MODEL)), _const_spec(w_rw.shape),
                  _const_spec(w_mla.shape), _const_spec(w_g.shape)],
        out_specs=[row(SHIFT_COLS), row(MLA_PAD_COLS), row(2 * D_MODEL)],
        out_shape=[jax.ShapeDtypeStruct((t, SHIFT_COLS), F32),
                   jax.ShapeDtypeStruct((t, MLA_PAD_COLS), F32),
                   jax.ShapeDtypeStruct((t, 2 * D_MODEL), F32)],
        compiler_params=_params("parallel"),
        name="norm_inproj",
    )(h, g, w_rw, w_mla, w_g)


def _rwkv_prep_kernel(blocks_per_seq, p_ref, pprev_ref, mu_ref, wl_hi_ref, wl_lo_ref, wg_hi_ref,
                      wg_lo_ref, dbase_ref, ibase_ref, kk_ref, ka_ref, rk_ref, ones_ref,
                      r_ref, lw_ref, k_ref, v_ref, a_ref, b_ref, gate_ref, bonus_ref):
    i = pl.program_id(0)
    x = p_ref[...]
    tm = x.shape[0]
    prev = pprev_ref[SUBLANES - 1:SUBLANES, :]
    prev = jnp.where(i % blocks_per_seq == 0, jnp.zeros_like(prev), prev)
    row = lax.broadcasted_iota(jnp.int32, (tm, 1), 0)
    shifted = jnp.where(row == 0, prev, pltpu.roll(x, 1, axis=0))
    xs = x + (shifted - x) * mu_ref[...]

    c0 = 3 * RWKV_DIM
    p_r = xs[:, 0:RWKV_DIM]
    p_k = xs[:, RWKV_DIM:2 * RWKV_DIM]
    p_v = xs[:, 2 * RWKV_DIM:c0]
    p_wa = xs[:, c0:c0 + DECAY_LORA + ICLR_LORA]
    p_g = xs[:, c0 + DECAY_LORA + ICLR_LORA:SHIFT_COLS]

    lane = lax.broadcasted_iota(jnp.int32, p_wa.shape, 1)
    lora_in = jnp.where(lane < DECAY_LORA, jnp.tanh(p_wa), p_wa)
    lora = _dot_x3(lora_in, wl_hi_ref[...], wl_lo_ref[...])
    log_w = -jax.nn.softplus(-(dbase_ref[...] + lora[:, 0:RWKV_DIM])) - 0.5
    iclr = jax.nn.sigmoid(ibase_ref[...] + lora[:, RWKV_DIM:2 * RWKV_DIM])
    gate = _dot_x3(jax.nn.sigmoid(p_g), wg_hi_ref[...], wg_lo_ref[...])

    ones = ones_ref[...]
    kk = p_k * kk_ref[...]
    kk = kk * lax.rsqrt(_dot_split_lhs(kk * kk, ones, 2) + 1e-12)
    k = p_k * (1.0 + (iclr - 1.0) * ka_ref[...])

    r_ref[...] = p_r
    lw_ref[...] = -jnp.exp(log_w)
    k_ref[...] = k
    v_ref[...] = p_v
    a_ref[...] = -kk
    b_ref[...] = kk * iclr
    gate_ref[...] = gate
    bonus_ref[...] = _dot_split_lhs(p_r * k * rk_ref[...], ones, 2) * p_v


def _rwkv_prep(p_rw, prm, seq, tm):
    t = p_rw.shape[0]
    row = pl.BlockSpec((tm, RWKV_DIM), lambda i: (i, 0))
    out = jax.ShapeDtypeStruct((t, RWKV_DIM), F32)
    prev_spec = pl.BlockSpec((SUBLANES, SHIFT_COLS),
                             lambda i: (jnp.maximum(i * (tm // SUBLANES) - 1, 0), 0))
    consts = [prm["mu"], prm["wl_hi"], prm["wl_lo"], prm["wg_hi"], prm["wg_lo"], prm["decay_base"],
              prm["iclr_base"], prm["k_k"], prm["k_a"], prm["r_k"], prm["head_ones"]]
    return pl.pallas_call(
        functools.partial(_rwkv_prep_kernel, seq // tm),
        grid=(t // tm,),
        in_specs=[pl.BlockSpec((tm, SHIFT_COLS), lambda i: (i, 0)), prev_spec]
        + [_const_spec(c.shape) for c in consts],
        out_specs=[row] * 8,
        out_shape=[out] * 8,
        compiler_params=_params("parallel"),
        name="rwkv_prep",
    )(p_rw, p_rw, *consts)


def _wkv_pair(load, state_ref, idx, store_y, ltri, eye):
    c = WKV_CHUNK
    lane = lax.broadcasted_iota(jnp.int32, (1, PAIR), 1)
    lo = lane < RWKV_HEAD_DIM
    r, lw, k, v, a, b = load()

    cum = _dot_split_lhs_left(ltri, lw)
    yield
    cum_last = cum[c - 1:c, :]
    g_inv = jnp.exp(-cum)
    a_t = a * jnp.exp(cum - lw)
    r_t = r * jnp.exp(cum)
    zero = jnp.zeros_like(a_t)
    lhs = jnp.concatenate([jnp.where(lo, a_t, zero), jnp.where(lo, zero, a_t),
                           jnp.where(lo, r_t, zero), jnp.where(lo, zero, r_t)], axis=0)
    b_t = b * g_inv
    k_t = k * g_inv
    a_b = _dot_nt(lhs, jnp.concatenate([b_t, b_t], axis=0))
    a_k = _dot_nt(lhs, jnp.concatenate([k_t, k_t], axis=0))
    yield

    rr = lax.broadcasted_iota(jnp.int32, (2 * c, 2 * c), 0)
    cc = lax.broadcasted_iota(jnp.int32, (2 * c, 2 * c), 1)
    same_head = (rr // c) == (cc // c)
    strict = same_head & ((cc % c) < (rr % c))
    incl = same_head & ((cc % c) <= (rr % c))
    l_bd = jnp.where(strict, a_b[0:2 * c], 0.0)
    aak_bd = jnp.where(strict, a_k[0:2 * c], 0.0)
    arb_bd = jnp.where(incl, a_b[2 * c:4 * c], 0.0)
    ark_bd = jnp.where(incl, a_k[2 * c:4 * c], 0.0)
    v_st = jnp.concatenate([jnp.where(lo, v, zero), jnp.where(lo, zero, v)], axis=0)
    av = _dot(jnp.concatenate([aak_bd, ark_bd], axis=0), v_st)
    q_st = av[0:2 * c]
    arkv_st = av[2 * c:4 * c]

    t_bd = eye + l_bd
    p = _dot(l_bd, l_bd)
    yield
    n = 2
    while 2 * n < c:
        t_bd = t_bd + _dot(p, t_bd)
        p = _dot(p, p)
        yield
        n *= 2
    t_bd = t_bd + _dot(p, t_bd)
    yield

    tq_ta = _dot(t_bd, jnp.concatenate([lhs[0:2 * c], q_st], axis=1))
    yield
    rq_ra = _dot(arb_bd, tq_ta)
    yield
    ta_st = tq_ta[:, 0:PAIR]
    tq_st = tq_ta[:, PAIR:2 * PAIR]
    rh_st = lhs[2 * c:4 * c] + rq_ra[:, 0:PAIR]
    yh_st = rq_ra[:, PAIR:2 * PAIR] + arkv_st
    unstack = lambda x: x[0:c] + x[c:2 * c]
    ta = unstack(ta_st)
    rh = unstack(rh_st)
    tq = unstack(tq_st)
    yh = unstack(yh_st)

    state = state_ref[idx]
    z = _dot_nt(state, jnp.concatenate([ta, rh], axis=0))
    yield
    z = z + jnp.concatenate([tq, yh], axis=0).T
    v_t = jnp.concatenate([v, v], axis=0).T
    lane2 = lax.broadcasted_iota(jnp.int32, (1, 2 * c), 1)
    g_out = jnp.exp(cum_last - cum)
    upd = _dot(jnp.where(lane2 < c, z, v_t), jnp.concatenate([b * g_out, k * g_out], axis=0))
    yield
    state_ref[idx] = state * jnp.exp(cum_last) + jnp.where(same_head, upd, 0.0)
    store_y(z.T[c:2 * c, :])


def _dot_split_lhs_left(m_bf16, x):
    out = None
    for p in _split_bf16(x, 3):
        d = jnp.dot(m_bf16, p, preferred_element_type=F32)
        out = d if out is None else out + d
    return out


def _wkv_kernel(r_ref, lw_ref, k_ref, v_ref, a_ref, b_ref, ltri_ref, eye_ref, y_ref, state_ref):
    @pl.when(pl.program_id(0) == 0)
    def _():
        state_ref[...] = jnp.zeros_like(state_ref)

    ltri = ltri_ref[...]
    eye = eye_ref[...]
    pairs = RWKV_DIM // PAIR
    chains = []
    for bi in range(r_ref.shape[0]):
        for p in range(pairs):
            sl = slice(p * PAIR, (p + 1) * PAIR)

            def load(bi=bi, sl=sl):
                return tuple(ref[bi, :, sl] for ref in (r_ref, lw_ref, k_ref, v_ref, a_ref, b_ref))

            def store_y(y, bi=bi, sl=sl):
                y_ref[bi, :, sl] = y

            chains.append(_wkv_pair(load, state_ref, bi * pairs + p, store_y, ltri, eye))
    while chains:
        chains = [ch for ch in chains if next(ch, StopIteration) is not StopIteration]


def _wkv(r, lw, k, v, a, b, batch, seq):
    c = WKV_CHUNK
    shp = (batch, seq, RWKV_DIM)
    args = [x.reshape(shp) for x in (r, lw, k, v, a, b)]
    ltri = jnp.tril(jnp.ones((c, c), F32)).astype(BF16)
    eye = jnp.eye(2 * c, dtype=F32)
    blk = pl.BlockSpec((batch, c, RWKV_DIM), lambda ci: (0, ci, 0))
    y = pl.pallas_call(
        _wkv_kernel,
        grid=(seq // c,),
        in_specs=[blk] * 6 + [_const_spec((c, c)), _const_spec((2 * c, 2 * c))],
        out_specs=blk,
        out_shape=jax.ShapeDtypeStruct(shp, F32),
        scratch_shapes=[pltpu.VMEM((batch * (RWKV_DIM // PAIR), PAIR, PAIR), F32)],
        compiler_params=_params("arbitrary"),
        name="wkv",
    )(*args, ltri, eye)
    return y.reshape(batch * seq, RWKV_DIM)


def _mla_prep_kernel(p_ref, cos_ref, sina_ref, sinb_ref, qn_ref, kvn_ref, wq_ref, wk_ref, wv_ref,
                     qt_ref, k_ref, vt_ref):
    x = p_ref[...]
    cos, sina, sinb = cos_ref[...], sina_ref[...], sinb_ref[...]
    scale = (QK_NOPE_DIM + QK_ROPE_DIM) ** -0.5 * LOG2_E
    qn = _rms(x[:, 0:Q_LORA_RANK], qn_ref[...]).astype(BF16)
    kvn = _rms(x[:, Q_LORA_RANK:Q_LORA_RANK + KV_LORA_RANK], kvn_ref[...]).astype(BF16)
    k_pe = _rope(x[:, Q_LORA_RANK + KV_LORA_RANK:MLA_PAD_COLS], cos, sina, sinb)
    q = jnp.dot(qn, wq_ref[...], preferred_element_type=F32)
    kf = jnp.dot(kvn, wk_ref[...], preferred_element_type=F32)
    v = jnp.dot(kvn, wv_ref[...], preferred_element_type=F32)
    for h in range(MLA_HEADS):
        sl = slice(h * LANES, (h + 1) * LANES)
        qt_ref[0, h] = (_rope(q[:, sl], cos, sina, sinb) * scale).T.astype(BF16)
        k_ref[0, h, 0] = (kf[:, sl] + k_pe).astype(BF16)
    ones = jnp.ones((V_ONES_ROWS, x.shape[0]), F32)
    for p in range(MLA_HEADS // 2):
        vt = v[:, p * LANES:(p + 1) * LANES].T
        vt_ref[0, p, 0] = jnp.concatenate([vt[0:V_HEAD_DIM], ones, vt[V_HEAD_DIM:], ones], axis=0).astype(BF16)


def _mla_prep(p_mla, tables, prm, batch, seq, blk):
    t = p_mla.shape[0]
    n = seq // blk
    tab = pl.BlockSpec((blk, LANES), lambda i: (i, 0))
    consts = [prm["q_norm"], prm["kv_norm"], prm["w_q"], prm["w_k"], prm["w_v"]]
    return pl.pallas_call(
        _mla_prep_kernel,
        grid=(t // blk,),
        in_specs=[pl.BlockSpec((blk, MLA_PAD_COLS), lambda i: (i, 0)), tab, tab, tab]
        + [_const_spec(c.shape) for c in consts],
        out_specs=[pl.BlockSpec((1, MLA_HEADS, LANES, blk), lambda i: (i // n, 0, 0, i % n)),
                   pl.BlockSpec((1, MLA_HEADS, 1, blk, LANES), lambda i: (i // n, 0, i % n, 0, 0)),
                   pl.BlockSpec((1, MLA_HEADS // 2, 1, 2 * V_EXT_ROWS, blk), lambda i: (i // n, 0, i % n, 0, 0))],
        out_shape=[jax.ShapeDtypeStruct((batch, MLA_HEADS, LANES, seq), BF16),
                   jax.ShapeDtypeStruct((batch, MLA_HEADS, n, blk, LANES), BF16),
                   jax.ShapeDtypeStruct((batch, MLA_HEADS // 2, n, 2 * V_EXT_ROWS, blk), BF16)],
        compiler_params=_params("parallel"),
        name="mla_prep",
    )(p_mla, *tables, *consts)


def _attn_kernel(qt_ref, k_ref, vt_ref, o_ref, m_ref, acc_ref):
    i = pl.program_id(2)
    blk = qt_ref.shape[3]
    m_ref[...] = jnp.full_like(m_ref, -jnp.inf)
    acc_ref[...] = jnp.zeros_like(acc_ref)

    def process(blocks):
        s = [[jnp.dot(k_ref[0, h, j], qt_ref[0, h], preferred_element_type=F32) for j, _ in blocks]
             for h in range(2)]
        for h in range(2):
            sh = s[h]
            for u, (_, masked) in enumerate(blocks):
                if masked:
                    k_pos = lax.broadcasted_iota(jnp.int32, (blk, blk), 0)
                    q_pos = lax.broadcasted_iota(jnp.int32, (blk, blk), 1)
                    sh[u] = jnp.where(k_pos <= q_pos, sh[u], NEG_INF)
            m_old = m_ref[h]
            m_new = m_old
            for su in sh:
                m_new = jnp.maximum(m_new, jnp.max(su, axis=0, keepdims=True))
            acc = jnp.exp2(m_old - m_new) * acc_ref[h]
            for su, (j, _) in zip(sh, blocks):
                vt = vt_ref[0, 0, j, h * V_EXT_ROWS:(h + 1) * V_EXT_ROWS, :]
                acc = acc + jnp.dot(vt, jnp.exp2(su - m_new).astype(BF16), preferred_element_type=F32)
            acc_ref[h] = acc
            m_ref[h] = m_new

    def body(jj, carry):
        process([(2 * jj, False), (2 * jj + 1, False)])
        return carry

    lax.fori_loop(0, i // 2, body, 0)

    @pl.when(i % 2 == 1)
    def _():
        process([(i - 1, False), (i, True)])

    @pl.when(i % 2 == 0)
    def _():
        process([(i, True)])

    o_t = jnp.concatenate([acc_ref[h, 0:V_HEAD_DIM] / acc_ref[h, V_HEAD_DIM:V_HEAD_DIM + 1]
                           for h in range(2)], axis=0)
    o_ref[0] = o_t.T.astype(BF16)


def _mla_attn(qt, k, vt, batch, seq, blk):
    n = seq // blk
    hv = MLA_HEADS * V_HEAD_DIM
    o = pl.pallas_call(
        _attn_kernel,
        grid=(batch, MLA_HEADS // 2, n),
        in_specs=[pl.BlockSpec((1, 2, LANES, blk), lambda b, p, i: (b, p, 0, i)),
                  pl.BlockSpec((1, 2, n, blk, LANES), lambda b, p, i: (b, p, 0, 0, 0)),
                  pl.BlockSpec((1, 1, n, 2 * V_EXT_ROWS, blk), lambda b, p, i: (b, p, 0, 0, 0))],
        out_specs=pl.BlockSpec((1, blk, LANES), lambda b, p, i: (b, i, p)),
        out_shape=jax.ShapeDtypeStruct((batch, seq, hv), BF16),
        scratch_shapes=[pltpu.VMEM((2, 1, blk), F32), pltpu.VMEM((2, V_EXT_ROWS, blk), F32)],
        compiler_params=_params("parallel", "parallel", "arbitrary"),
        name="mla_attn",
    )(qt, k, vt)
    return o.reshape(batch * seq, hv)


def _merge_kernel(y_ref, bonus_ref, gate_ref, o_ref, g_ref, h_ref, lnw_ref, lnb_ref, ones_ref,
                  woa_ref, wob_ref, wout_ref, out_ref):
    ones = ones_ref[...]
    inv_n = 1.0 / RWKV_HEAD_DIM
    y = y_ref[...]
    mu = _dot_split_lhs(y, ones, 2) * inv_n
    d = y - mu
    var = _dot_split_lhs(d * d, ones, 2) * inv_n
    yn = (d * lax.rsqrt(var + LNX_EPS)) * lnw_ref[...] + lnb_ref[...]
    z = (yn + bonus_ref[...]) * gate_ref[...]
    y_a = _dot(z, woa_ref[...])
    y_b = jnp.dot(o_ref[...], wob_ref[...], preferred_element_type=F32)
    g = g_ref[...]
    merged = jax.nn.sigmoid(g[:, 0:D_MODEL]) * y_a + jax.nn.sigmoid(g[:, D_MODEL:2 * D_MODEL]) * y_b
    out_ref[...] = h_ref[...] + _dot(merged, wout_ref[...])


def _merge(y, bonus, gate, o_mla, gates, h, prm, tm):
    t = h.shape[0]
    row = lambda c: pl.BlockSpec((tm, c), lambda i: (i, 0))
    consts = [prm["lnx_w"], prm["lnx_b"], prm["head_ones"], prm["w_oa"], prm["w_ob"], prm["w_out"]]
    return pl.pallas_call(
        _merge_kernel,
        grid=(t // tm,),
        in_specs=[row(RWKV_DIM)] * 4 + [row(2 * D_MODEL), row(D_MODEL)]
        + [_const_spec(c.shape) for c in consts],
        out_specs=row(D_MODEL),
        out_shape=jax.ShapeDtypeStruct((t, D_MODEL), F32),
        compiler_params=_params("parallel"),
        name="merge",
    )(y, bonus, gate, o_mla, gates, h, *consts)


def _ffn_kernel(blocks_per_seq, final, h_ref, g_ref, wg_ref, wv_ref, cw_ref, cb_ref, wd_ref, fg_ref,
                out_ref, n_ref, acc_ref, carry_ref):
    i = pl.program_id(0)
    f = pl.program_id(1)
    tm = h_ref.shape[0]
    rows = min(FFN_SUB_ROWS, tm)

    @pl.when(f == 0)
    def _():
        n_ref[...] = _rms(h_ref[...], g_ref[...]).astype(BF16)
        acc_ref[...] = jnp.zeros_like(acc_ref)

    def up(sb):
        n = n_ref[sb * rows:(sb + 1) * rows, :]
        return (jnp.dot(n, wg_ref[...], preferred_element_type=F32),
                jnp.dot(n, wv_ref[...], preferred_element_type=F32))

    tail = carry_ref[f]
    tail = jnp.where(i % blocks_per_seq == 0, jnp.zeros_like(tail), tail)
    cw = cw_ref[0]
    row = lax.broadcasted_iota(jnp.int32, (rows, 1), 0)
    sqrt_half = np.sqrt(0.5).astype(np.float32)
    nxt = up(0)
    for sb in range(tm // rows):
        u_gate, u_val = nxt
        if sb + 1 < tm // rows:
            nxt = up(sb + 1)
        back1 = jnp.where(row == 0, tail[SUBLANES - 1:SUBLANES], pltpu.roll(u_gate, 1, axis=0))
        back2 = jnp.where(row == 0, tail[SUBLANES - 2:SUBLANES - 1],
                          jnp.where(row == 1, tail[SUBLANES - 1:SUBLANES], pltpu.roll(u_gate, 2, axis=0)))
        c = cb_ref[0] + cw[0:1] * back2
        c = c + cw[1:2] * back1
        c = c + cw[2:3] * u_gate
        tail = u_gate[rows - SUBLANES:rows]
        act = 0.5 * c * (1.0 + lax.erf(c * sqrt_half))
        down = jnp.dot((act * u_val).astype(BF16), wd_ref[...], preferred_element_type=F32)
        acc_ref[sb * rows:(sb + 1) * rows, :] += down
    carry_ref[f] = tail

    @pl.when(f == pl.num_programs(1) - 1)
    def _():
        out = h_ref[...] + acc_ref[...]
        out_ref[...] = _rms(out, fg_ref[...]) if final else out


def _conv_ffn(h, prm, final_gain, seq, tm, fc):
    t = h.shape[0]
    nf = FFN_DIM // fc
    row = pl.BlockSpec((tm, D_MODEL), lambda i, f: (i, 0))
    final = final_gain is not None
    fg = final_gain.reshape(1, D_MODEL) if final else prm["ffn_norm"]
    return pl.pallas_call(
        functools.partial(_ffn_kernel, seq // tm, final),
        grid=(t // tm, nf),
        in_specs=[row, _const_spec((1, D_MODEL)),
                  pl.BlockSpec((D_MODEL, fc), lambda i, f: (0, f)),
                  pl.BlockSpec((D_MODEL, fc), lambda i, f: (0, nf + f)),
                  pl.BlockSpec((1, SUBLANES, fc), lambda i, f: (f, 0, 0)),
                  pl.BlockSpec((1, 1, fc), lambda i, f: (f, 0, 0)),
                  pl.BlockSpec((fc, D_MODEL), lambda i, f: (f, 0)),
                  _const_spec((1, D_MODEL))],
        out_specs=row,
        out_shape=jax.ShapeDtypeStruct((t, D_MODEL), F32),
        scratch_shapes=[pltpu.VMEM((tm, D_MODEL), BF16), pltpu.VMEM((tm, D_MODEL), F32),
                        pltpu.VMEM((nf, SUBLANES, fc), F32)],
        compiler_params=_params("arbitrary", "arbitrary"),
        name="conv_ffn",
    )(h, prm["ffn_norm"], prm["w_up"], prm["w_up"], prm["conv_w"], prm["conv_b"], prm["w_down"], fg)


def _hi_lo(w):
    hi = w.astype(BF16)
    return hi, (w - hi.astype(F32)).astype(BF16)


def _layer_params(l, fc, attn_norm, w_in, mu_shift, decay_base, w_decay_up, iclr_base, w_iclr_up,
                  w_gate_up, k_k, k_a, r_k, lnx_w, lnx_b, w_out_rwkv, q_norm, w_q_up, kv_norm,
                  w_kv_up, w_out_mla, w_out, ffn_norm, w_ffn_up, conv_w, conv_b, w_ffn_down):
    row = lambda x: x.reshape(1, -1)
    w = w_in[l]
    w_mla_src = w[:, SHIFT_COLS:SHIFT_COLS + MLA_COLS]
    lat = Q_LORA_RANK + KV_LORA_RANK
    w_mla = jnp.zeros((D_MODEL, MLA_PAD_COLS), F32)
    w_mla = w_mla.at[:, 0:lat].set(w_mla_src[:, 0:lat])
    w_mla = w_mla.at[:, lat + ROPE_LANE0:lat + ROPE_LANE0 + QK_ROPE_DIM].set(w_mla_src[:, lat:])
    w_lora = jnp.zeros((DECAY_LORA + ICLR_LORA, 2 * RWKV_DIM), F32)
    w_lora = w_lora.at[0:DECAY_LORA, 0:RWKV_DIM].set(w_decay_up[l])
    w_lora = w_lora.at[DECAY_LORA:, RWKV_DIM:].set(w_iclr_up[l])
    wl_hi, wl_lo = _hi_lo(w_lora)
    wg_hi, wg_lo = _hi_lo(w_gate_up[l])
    head = jnp.arange(RWKV_DIM) // RWKV_HEAD_DIM
    head_ones = (head[:, None] == head[None, :]).astype(BF16)
    qk = QK_NOPE_DIM + QK_ROPE_DIM
    w_q = jnp.pad(w_q_up[l].reshape(Q_LORA_RANK, MLA_HEADS, qk), ((0, 0), (0, 0), (0, LANES - qk)))
    w_kv = w_kv_up[l].reshape(KV_LORA_RANK, MLA_HEADS, QK_NOPE_DIM + V_HEAD_DIM)
    w_k = jnp.pad(w_kv[:, :, :QK_NOPE_DIM], ((0, 0), (0, 0), (0, LANES - QK_NOPE_DIM)))
    nf = FFN_DIM // fc
    return dict(
        attn_norm=row(attn_norm[l]),
        w_rw=w[:, :SHIFT_COLS].astype(BF16),
        w_mla=w_mla.astype(BF16),
        w_g=w[:, SHIFT_COLS + MLA_COLS:].astype(BF16),
        mu=row(mu_shift[l]), wl_hi=wl_hi, wl_lo=wl_lo, wg_hi=wg_hi, wg_lo=wg_lo,
        decay_base=row(decay_base[l]), iclr_base=row(iclr_base[l]), k_k=row(k_k[l]), k_a=row(k_a[l]),
        r_k=row(r_k[l]), head_ones=head_ones, lnx_w=row(lnx_w[l]), lnx_b=row(lnx_b[l]),
        q_norm=row(q_norm[l]), kv_norm=row(kv_norm[l]),
        w_q=w_q.reshape(Q_LORA_RANK, MLA_HEADS * LANES).astype(BF16),
        w_k=w_k.reshape(KV_LORA_RANK, MLA_HEADS * LANES).astype(BF16),
        w_v=w_kv[:, :, QK_NOPE_DIM:].reshape(KV_LORA_RANK, MLA_HEADS * V_HEAD_DIM).astype(BF16),
        w_oa=w_out_rwkv[l].astype(BF16), w_ob=w_out_mla[l].astype(BF16), w_out=w_out[l].astype(BF16),
        ffn_norm=row(ffn_norm[l]), w_up=w_ffn_up[l].astype(BF16), w_down=w_ffn_down[l].astype(BF16),
        conv_w=jnp.pad(conv_w[l].reshape(3, nf, fc).transpose(1, 0, 2), ((0, 0), (0, SUBLANES - 3), (0, 0))),
        conv_b=conv_b[l].reshape(nf, 1, fc),
    )


def _tile_plan(seq):
    pick = lambda want: min(want, seq)
    return dict(proj=pick(256), prep=pick(256), attn=pick(512), merge=pick(256),
                ffn=pick(1024), ffn_cols=256, rope=pick(512))


def kernel(x, positions, attn_norm, w_in, mu_shift, decay_base, w_decay_up, iclr_base, w_iclr_up, w_gate_up, k_k, k_a, r_k, lnx_w, lnx_b, w_out_rwkv, q_norm, w_q_up, kv_norm, w_kv_up, w_out_mla, w_out, ffn_norm, w_ffn_up, conv_w, conv_b, w_ffn_down, final_norm):
    batch, seq, _ = x.shape
    depth = w_in.shape[0]
    assert seq % WKV_CHUNK == 0 and x.shape[2] == D_MODEL
    plan = _tile_plan(seq)
    tables = _rope_tables(positions, plan["rope"])
    h = x.reshape(batch * seq, D_MODEL)
    for l in range(depth):
        prm = _layer_params(l, plan["ffn_cols"], attn_norm, w_in, mu_shift, decay_base, w_decay_up,
                            iclr_base, w_iclr_up, w_gate_up, k_k, k_a, r_k, lnx_w, lnx_b, w_out_rwkv,
                            q_norm, w_q_up, kv_norm, w_kv_up, w_out_mla, w_out, ffn_norm, w_ffn_up,
                            conv_w, conv_b, w_ffn_down)
        p_rw, p_mla, gates = _norm_inproj(h, prm["attn_norm"], prm["w_rw"], prm["w_mla"], prm["w_g"],
                                          plan["proj"])
        r, lw, k, v, a, b, gate, bonus = _rwkv_prep(p_rw, prm, seq, plan["prep"])
        y = _wkv(r, lw, k, v, a, b, batch, seq)
        q_h, k_h, v_h = _mla_prep(p_mla, tables, prm, batch, seq, plan["attn"])
        o_mla = _mla_attn(q_h, k_h, v_h, batch, seq, plan["attn"])
        h = _merge(y, bonus, gate, o_mla, gates, h, prm, plan["merge"])
        last = final_norm if l == depth - 1 else None
        h = _conv_ffn(h, prm, last, seq, plan["ffn"], plan["ffn_cols"])
    return h.reshape(batch, seq, D_MODEL)
```

```python
import functools

import jax
import jax.numpy as jnp
import numpy as np
from jax import lax
from jax.experimental import pallas as pl
from jax.experimental.pallas import tpu as pltpu

F32 = jnp.float32
BF16 = jnp.bfloat16

D_MODEL = 1024
RWKV_HEADS = 8
RWKV_HEAD_DIM = 64
RWKV_DIM = RWKV_HEADS * RWKV_HEAD_DIM
DECAY_LORA = 64
ICLR_LORA = 64
GATE_LORA = 128
LNX_EPS = 64e-5
MLA_HEADS = 8
QK_NOPE_DIM = 64
QK_ROPE_DIM = 32
V_HEAD_DIM = 64
Q_LORA_RANK = 256
KV_LORA_RANK = 128
ROPE_THETA = 10000.0
NEG_INF = -1e30
FFN_DIM = 2816
NORM_EPS = 1e-6
SHIFT_COLS = 3 * RWKV_DIM + DECAY_LORA + ICLR_LORA + GATE_LORA
MLA_COLS = Q_LORA_RANK + KV_LORA_RANK + QK_ROPE_DIM

LANES = 128
SUBLANES = 8
VMEM_LIMIT_BYTES = 56 * 1024 * 1024

MLA_PAD_COLS = 4 * LANES
ROPE_LANE0 = QK_NOPE_DIM
WKV_CHUNK = 64
PAIR = 2 * RWKV_HEAD_DIM
LOG2_E = float(np.log2(np.e))
V_ONES_ROWS = 16
V_EXT_ROWS = V_HEAD_DIM + V_ONES_ROWS
FFN_SUB_ROWS = 256


def _params(*semantics, flags=None):
    return pltpu.CompilerParams(dimension_semantics=semantics, vmem_limit_bytes=VMEM_LIMIT_BYTES,
                                flags=flags)


def _dot(a, b):
    return jnp.dot(a.astype(BF16), b.astype(BF16), preferred_element_type=F32)


def _dot_nt(a, b):
    return lax.dot_general(a.astype(BF16), b.astype(BF16), (((1,), (1,)), ((), ())),
                           preferred_element_type=F32)


def _split_bf16(x, terms):
    parts = []
    rest = x
    for _ in range(terms):
        p = rest.astype(BF16)
        parts.append(p)
        rest = rest - p.astype(F32)
    return parts


def _dot_split_lhs(a, b_bf16, terms):
    out = None
    for p in _split_bf16(a, terms):
        d = jnp.dot(p, b_bf16, preferred_element_type=F32)
        out = d if out is None else out + d
    return out


def _dot_x3(a, b_hi, b_lo):
    a_hi, a_lo = _split_bf16(a, 2)
    return (jnp.dot(a_hi, b_hi, preferred_element_type=F32)
            + jnp.dot(a_hi, b_lo, preferred_element_type=F32)
            + jnp.dot(a_lo, b_hi, preferred_element_type=F32))


def _rms(x, g):
    ms = jnp.mean(x * x, axis=-1, keepdims=True)
    return (x * lax.rsqrt(ms + NORM_EPS)) * g


def _const_spec(shape):
    return pl.BlockSpec(shape, lambda *_: (0,) * len(shape))


def _rope_tables_kernel(pos_ref, invf_ref, cos_ref, sina_ref, sinb_ref):
    ang = pos_ref[...] * invf_ref[...]
    lane = lax.broadcasted_iota(jnp.int32, ang.shape, 1)
    half = QK_ROPE_DIM // 2
    first = (lane >= ROPE_LANE0) & (lane < ROPE_LANE0 + half)
    second = (lane >= ROPE_LANE0 + half) & (lane < ROPE_LANE0 + QK_ROPE_DIM)
    c = jnp.cos(ang)
    s = jnp.sin(ang)
    cos_ref[...] = jnp.where(first | second, c, 1.0)
    sina_ref[...] = jnp.where(first, -s, 0.0)
    sinb_ref[...] = jnp.where(second, s, 0.0)


def _rope_tables(positions, tm):
    t = positions.size
    half = QK_ROPE_DIM // 2
    inv_freq = jnp.power(ROPE_THETA, -jnp.arange(0, QK_ROPE_DIM, 2, dtype=F32) / QK_ROPE_DIM)
    invf = jnp.zeros((1, LANES), F32)
    invf = invf.at[0, ROPE_LANE0:ROPE_LANE0 + half].set(inv_freq)
    invf = invf.at[0, ROPE_LANE0 + half:ROPE_LANE0 + QK_ROPE_DIM].set(inv_freq)
    posb = jnp.broadcast_to(positions.reshape(t, 1).astype(F32), (t, LANES))
    spec = pl.BlockSpec((tm, LANES), lambda i: (i, 0))
    out = jax.ShapeDtypeStruct((t, LANES), F32)
    return pl.pallas_call(
        _rope_tables_kernel,
        grid=(t // tm,),
        in_specs=[spec, _const_spec((1, LANES))],
        out_specs=[spec, spec, spec],
        out_shape=[out, out, out],
        compiler_params=_params("parallel"),
        name="rope_tables",
    )(posb, invf)


def _rope(tile, cos, sina, sinb):
    half = QK_ROPE_DIM // 2
    return (tile * cos + pltpu.roll(tile, LANES - half, axis=1) * sina
            + pltpu.roll(tile, half, axis=1) * sinb)


def _inproj_kernel(blocks_per_seq, x_ref, g_ref, wrw_ref, wmla_ref, wgate_ref, mu_ref, wl_hi_ref,
                   wl_lo_ref, wg_hi_ref, wg_lo_ref, dbase_ref, ibase_ref, kk_ref, ka_ref, rk_ref, ones_ref,
                   pmla_ref, pg_ref, r_ref, lw_ref, k_ref, v_ref, a_ref, b_ref, gate_ref, bonus_ref,
                   tail_ref):
    i = pl.program_id(0)
    n = _rms(x_ref[...], g_ref[...]).astype(BF16)
    pmla_ref[...] = jnp.dot(n, wmla_ref[...], preferred_element_type=F32)
    pg_ref[...] = jnp.dot(n, wgate_ref[...], preferred_element_type=F32).astype(BF16)
    x = jnp.dot(n, wrw_ref[...], preferred_element_type=F32)
    tm = x.shape[0]
    prev = tail_ref[SUBLANES - 1:SUBLANES, :]
    prev = jnp.where(i % blocks_per_seq == 0, jnp.zeros_like(prev), prev)
    tail_ref[...] = x[tm - SUBLANES:tm]
    row = lax.broadcasted_iota(jnp.int32, (tm, 1), 0)
    shifted = jnp.where(row == 0, prev, pltpu.roll(x, 1, axis=0))
    xs = x + (shifted - x) * mu_ref[...]

    c0 = 3 * RWKV_DIM
    p_r = xs[:, 0:RWKV_DIM]
    p_k = xs[:, RWKV_DIM:2 * RWKV_DIM]
    p_v = xs[:, 2 * RWKV_DIM:c0]
    p_wa = xs[:, c0:c0 + DECAY_LORA + ICLR_LORA]
    p_g = xs[:, c0 + DECAY_LORA + ICLR_LORA:SHIFT_COLS]

    lane = lax.broadcasted_iota(jnp.int32, p_wa.shape, 1)
    lora_in = jnp.where(lane < DECAY_LORA, jnp.tanh(p_wa), p_wa)
    lora = _dot_x3(lora_in, wl_hi_ref[...], wl_lo_ref[...])
    log_w = -jax.nn.softplus(-(dbase_ref[...] + lora[:, 0:RWKV_DIM])) - 0.5
    iclr = jax.nn.sigmoid(ibase_ref[...] + lora[:, RWKV_DIM:2 * RWKV_DIM])
    gate = _dot_x3(jax.nn.sigmoid(p_g), wg_hi_ref[...], wg_lo_ref[...])

    ones = ones_ref[...]
    kk = p_k * kk_ref[...]
    kk = kk * lax.rsqrt(_dot_split_lhs(kk * kk, ones, 2) + 1e-12)
    k = p_k * (1.0 + (iclr - 1.0) * ka_ref[...])

    r_ref[...] = p_r
    lw_ref[...] = -jnp.exp(log_w)
    k_ref[...] = k
    v_ref[...] = p_v
    a_ref[...] = -kk
    b_ref[...] = kk * iclr
    gate_ref[...] = gate
    bonus_ref[...] = _dot_split_lhs(p_r * k * rk_ref[...], ones, 2) * p_v


def _inproj(h, prm, seq, tm):
    t = h.shape[0]
    row = lambda c: pl.BlockSpec((tm, c), lambda i: (i, 0))
    out = lambda c, dt: jax.ShapeDtypeStruct((t, c), dt)
    consts = [prm["attn_norm"], prm["w_rw"], prm["w_mla"], prm["w_g"], prm["mu"], prm["wl_hi"],
              prm["wl_lo"], prm["wg_hi"], prm["wg_lo"], prm["decay_base"], prm["iclr_base"], prm["k_k"],
              prm["k_a"], prm["r_k"], prm["head_ones"]]
    return pl.pallas_call(
        functools.partial(_inproj_kernel, seq // tm),
        grid=(t // tm,),
        in_specs=[row(D_MODEL)] + [_const_spec(c.shape) for c in consts],
        out_specs=[row(MLA_PAD_COLS), row(2 * D_MODEL)] + [row(RWKV_DIM)] * 8,
        out_shape=[out(MLA_PAD_COLS, F32), out(2 * D_MODEL, BF16)] + [out(RWKV_DIM, F32)] * 8,
        scratch_shapes=[pltpu.VMEM((SUBLANES, SHIFT_COLS), F32)],
        compiler_params=_params("arbitrary"),
        name="inproj",
    )(h, *consts)


def _wkv_pair(load, state_ref, idx, store_y, ltri, eye):
    c = WKV_CHUNK
    lane = lax.broadcasted_iota(jnp.int32, (1, PAIR), 1)
    lo = lane < RWKV_HEAD_DIM
    r, lw, k, v, a, b = load()

    cum = _dot_split_lhs_left(ltri, lw)
    yield
    cum_last = cum[c - 1:c, :]
    g_inv = jnp.exp(-cum)
    a_t = a * jnp.exp(cum - lw)
    r_t = r * jnp.exp(cum)
    zero = jnp.zeros_like(a_t)
    lhs = jnp.concatenate([jnp.where(lo, a_t, zero), jnp.where(lo, zero, a_t),
                           jnp.where(lo, r_t, zero), jnp.where(lo, zero, r_t)], axis=0)
    b_t = b * g_inv
    k_t = k * g_inv
    a_b = _dot_nt(lhs, jnp.concatenate([b_t, b_t], axis=0))
    a_k = _dot_nt(lhs, jnp.concatenate([k_t, k_t], axis=0))
    yield

    rr = lax.broadcasted_iota(jnp.int32, (2 * c, 2 * c), 0)
    cc = lax.broadcasted_iota(jnp.int32, (2 * c, 2 * c), 1)
    same_head = (rr // c) == (cc // c)
    strict = same_head & ((cc % c) < (rr % c))
    incl = same_head & ((cc % c) <= (rr % c))
    l_bd = jnp.where(strict, a_b[0:2 * c], 0.0)
    aak_bd = jnp.where(strict, a_k[0:2 * c], 0.0)
    arb_bd = jnp.where(incl, a_b[2 * c:4 * c], 0.0)
    ark_bd = jnp.where(incl, a_k[2 * c:4 * c], 0.0)
    v_st = jnp.concatenate([jnp.where(lo, v, zero), jnp.where(lo, zero, v)], axis=0)
    av = _dot(jnp.concatenate([aak_bd, ark_bd], axis=0), v_st)
    q_st = av[0:2 * c]
    arkv_st = av[2 * c:4 * c]

    t_bd = eye + l_bd
    p = _dot(l_bd, l_bd)
    yield
    n = 2
    while 2 * n < c:
        t_bd = t_bd + _dot(p, t_bd)
        p = _dot(p, p)
        yield
        n *= 2
    t_bd = t_bd + _dot(p, t_bd)
    yield

    tq_ta = _dot(t_bd, jnp.concatenate([lhs[0:2 * c], q_st], axis=1))
    yield
    rq_ra = _dot(arb_bd, tq_ta)
    yield
    ta_st = tq_ta[:, 0:PAIR]
    tq_st = tq_ta[:, PAIR:2 * PAIR]
    rh_st = lhs[2 * c:4 * c] + rq_ra[:, 0:PAIR]
    yh_st = rq_ra[:, PAIR:2 * PAIR] + arkv_st
    unstack = lambda x: x[0:c] + x[c:2 * c]
    ta = unstack(ta_st)
    rh = unstack(rh_st)
    tq = unstack(tq_st)
    yh = unstack(yh_st)

    state = state_ref[idx]
    z = _dot_nt(state, jnp.concatenate([ta, rh], axis=0))
    yield
    z = z + jnp.concatenate([tq, yh], axis=0).T
    v_t = jnp.concatenate([v, v], axis=0).T
    lane2 = lax.broadcasted_iota(jnp.int32, (1, 2 * c), 1)
    g_out = jnp.exp(cum_last - cum)
    upd = _dot(jnp.where(lane2 < c, z, v_t), jnp.concatenate([b * g_out, k * g_out], axis=0))
    yield
    state_ref[idx] = state * jnp.exp(cum_last) + jnp.where(same_head, upd, 0.0)
    store_y(z.T[c:2 * c, :])


def _dot_split_lhs_left(m_bf16, x):
    out = None
    for p in _split_bf16(x, 3):
        d = jnp.dot(m_bf16, p, preferred_element_type=F32)
        out = d if out is None else out + d
    return out


def _wkv_kernel(r_ref, lw_ref, k_ref, v_ref, a_ref, b_ref, ltri_ref, eye_ref, y_ref, state_ref):
    @pl.when(pl.program_id(0) == 0)
    def _():
        state_ref[...] = jnp.zeros_like(state_ref)

    ltri = ltri_ref[...]
    eye = eye_ref[...]
    pairs = RWKV_DIM // PAIR
    chains = []
    for bi in range(r_ref.shape[0]):
        for p in range(pairs):
            sl = slice(p * PAIR, (p + 1) * PAIR)

            def load(bi=bi, sl=sl):
                return tuple(ref[bi, :, sl] for ref in (r_ref, lw_ref, k_ref, v_ref, a_ref, b_ref))

            def store_y(y, bi=bi, sl=sl):
                y_ref[bi, :, sl] = y

            chains.append(_wkv_pair(load, state_ref, bi * pairs + p, store_y, ltri, eye))
    while chains:
        chains = [ch for ch in chains if next(ch, StopIteration) is not StopIteration]


def _wkv(r, lw, k, v, a, b, batch, seq):
    c = WKV_CHUNK
    shp = (batch, seq, RWKV_DIM)
    args = [x.reshape(shp) for x in (r, lw, k, v, a, b)]
    ltri = jnp.tril(jnp.ones((c, c), F32)).astype(BF16)
    eye = jnp.eye(2 * c, dtype=F32)
    blk = pl.BlockSpec((batch, c, RWKV_DIM), lambda ci: (0, ci, 0))
    y = pl.pallas_call(
        _wkv_kernel,
        grid=(seq // c,),
        in_specs=[blk] * 6 + [_const_spec((c, c)), _const_spec((2 * c, 2 * c))],
        out_specs=blk,
        out_shape=jax.ShapeDtypeStruct(shp, F32),
        scratch_shapes=[pltpu.VMEM((batch * (RWKV_DIM // PAIR), PAIR, PAIR), F32)],
        compiler_params=_params("arbitrary"),
        name="wkv",
    )(*args, ltri, eye)
    return y.reshape(batch * seq, RWKV_DIM)


def _mla_prep_kernel(p_ref, cos_ref, sina_ref, sinb_ref, qn_ref, kvn_ref, wq_ref, wk_ref, wv_ref,
                     qt_ref, k_ref, vt_ref):
    x = p_ref[...]
    cos, sina, sinb = cos_ref[...], sina_ref[...], sinb_ref[...]
    scale = (QK_NOPE_DIM + QK_ROPE_DIM) ** -0.5 * LOG2_E
    qn = _rms(x[:, 0:Q_LORA_RANK], qn_ref[...]).astype(BF16)
    kvn = _rms(x[:, Q_LORA_RANK:Q_LORA_RANK + KV_LORA_RANK], kvn_ref[...]).astype(BF16)
    k_pe = _rope(x[:, Q_LORA_RANK + KV_LORA_RANK:MLA_PAD_COLS], cos, sina, sinb)
    q = jnp.dot(qn, wq_ref[...], preferred_element_type=F32)
    kf = jnp.dot(kvn, wk_ref[...], preferred_element_type=F32)
    v = jnp.dot(kvn, wv_ref[...], preferred_element_type=F32)
    for h in range(MLA_HEADS):
        sl = slice(h * LANES, (h + 1) * LANES)
        qt_ref[0, h] = (_rope(q[:, sl], cos, sina, sinb) * scale).T.astype(BF16)
        k_ref[0, h, 0] = (kf[:, sl] + k_pe).astype(BF16)
    ones = jnp.ones((V_ONES_ROWS, x.shape[0]), F32)
    for p in range(MLA_HEADS // 2):
        vt = v[:, p * LANES:(p + 1) * LANES].T
        vt_ref[0, p, 0] = jnp.concatenate([vt[0:V_HEAD_DIM], ones, vt[V_HEAD_DIM:], ones], axis=0).astype(BF16)


def _mla_prep(p_mla, tables, prm, batch, seq, blk):
    t = p_mla.shape[0]
    n = seq // blk
    tab = pl.BlockSpec((blk, LANES), lambda i: (i, 0))
    consts = [prm["q_norm"], prm["kv_norm"], prm["w_q"], prm["w_k"], prm["w_v"]]
    return pl.pallas_call(
        _mla_prep_kernel,
        grid=(t // blk,),
        in_specs=[pl.BlockSpec((blk, MLA_PAD_COLS), lambda i: (i, 0)), tab, tab, tab]
        + [_const_spec(c.shape) for c in consts],
        out_specs=[pl.BlockSpec((1, MLA_HEADS, LANES, blk), lambda i: (i // n, 0, 0, i % n)),
                   pl.BlockSpec((1, MLA_HEADS, 1, blk, LANES), lambda i: (i // n, 0, i % n, 0, 0)),
                   pl.BlockSpec((1, MLA_HEADS // 2, 1, 2 * V_EXT_ROWS, blk), lambda i: (i // n, 0, i % n, 0, 0))],
        out_shape=[jax.ShapeDtypeStruct((batch, MLA_HEADS, LANES, seq), BF16),
                   jax.ShapeDtypeStruct((batch, MLA_HEADS, n, blk, LANES), BF16),
                   jax.ShapeDtypeStruct((batch, MLA_HEADS // 2, n, 2 * V_EXT_ROWS, blk), BF16)],
        compiler_params=_params("parallel"),
        name="mla_prep",
    )(p_mla, *tables, *consts)


def _attn_kernel(qt_ref, k_ref, vt_ref, o_ref, m_ref, acc_ref):
    i = pl.program_id(2)
    blk = qt_ref.shape[3]
    m_ref[...] = jnp.full_like(m_ref, -jnp.inf)
    acc_ref[...] = jnp.zeros_like(acc_ref)

    def process(blocks):
        s = [[jnp.dot(k_ref[0, h, j], qt_ref[0, h], preferred_element_type=F32) for j, _ in blocks]
             for h in range(2)]
        for h in range(2):
            sh = s[h]
            for u, (_, masked) in enumerate(blocks):
                if masked:
                    k_pos = lax.broadcasted_iota(jnp.int32, (blk, blk), 0)
                    q_pos = lax.broadcasted_iota(jnp.int32, (blk, blk), 1)
                    sh[u] = jnp.where(k_pos <= q_pos, sh[u], NEG_INF)
            m_old = m_ref[h]
            m_new = m_old
            for su in sh:
                m_new = jnp.maximum(m_new, jnp.max(su, axis=0, keepdims=True))
            acc = jnp.exp2(m_old - m_new) * acc_ref[h]
            for su, (j, _) in zip(sh, blocks):
                vt = vt_ref[0, 0, j, h * V_EXT_ROWS:(h + 1) * V_EXT_ROWS, :]
                acc = acc + jnp.dot(vt, jnp.exp2(su - m_new).astype(BF16), preferred_element_type=F32)
            acc_ref[h] = acc
            m_ref[h] = m_new

    def body(jj, carry):
        process([(2 * jj, False), (2 * jj + 1, False)])
        return carry

    lax.fori_loop(0, i // 2, body, 0)

    @pl.when(i % 2 == 1)
    def _():
        process([(i - 1, False), (i, True)])

    @pl.when(i % 2 == 0)
    def _():
        process([(i, True)])

    o_t = jnp.concatenate([acc_ref[h, 0:V_HEAD_DIM] / acc_ref[h, V_HEAD_DIM:V_HEAD_DIM + 1]
                           for h in range(2)], axis=0)
    o_ref[0] = o_t.T.astype(BF16)


def _mla_attn(qt, k, vt, batch, seq, blk):
    n = seq // blk
    hv = MLA_HEADS * V_HEAD_DIM
    o = pl.pallas_call(
        _attn_kernel,
        grid=(batch, MLA_HEADS // 2, n),
        in_specs=[pl.BlockSpec((1, 2, LANES, blk), lambda b, p, i: (b, p, 0, i)),
                  pl.BlockSpec((1, 2, n, blk, LANES), lambda b, p, i: (b, p, 0, 0, 0)),
                  pl.BlockSpec((1, 1, n, 2 * V_EXT_ROWS, blk), lambda b, p, i: (b, p, 0, 0, 0))],
        out_specs=pl.BlockSpec((1, blk, LANES), lambda b, p, i: (b, i, p)),
        out_shape=jax.ShapeDtypeStruct((batch, seq, hv), BF16),
        scratch_shapes=[pltpu.VMEM((2, 1, blk), F32), pltpu.VMEM((2, V_EXT_ROWS, blk), F32)],
        compiler_params=_params("parallel", "parallel", "arbitrary"),
        name="mla_attn",
    )(qt, k, vt)
    return o.reshape(batch * seq, hv)


def _merge_kernel(y_ref, bonus_ref, gate_ref, o_ref, g_ref, h_ref, lnw_ref, lnb_ref, ones_ref,
                  woa_ref, wob_ref, wout_ref, out_ref):
    ones = ones_ref[...]
    inv_n = 1.0 / RWKV_HEAD_DIM
    y = y_ref[...]
    mu = _dot_split_lhs(y, ones, 2) * inv_n
    d = y - mu
    var = _dot_split_lhs(d * d, ones, 2) * inv_n
    yn = (d * lax.rsqrt(var + LNX_EPS)) * lnw_ref[...] + lnb_ref[...]
    z = (yn + bonus_ref[...]) * gate_ref[...]
    y_a = _dot(z, woa_ref[...])
    y_b = jnp.dot(o_ref[...], wob_ref[...], preferred_element_type=F32)
    g = g_ref[...].astype(F32)
    merged = jax.nn.sigmoid(g[:, 0:D_MODEL]) * y_a + jax.nn.sigmoid(g[:, D_MODEL:2 * D_MODEL]) * y_b
    out_ref[...] = h_ref[...] + _dot(merged, wout_ref[...])


def _merge(y, bonus, gate, o_mla, gates, h, prm, tm):
    t = h.shape[0]
    row = lambda c: pl.BlockSpec((tm, c), lambda i: (i, 0))
    consts = [prm["lnx_w"], prm["lnx_b"], prm["head_ones"], prm["w_oa"], prm["w_ob"], prm["w_out"]]
    return pl.pallas_call(
        _merge_kernel,
        grid=(t // tm,),
        in_specs=[row(RWKV_DIM)] * 4 + [row(2 * D_MODEL), row(D_MODEL)]
        + [_const_spec(c.shape) for c in consts],
        out_specs=row(D_MODEL),
        out_shape=jax.ShapeDtypeStruct((t, D_MODEL), F32),
        compiler_params=_params("parallel"),
        name="merge",
    )(y, bonus, gate, o_mla, gates, h, *consts)


def _ffn_kernel(blocks_per_seq, final, h_ref, g_ref, wg_ref, wv_ref, cw_ref, cb_ref, wd_ref, fg_ref,
                out_ref, n_ref, acc_ref, carry_ref):
    i = pl.program_id(0)
    f = pl.program_id(1)
    tm = h_ref.shape[0]
    rows = min(FFN_SUB_ROWS, tm)

    @pl.when(f == 0)
    def _():
        n_ref[...] = _rms(h_ref[...], g_ref[...]).astype(BF16)
        acc_ref[...] = jnp.zeros_like(acc_ref)

    def up(sb):
        n = n_ref[sb * rows:(sb + 1) * rows, :]
        return (jnp.dot(n, wg_ref[...], preferred_element_type=F32),
                jnp.dot(n, wv_ref[...], preferred_element_type=F32))

    tail = carry_ref[f]
    tail = jnp.where(i % blocks_per_seq == 0, jnp.zeros_like(tail), tail)
    cw = cw_ref[0]
    row = lax.broadcasted_iota(jnp.int32, (rows, 1), 0)
    sqrt_half = np.sqrt(0.5).astype(np.float32)
    nxt = up(0)
    for sb in range(tm // rows):
        u_gate, u_val = nxt
        if sb + 1 < tm // rows:
            nxt = up(sb + 1)
        back1 = jnp.where(row == 0, tail[SUBLANES - 1:SUBLANES], pltpu.roll(u_gate, 1, axis=0))
        back2 = jnp.where(row == 0, tail[SUBLANES - 2:SUBLANES - 1],
                          jnp.where(row == 1, tail[SUBLANES - 1:SUBLANES], pltpu.roll(u_gate, 2, axis=0)))
        c = cb_ref[0] + cw[0:1] * back2
        c = c + cw[1:2] * back1
        c = c + cw[2:3] * u_gate
        tail = u_gate[rows - SUBLANES:rows]
        act = 0.5 * c * (1.0 + lax.erf(c * sqrt_half))
        down = jnp.dot((act * u_val).astype(BF16), wd_ref[...], preferred_element_type=F32)
        acc_ref[sb * rows:(sb + 1) * rows, :] += down
    carry_ref[f] = tail

    @pl.when(f == pl.num_programs(1) - 1)
    def _():
        out = h_ref[...] + acc_ref[...]
        out_ref[...] = _rms(out, fg_ref[...]) if final else out


def _conv_ffn(h, prm, final_gain, seq, tm, fc):
    t = h.shape[0]
    nf = FFN_DIM // fc
    row = pl.BlockSpec((tm, D_MODEL), lambda i, f: (i, 0))
    final = final_gain is not None
    fg = final_gain.reshape(1, D_MODEL) if final else prm["ffn_norm"]
    return pl.pallas_call(
        functools.partial(_ffn_kernel, seq // tm, final),
        grid=(t // tm, nf),
        in_specs=[row, _const_spec((1, D_MODEL)),
                  pl.BlockSpec((D_MODEL, fc), lambda i, f: (0, f)),
                  pl.BlockSpec((D_MODEL, fc), lambda i, f: (0, nf + f)),
                  pl.BlockSpec((1, SUBLANES, fc), lambda i, f: (f, 0, 0)),
                  pl.BlockSpec((1, 1, fc), lambda i, f: (f, 0, 0)),
                  pl.BlockSpec((fc, D_MODEL), lambda i, f: (f, 0)),
                  _const_spec((1, D_MODEL))],
        out_specs=row,
        out_shape=jax.ShapeDtypeStruct((t, D_MODEL), F32),
        scratch_shapes=[pltpu.VMEM((tm, D_MODEL), BF16), pltpu.VMEM((tm, D_MODEL), F32),
                        pltpu.VMEM((nf, SUBLANES, fc), F32)],
        compiler_params=_params("arbitrary", "arbitrary"),
        name="conv_ffn",
    )(h, prm["ffn_norm"], prm["w_up"], prm["w_up"], prm["conv_w"], prm["conv_b"], prm["w_down"], fg)


def _hi_lo(w):
    hi = w.astype(BF16)
    return hi, (w - hi.astype(F32)).astype(BF16)


def _layer_params(l, fc, attn_norm, w_in, mu_shift, decay_base, w_decay_up, iclr_base, w_iclr_up,
                  w_gate_up, k_k, k_a, r_k, lnx_w, lnx_b, w_out_rwkv, q_norm, w_q_up, kv_norm,
                  w_kv_up, w_out_mla, w_out, ffn_norm, w_ffn_up, conv_w, conv_b, w_ffn_down):
    row = lambda x: x.reshape(1, -1)
    w = w_in[l]
    w_mla_src = w[:, SHIFT_COLS:SHIFT_COLS + MLA_COLS]
    lat = Q_LORA_RANK + KV_LORA_RANK
    w_mla = jnp.zeros((D_MODEL, MLA_PAD_COLS), F32)
    w_mla = w_mla.at[:, 0:lat].set(w_mla_src[:, 0:lat])
    w_mla = w_mla.at[:, lat + ROPE_LANE0:lat + ROPE_LANE0 + QK_ROPE_DIM].set(w_mla_src[:, lat:])
    w_lora = jnp.zeros((DECAY_LORA + ICLR_LORA, 2 * RWKV_DIM), F32)
    w_lora = w_lora.at[0:DECAY_LORA, 0:RWKV_DIM].set(w_decay_up[l])
    w_lora = w_lora.at[DECAY_LORA:, RWKV_DIM:].set(w_iclr_up[l])
    wl_hi, wl_lo = _hi_lo(w_lora)
    wg_hi, wg_lo = _hi_lo(w_gate_up[l])
    head = jnp.arange(RWKV_DIM) // RWKV_HEAD_DIM
    head_ones = (head[:, None] == head[None, :]).astype(BF16)
    qk = QK_NOPE_DIM + QK_ROPE_DIM
    w_q = jnp.pad(w_q_up[l].reshape(Q_LORA_RANK, MLA_HEADS, qk), ((0, 0), (0, 0), (0, LANES - qk)))
    w_kv = w_kv_up[l].reshape(KV_LORA_RANK, MLA_HEADS, QK_NOPE_DIM + V_HEAD_DIM)
    w_k = jnp.pad(w_kv[:, :, :QK_NOPE_DIM], ((0, 0), (0, 0), (0, LANES - QK_NOPE_DIM)))
    nf = FFN_DIM // fc
    return dict(
        attn_norm=row(attn_norm[l]),
        w_rw=w[:, :SHIFT_COLS].astype(BF16),
        w_mla=w_mla.astype(BF16),
        w_g=w[:, SHIFT_COLS + MLA_COLS:].astype(BF16),
        mu=row(mu_shift[l]), wl_hi=wl_hi, wl_lo=wl_lo, wg_hi=wg_hi, wg_lo=wg_lo,
        decay_base=row(decay_base[l]), iclr_base=row(iclr_base[l]), k_k=row(k_k[l]), k_a=row(k_a[l]),
        r_k=row(r_k[l]), head_ones=head_ones, lnx_w=row(lnx_w[l]), lnx_b=row(lnx_b[l]),
        q_norm=row(q_norm[l]), kv_norm=row(kv_norm[l]),
        w_q=w_q.reshape(Q_LORA_RANK, MLA_HEADS * LANES).astype(BF16),
        w_k=w_k.reshape(KV_LORA_RANK, MLA_HEADS * LANES).astype(BF16),
        w_v=w_kv[:, :, QK_NOPE_DIM:].reshape(KV_LORA_RANK, MLA_HEADS * V_HEAD_DIM).astype(BF16),
        w_oa=w_out_rwkv[l].astype(BF16), w_ob=w_out_mla[l].astype(BF16), w_out=w_out[l].astype(BF16),
        ffn_norm=row(ffn_norm[l]), w_up=w_ffn_up[l].astype(BF16), w_down=w_ffn_down[l].astype(BF16),
        conv_w=jnp.pad(conv_w[l].reshape(3, nf, fc).transpose(1, 0, 2), ((0, 0), (0, SUBLANES - 3), (0, 0))),
        conv_b=conv_b[l].reshape(nf, 1, fc),
    )


def _tile_plan(seq):
    pick = lambda want: min(want, seq)
    return dict(proj=pick(256), attn=pick(512), merge=pick(256), ffn=pick(1024),
                ffn_cols=FFN_DIM // 2, rope=pick(512))


def kernel(x, positions, attn_norm, w_in, mu_shift, decay_base, w_decay_up, iclr_base, w_iclr_up, w_gate_up, k_k, k_a, r_k, lnx_w, lnx_b, w_out_rwkv, q_norm, w_q_up, kv_norm, w_kv_up, w_out_mla, w_out, ffn_norm, w_ffn_up, conv_w, conv_b, w_ffn_down, final_norm):
    batch, seq, _ = x.shape
    depth = w_in.shape[0]
    assert seq % WKV_CHUNK == 0 and x.shape[2] == D_MODEL
    plan = _tile_plan(seq)
    tables = _rope_tables(positions, plan["rope"])
    h = x.reshape(batch * seq, D_MODEL)
    for l in range(depth):
        prm = _layer_params(l, plan["ffn_cols"], attn_norm, w_in, mu_shift, decay_base, w_decay_up,
                            iclr_base, w_iclr_up, w_gate_up, k_k, k_a, r_k, lnx_w, lnx_b, w_out_rwkv,
                            q_norm, w_q_up, kv_norm, w_kv_up, w_out_mla, w_out, ffn_norm, w_ffn_up,
                            conv_w, conv_b, w_ffn_down)
        p_mla, gates, r, lw, k, v, a, b, gate, bonus = _inproj(h, prm, seq, plan["proj"])
        y = _wkv(r, lw, k, v, a, b, batch, seq)
        q_h, k_h, v_h = _mla_prep(p_mla, tables, prm, batch, seq, plan["attn"])
        o_mla = _mla_attn(q_h, k_h, v_h, batch, seq, plan["attn"])
        h = _merge(y, bonus, gate, o_mla, gates, h, prm, plan["merge"])
        last = final_norm if l == depth - 1 else None
        h = _conv_ffn(h, prm, last, seq, plan["ffn"], plan["ffn_cols"])
    return h.reshape(batch, seq, D_MODEL)
```

```python
import functools

import jax
import jax.numpy as jnp
import numpy as np
from jax import lax
from jax.experimental import pallas as pl
from jax.experimental.pallas import tpu as pltpu

F32 = jnp.float32
BF16 = jnp.bfloat16

D_MODEL = 1024
RWKV_HEADS = 8
RWKV_HEAD_DIM = 64
RWKV_DIM = RWKV_HEADS * RWKV_HEAD_DIM
DECAY_LORA = 64
ICLR_LORA = 64
GATE_LORA = 128
LNX_EPS = 64e-5
MLA_HEADS = 8
QK_NOPE_DIM = 64
QK_ROPE_DIM = 32
V_HEAD_DIM = 64
Q_LORA_RANK = 256
KV_LORA_RANK = 128
ROPE_THETA = 10000.0
NEG_INF = -1e30
FFN_DIM = 2816
NORM_EPS = 1e-6
SHIFT_COLS = 3 * RWKV_DIM + DECAY_LORA + ICLR_LORA + GATE_LORA
MLA_COLS = Q_LORA_RANK + KV_LORA_RANK + QK_ROPE_DIM

LANES = 128
SUBLANES = 8
MXU_WIDTH = 256
VMEM_LIMIT_BYTES = 56 * 1024 * 1024

MLA_PAD_COLS = 4 * LANES
ROPE_LANE0 = QK_NOPE_DIM
WKV_CHUNK = 64
PAIR = 2 * RWKV_HEAD_DIM
LOG2_E = float(np.log2(np.e))
V_ONES_ROWS = 16
V_EXT_ROWS = V_HEAD_DIM + V_ONES_ROWS
FFN_SUB_ROWS = 256


def _params(*semantics, flags=None):
    return pltpu.CompilerParams(dimension_semantics=semantics, vmem_limit_bytes=VMEM_LIMIT_BYTES,
                                flags=flags)


def _dot(a, b):
    return jnp.dot(a.astype(BF16), b.astype(BF16), preferred_element_type=F32)


def _dot_nt(a, b):
    return lax.dot_general(a.astype(BF16), b.astype(BF16), (((1,), (1,)), ((), ())),
                           preferred_element_type=F32)


def _split_bf16(x, terms):
    parts = []
    rest = x
    for _ in range(terms):
        p = rest.astype(BF16)
        parts.append(p)
        rest = rest - p.astype(F32)
    return parts


def _dot_split_lhs(a, b_bf16, terms):
    out = None
    for p in _split_bf16(a, terms):
        d = jnp.dot(p, b_bf16, preferred_element_type=F32)
        out = d if out is None else out + d
    return out


def _head_sums(x, ones):
    w = ones.shape[0]
    return jnp.concatenate([_dot_split_lhs(x[:, c:c + w], ones, 2) for c in range(0, x.shape[1], w)],
                           axis=1)


def _dot_x3(a, b_hi, b_lo):
    a_hi, a_lo = _split_bf16(a, 2)
    return (jnp.dot(a_hi, b_hi, preferred_element_type=F32)
            + jnp.dot(a_hi, b_lo, preferred_element_type=F32)
            + jnp.dot(a_lo, b_hi, preferred_element_type=F32))


def _rms(x, g):
    ms = jnp.mean(x * x, axis=-1, keepdims=True)
    return (x * lax.rsqrt(ms + NORM_EPS)) * g


def _const_spec(shape):
    return pl.BlockSpec(shape, lambda *_: (0,) * len(shape))


def _rope_tables_kernel(pos_ref, invf_ref, cos_ref, sina_ref, sinb_ref):
    ang = pos_ref[...] * invf_ref[...]
    lane = lax.broadcasted_iota(jnp.int32, ang.shape, 1)
    half = QK_ROPE_DIM // 2
    first = (lane >= ROPE_LANE0) & (lane < ROPE_LANE0 + half)
    second = (lane >= ROPE_LANE0 + half) & (lane < ROPE_LANE0 + QK_ROPE_DIM)
    c = jnp.cos(ang)
    s = jnp.sin(ang)
    cos_ref[...] = jnp.where(first | second, c, 1.0)
    sina_ref[...] = jnp.where(first, -s, 0.0)
    sinb_ref[...] = jnp.where(second, s, 0.0)


def _rope_tables(positions, tm):
    t = positions.size
    half = QK_ROPE_DIM // 2
    inv_freq = jnp.power(ROPE_THETA, -jnp.arange(0, QK_ROPE_DIM, 2, dtype=F32) / QK_ROPE_DIM)
    invf = jnp.zeros((1, LANES), F32)
    invf = invf.at[0, ROPE_LANE0:ROPE_LANE0 + half].set(inv_freq)
    invf = invf.at[0, ROPE_LANE0 + half:ROPE_LANE0 + QK_ROPE_DIM].set(inv_freq)
    posb = jnp.broadcast_to(positions.reshape(t, 1).astype(F32), (t, LANES))
    spec = pl.BlockSpec((tm, LANES), lambda i: (i, 0))
    out = jax.ShapeDtypeStruct((t, LANES), F32)
    return pl.pallas_call(
        _rope_tables_kernel,
        grid=(t // tm,),
        in_specs=[spec, _const_spec((1, LANES))],
        out_specs=[spec, spec, spec],
        out_shape=[out, out, out],
        compiler_params=_params("parallel"),
        name="rope_tables",
    )(posb, invf)


def _rope(tile, cos, sina, sinb):
    half = QK_ROPE_DIM // 2
    return (tile * cos + pltpu.roll(tile, LANES - half, axis=1) * sina
            + pltpu.roll(tile, half, axis=1) * sinb)


def _inproj_kernel(blocks_per_seq, x_ref, g_ref, wrw_ref, wmla_ref, wgate_ref, mu_ref, wl_hi_ref,
                   wl_lo_ref, wg_hi_ref, wg_lo_ref, dbase_ref, ibase_ref, kk_ref, ka_ref, rk_ref, ones_ref,
                   pmla_ref, pg_ref, r_ref, lw_ref, k_ref, v_ref, a_ref, b_ref, gate_ref, bonus_ref,
                   tail_ref):
    i = pl.program_id(0)
    n = _rms(x_ref[...], g_ref[...]).astype(BF16)
    pmla_ref[...] = jnp.dot(n, wmla_ref[...], preferred_element_type=F32)
    pg_ref[...] = jnp.dot(n, wgate_ref[...], preferred_element_type=F32).astype(BF16)
    x = jnp.dot(n, wrw_ref[...], preferred_element_type=F32)
    tm = x.shape[0]
    prev = tail_ref[SUBLANES - 1:SUBLANES, :]
    prev = jnp.where(i % blocks_per_seq == 0, jnp.zeros_like(prev), prev)
    tail_ref[...] = x[tm - SUBLANES:tm]
    row = lax.broadcasted_iota(jnp.int32, (tm, 1), 0)
    shifted = jnp.where(row == 0, prev, pltpu.roll(x, 1, axis=0))
    xs = x + (shifted - x) * mu_ref[...]

    c0 = 3 * RWKV_DIM
    p_r = xs[:, 0:RWKV_DIM]
    p_k = xs[:, RWKV_DIM:2 * RWKV_DIM]
    p_v = xs[:, 2 * RWKV_DIM:c0]
    p_wa = xs[:, c0:c0 + DECAY_LORA + ICLR_LORA]
    p_g = xs[:, c0 + DECAY_LORA + ICLR_LORA:SHIFT_COLS]

    lane = lax.broadcasted_iota(jnp.int32, p_wa.shape, 1)
    lora_in = jnp.where(lane < DECAY_LORA, jnp.tanh(p_wa), p_wa)
    lora = _dot_x3(lora_in, wl_hi_ref[...], wl_lo_ref[...])
    log_w = -jax.nn.softplus(-(dbase_ref[...] + lora[:, 0:RWKV_DIM])) - 0.5
    iclr = jax.nn.sigmoid(ibase_ref[...] + lora[:, RWKV_DIM:2 * RWKV_DIM])
    gate = _dot_x3(jax.nn.sigmoid(p_g), wg_hi_ref[...], wg_lo_ref[...])

    ones = ones_ref[...]
    kk = p_k * kk_ref[...]
    kk = kk * lax.rsqrt(_head_sums(kk * kk, ones) + 1e-12)
    k = p_k * (1.0 + (iclr - 1.0) * ka_ref[...])

    r_ref[...] = p_r
    lw_ref[...] = -jnp.exp(log_w)
    k_ref[...] = k
    v_ref[...] = p_v
    a_ref[...] = -kk
    b_ref[...] = kk * iclr
    gate_ref[...] = gate
    bonus_ref[...] = _head_sums(p_r * k * rk_ref[...], ones) * p_v


def _inproj(h, prm, seq, tm):
    t = h.shape[0]
    row = lambda c: pl.BlockSpec((tm, c), lambda i: (i, 0))
    out = lambda c, dt: jax.ShapeDtypeStruct((t, c), dt)
    consts = [prm["attn_norm"], prm["w_rw"], prm["w_mla"], prm["w_g"], prm["mu"], prm["wl_hi"],
              prm["wl_lo"], prm["wg_hi"], prm["wg_lo"], prm["decay_base"], prm["iclr_base"], prm["k_k"],
              prm["k_a"], prm["r_k"], prm["head_ones"]]
    return pl.pallas_call(
        functools.partial(_inproj_kernel, seq // tm),
        grid=(t // tm,),
        in_specs=[row(D_MODEL)] + [_const_spec(c.shape) for c in consts],
        out_specs=[row(MLA_PAD_COLS), row(2 * D_MODEL)] + [row(RWKV_DIM)] * 8,
        out_shape=[out(MLA_PAD_COLS, F32), out(2 * D_MODEL, BF16)] + [out(RWKV_DIM, F32)] * 8,
        scratch_shapes=[pltpu.VMEM((SUBLANES, SHIFT_COLS), F32)],
        compiler_params=_params("arbitrary"),
        name="inproj",
    )(h, *consts)


def _wkv_pair(load, state_ref, idx, store_y, ltri, eye):
    c = WKV_CHUNK
    lane = lax.broadcasted_iota(jnp.int32, (1, PAIR), 1)
    lo = lane < RWKV_HEAD_DIM
    r, lw, k, v, a, b = load()

    cum = _dot_split_lhs_left(ltri, lw)
    yield
    cum_last = cum[c - 1:c, :]
    g_inv = jnp.exp(-cum)
    a_t = a * jnp.exp(cum - lw)
    r_t = r * jnp.exp(cum)
    zero = jnp.zeros_like(a_t)
    lhs = jnp.concatenate([jnp.where(lo, a_t, zero), jnp.where(lo, zero, a_t),
                           jnp.where(lo, r_t, zero), jnp.where(lo, zero, r_t)], axis=0)
    b_t = b * g_inv
    k_t = k * g_inv
    a_all = _dot_nt(lhs, jnp.concatenate([b_t, k_t], axis=0))
    yield
    swapped = pltpu.roll(a_all, c, axis=1)
    head0_rows = (lax.broadcasted_iota(jnp.int32, (4 * c, 1), 0) // c) % 2 == 0
    a_b = jnp.where(head0_rows, a_all, swapped)
    a_k = jnp.where(head0_rows, swapped, a_all)

    rr = lax.broadcasted_iota(jnp.int32, (2 * c, 2 * c), 0)
    cc = lax.broadcasted_iota(jnp.int32, (2 * c, 2 * c), 1)
    same_head = (rr // c) == (cc // c)
    strict = same_head & ((cc % c) < (rr % c))
    incl = same_head & ((cc % c) <= (rr % c))
    l_bd = jnp.where(strict, a_b[0:2 * c], 0.0)
    aak_bd = jnp.where(strict, a_k[0:2 * c], 0.0)
    arb_bd = jnp.where(incl, a_b[2 * c:4 * c], 0.0)
    ark_bd = jnp.where(incl, a_k[2 * c:4 * c], 0.0)
    v_st = jnp.concatenate([jnp.where(lo, v, zero), jnp.where(lo, zero, v)], axis=0)
    av = _dot(jnp.concatenate([aak_bd, ark_bd], axis=0), v_st)
    q_st = av[0:2 * c]
    arkv_st = av[2 * c:4 * c]

    t_bd = eye + l_bd
    p = _dot(l_bd, l_bd)
    yield
    n = 2
    while 2 * n < c:
        pp_pt = _dot(p, jnp.concatenate([p, t_bd], axis=1))
        yield
        p = pp_pt[:, 0:2 * c]
        t_bd = t_bd + pp_pt[:, 2 * c:4 * c]
        n *= 2
    t_bd = t_bd + _dot(p, t_bd)
    yield

    tq_ta = _dot(t_bd, jnp.concatenate([lhs[0:2 * c], q_st], axis=1))
    yield
    rq_ra = _dot(arb_bd, tq_ta)
    yield
    ta_st = tq_ta[:, 0:PAIR]
    tq_st = tq_ta[:, PAIR:2 * PAIR]
    rh_st = lhs[2 * c:4 * c] + rq_ra[:, 0:PAIR]
    yh_st = rq_ra[:, PAIR:2 * PAIR] + arkv_st
    unstack = lambda x: x[0:c] + x[c:2 * c]
    ta = unstack(ta_st)
    rh = unstack(rh_st)
    tq = unstack(tq_st)
    yh = unstack(yh_st)

    state = state_ref[idx]
    z = _dot_nt(state, jnp.concatenate([ta, rh], axis=0))
    yield
    z = z + jnp.concatenate([tq, yh], axis=0).T
    v_t = jnp.concatenate([v, v], axis=0).T
    lane2 = lax.broadcasted_iota(jnp.int32, (1, 2 * c), 1)
    g_out = jnp.exp(cum_last - cum)
    upd = _dot(jnp.where(lane2 < c, z, v_t), jnp.concatenate([b * g_out, k * g_out], axis=0))
    yield
    state_ref[idx] = state * jnp.exp(cum_last) + jnp.where(same_head, upd, 0.0)
    store_y(z.T[c:2 * c, :])


def _dot_split_lhs_left(m_bf16, x):
    out = None
    for p in _split_bf16(x, 3):
        d = jnp.dot(m_bf16, p, preferred_element_type=F32)
        out = d if out is None else out + d
    return out


def _wkv_kernel(r_ref, lw_ref, k_ref, v_ref, a_ref, b_ref, ltri_ref, eye_ref, y_ref, state_ref):
    @pl.when(pl.program_id(0) == 0)
    def _():
        state_ref[...] = jnp.zeros_like(state_ref)

    ltri = ltri_ref[...]
    eye = eye_ref[...]
    pairs = RWKV_DIM // PAIR
    chains = []
    for bi in range(r_ref.shape[0]):
        for p in range(pairs):
            sl = slice(p * PAIR, (p + 1) * PAIR)

            def load(bi=bi, sl=sl):
                return tuple(ref[bi, :, sl] for ref in (r_ref, lw_ref, k_ref, v_ref, a_ref, b_ref))

            def store_y(y, bi=bi, sl=sl):
                y_ref[bi, :, sl] = y

            chains.append(_wkv_pair(load, state_ref, bi * pairs + p, store_y, ltri, eye))
    while chains:
        chains = [ch for ch in chains if next(ch, StopIteration) is not StopIteration]


def _wkv(r, lw, k, v, a, b, batch, seq):
    c = WKV_CHUNK
    shp = (batch, seq, RWKV_DIM)
    args = [x.reshape(shp) for x in (r, lw, k, v, a, b)]
    ltri = jnp.tril(jnp.ones((c, c), F32)).astype(BF16)
    eye = jnp.eye(2 * c, dtype=F32)
    blk = pl.BlockSpec((batch, c, RWKV_DIM), lambda ci: (0, ci, 0))
    y = pl.pallas_call(
        _wkv_kernel,
        grid=(seq // c,),
        in_specs=[blk] * 6 + [_const_spec((c, c)), _const_spec((2 * c, 2 * c))],
        out_specs=blk,
        out_shape=jax.ShapeDtypeStruct(shp, F32),
        scratch_shapes=[pltpu.VMEM((batch * (RWKV_DIM // PAIR), PAIR, PAIR), F32)],
        compiler_params=_params("arbitrary"),
        name="wkv",
    )(*args, ltri, eye)
    return y.reshape(batch * seq, RWKV_DIM)


def _mla_prep_kernel(p_ref, cos_ref, sina_ref, sinb_ref, qn_ref, kvn_ref, wq_ref, wk_ref, wv_ref,
                     qt_ref, k_ref, vt_ref):
    x = p_ref[...]
    cos, sina, sinb = cos_ref[...], sina_ref[...], sinb_ref[...]
    scale = (QK_NOPE_DIM + QK_ROPE_DIM) ** -0.5 * LOG2_E
    qn = _rms(x[:, 0:Q_LORA_RANK], qn_ref[...]).astype(BF16)
    kvn = _rms(x[:, Q_LORA_RANK:Q_LORA_RANK + KV_LORA_RANK], kvn_ref[...]).astype(BF16)
    k_pe = _rope(x[:, Q_LORA_RANK + KV_LORA_RANK:MLA_PAD_COLS], cos, sina, sinb)
    q = jnp.dot(qn, wq_ref[...], preferred_element_type=F32)
    kf = jnp.dot(kvn, wk_ref[...], preferred_element_type=F32)
    v = jnp.dot(kvn, wv_ref[...], preferred_element_type=F32)
    for h in range(MLA_HEADS):
        sl = slice(h * LANES, (h + 1) * LANES)
        qt_ref[0, h] = (_rope(q[:, sl], cos, sina, sinb) * scale).T.astype(BF16)
        k_ref[0, h, 0] = (kf[:, sl] + k_pe).astype(BF16)
    ones = jnp.ones((V_ONES_ROWS, x.shape[0]), F32)
    for p in range(MLA_HEADS // 2):
        vt = v[:, p * LANES:(p + 1) * LANES].T
        vt_ref[0, p, 0] = jnp.concatenate([vt[0:V_HEAD_DIM], ones, vt[V_HEAD_DIM:], ones], axis=0).astype(BF16)


def _mla_prep(p_mla, tables, prm, batch, seq, blk):
    t = p_mla.shape[0]
    n = seq // blk
    tab = pl.BlockSpec((blk, LANES), lambda i: (i, 0))
    consts = [prm["q_norm"], prm["kv_norm"], prm["w_q"], prm["w_k"], prm["w_v"]]
    return pl.pallas_call(
        _mla_prep_kernel,
        grid=(t // blk,),
        in_specs=[pl.BlockSpec((blk, MLA_PAD_COLS), lambda i: (i, 0)), tab, tab, tab]
        + [_const_spec(c.shape) for c in consts],
        out_specs=[pl.BlockSpec((1, MLA_HEADS, LANES, blk), lambda i: (i // n, 0, 0, i % n)),
                   pl.BlockSpec((1, MLA_HEADS, 1, blk, LANES), lambda i: (i // n, 0, i % n, 0, 0)),
                   pl.BlockSpec((1, MLA_HEADS // 2, 1, 2 * V_EXT_ROWS, blk), lambda i: (i // n, 0, i % n, 0, 0))],
        out_shape=[jax.ShapeDtypeStruct((batch, MLA_HEADS, LANES, seq), BF16),
                   jax.ShapeDtypeStruct((batch, MLA_HEADS, n, blk, LANES), BF16),
                   jax.ShapeDtypeStruct((batch, MLA_HEADS // 2, n, 2 * V_EXT_ROWS, blk), BF16)],
        compiler_params=_params("parallel"),
        name="mla_prep",
    )(p_mla, *tables, *consts)


def _attn_kernel(qt_ref, k_ref, vt_ref, o_ref, m_ref, acc_ref, s0_ref, s1_ref):
    i = pl.program_id(2)
    blk = qt_ref.shape[3]
    m_ref[...] = jnp.full_like(m_ref, -jnp.inf)
    acc_ref[...] = jnp.zeros_like(acc_ref)

    def scores(j, s_ref):
        for h in range(2):
            s_ref[h] = jnp.dot(k_ref[0, h, j], qt_ref[0, h], preferred_element_type=F32)

    def update(j, s_ref, masked):
        for h in range(2):
            s = s_ref[h]
            if masked:
                k_pos = lax.broadcasted_iota(jnp.int32, (blk, blk), 0)
                q_pos = lax.broadcasted_iota(jnp.int32, (blk, blk), 1)
                s = jnp.where(k_pos <= q_pos, s, NEG_INF)
            m_old = m_ref[h]
            m_new = jnp.maximum(m_old, jnp.max(s, axis=0, keepdims=True))
            vt = vt_ref[0, 0, j, h * V_EXT_ROWS:(h + 1) * V_EXT_ROWS, :]
            pv = jnp.dot(vt, jnp.exp2(s - m_new).astype(BF16), preferred_element_type=F32)
            acc_ref[h] = jnp.exp2(m_old - m_new) * acc_ref[h] + pv
            m_ref[h] = m_new

    scores(0, s0_ref)

    def body(jj, carry):
        scores(2 * jj + 1, s1_ref)
        update(2 * jj, s0_ref, False)
        scores(2 * jj + 2, s0_ref)
        update(2 * jj + 1, s1_ref, False)
        return carry

    lax.fori_loop(0, i // 2, body, 0)

    @pl.when(i % 2 == 0)
    def _():
        update(i, s0_ref, True)

    @pl.when(i % 2 == 1)
    def _():
        scores(i, s1_ref)
        update(i - 1, s0_ref, False)
        update(i, s1_ref, True)

    o_t = jnp.concatenate([acc_ref[h, 0:V_HEAD_DIM] / acc_ref[h, V_HEAD_DIM:V_HEAD_DIM + 1]
                           for h in range(2)], axis=0)
    o_ref[0] = o_t.T.astype(BF16)


def _mla_attn(qt, k, vt, batch, seq, blk):
    n = seq // blk
    hv = MLA_HEADS * V_HEAD_DIM
    o = pl.pallas_call(
        _attn_kernel,
        grid=(batch, MLA_HEADS // 2, n),
        in_specs=[pl.BlockSpec((1, 2, LANES, blk), lambda b, p, i: (b, p, 0, i)),
                  pl.BlockSpec((1, 2, n, blk, LANES), lambda b, p, i: (b, p, 0, 0, 0)),
                  pl.BlockSpec((1, 1, n, 2 * V_EXT_ROWS, blk), lambda b, p, i: (b, p, 0, 0, 0))],
        out_specs=pl.BlockSpec((1, blk, LANES), lambda b, p, i: (b, i, p)),
        out_shape=jax.ShapeDtypeStruct((batch, seq, hv), BF16),
        scratch_shapes=[pltpu.VMEM((2, 1, blk), F32), pltpu.VMEM((2, V_EXT_ROWS, blk), F32),
                        pltpu.VMEM((2, blk, blk), F32), pltpu.VMEM((2, blk, blk), F32)],
        compiler_params=_params("parallel", "parallel", "arbitrary"),
        name="mla_attn",
    )(qt, k, vt)
    return o.reshape(batch * seq, hv)


def _merge_kernel(y_ref, bonus_ref, gate_ref, o_ref, g_ref, h_ref, lnw_ref, lnb_ref, ones_ref,
                  woa_ref, wob_ref, wout_ref, out_ref):
    ones = ones_ref[...]
    inv_n = 1.0 / RWKV_HEAD_DIM
    y = y_ref[...]
    mu = _head_sums(y, ones) * inv_n
    d = y - mu
    var = _head_sums(d * d, ones) * inv_n
    yn = (d * lax.rsqrt(var + LNX_EPS)) * lnw_ref[...] + lnb_ref[...]
    z = (yn + bonus_ref[...]) * gate_ref[...]
    y_a = _dot(z, woa_ref[...])
    y_b = jnp.dot(o_ref[...], wob_ref[...], preferred_element_type=F32)
    g = g_ref[...].astype(F32)
    merged = jax.nn.sigmoid(g[:, 0:D_MODEL]) * y_a + jax.nn.sigmoid(g[:, D_MODEL:2 * D_MODEL]) * y_b
    out_ref[...] = h_ref[...] + _dot(merged, wout_ref[...])


def _merge(y, bonus, gate, o_mla, gates, h, prm, tm):
    t = h.shape[0]
    row = lambda c: pl.BlockSpec((tm, c), lambda i: (i, 0))
    consts = [prm["lnx_w"], prm["lnx_b"], prm["head_ones"], prm["w_oa"], prm["w_ob"], prm["w_out"]]
    return pl.pallas_call(
        _merge_kernel,
        grid=(t // tm,),
        in_specs=[row(RWKV_DIM)] * 4 + [row(2 * D_MODEL), row(D_MODEL)]
        + [_const_spec(c.shape) for c in consts],
        out_specs=row(D_MODEL),
        out_shape=jax.ShapeDtypeStruct((t, D_MODEL), F32),
        compiler_params=_params("parallel"),
        name="merge",
    )(y, bonus, gate, o_mla, gates, h, *consts)


def _ffn_kernel(blocks_per_seq, final, h_ref, g_ref, wg_ref, wv_ref, cw_ref, cb_ref, wd_ref, fg_ref,
                out_ref, n_ref, acc_ref, carry_ref):
    i = pl.program_id(0)
    f = pl.program_id(1)
    tm = h_ref.shape[0]
    rows = min(FFN_SUB_ROWS, tm)

    @pl.when(f == 0)
    def _():
        n_ref[...] = _rms(h_ref[...], g_ref[...]).astype(BF16)
        acc_ref[...] = jnp.zeros_like(acc_ref)

    def up(sb):
        n = n_ref[sb * rows:(sb + 1) * rows, :]
        return (jnp.dot(n, wg_ref[...], preferred_element_type=F32),
                jnp.dot(n, wv_ref[...], preferred_element_type=F32))

    tail = carry_ref[f]
    tail = jnp.where(i % blocks_per_seq == 0, jnp.zeros_like(tail), tail)
    cw = cw_ref[0]
    row = lax.broadcasted_iota(jnp.int32, (rows, 1), 0)
    sqrt_half = np.sqrt(0.5).astype(np.float32)
    nxt = up(0)
    for sb in range(tm // rows):
        u_gate, u_val = nxt
        if sb + 1 < tm // rows:
            nxt = up(sb + 1)
        back1 = jnp.where(row == 0, tail[SUBLANES - 1:SUBLANES], pltpu.roll(u_gate, 1, axis=0))
        back2 = jnp.where(row == 0, tail[SUBLANES - 2:SUBLANES - 1],
                          jnp.where(row == 1, tail[SUBLANES - 1:SUBLANES], pltpu.roll(u_gate, 2, axis=0)))
        c = cb_ref[0] + cw[0:1] * back2
        c = c + cw[1:2] * back1
        c = c + cw[2:3] * u_gate
        tail = u_gate[rows - SUBLANES:rows]
        act = 0.5 * c * (1.0 + lax.erf(c * sqrt_half))
        down = jnp.dot((act * u_val).astype(BF16), wd_ref[...], preferred_element_type=F32)
        acc_ref[sb * rows:(sb + 1) * rows, :] += down
    carry_ref[f] = tail

    @pl.when(f == pl.num_programs(1) - 1)
    def _():
        out = h_ref[...] + acc_ref[...]
        out_ref[...] = _rms(out, fg_ref[...]) if final else out


def _conv_ffn(h, prm, final_gain, seq, tm, fc):
    t = h.shape[0]
    nf = FFN_DIM // fc
    row = pl.BlockSpec((tm, D_MODEL), lambda i, f: (i, 0))
    final = final_gain is not None
    fg = final_gain.reshape(1, D_MODEL) if final else prm["ffn_norm"]
    return pl.pallas_call(
        functools.partial(_ffn_kernel, seq // tm, final),
        grid=(t // tm, nf),
        in_specs=[row, _const_spec((1, D_MODEL)),
                  pl.BlockSpec((D_MODEL, fc), lambda i, f: (0, f)),
                  pl.BlockSpec((D_MODEL, fc), lambda i, f: (0, nf + f)),
                  pl.BlockSpec((1, SUBLANES, fc), lambda i, f: (f, 0, 0)),
                  pl.BlockSpec((1, 1, fc), lambda i, f: (f, 0, 0)),
                  pl.BlockSpec((fc, D_MODEL), lambda i, f: (f, 0)),
                  _const_spec((1, D_MODEL))],
        out_specs=row,
        out_shape=jax.ShapeDtypeStruct((t, D_MODEL), F32),
        scratch_shapes=[pltpu.VMEM((tm, D_MODEL), BF16), pltpu.VMEM((tm, D_MODEL), F32),
                        pltpu.VMEM((nf, SUBLANES, fc), F32)],
        compiler_params=_params("arbitrary", "arbitrary"),
        name="conv_ffn",
    )(h, prm["ffn_norm"], prm["w_up"], prm["w_up"], prm["conv_w"], prm["conv_b"], prm["w_down"], fg)


def _hi_lo(w):
    hi = w.astype(BF16)
    return hi, (w - hi.astype(F32)).astype(BF16)


def _layer_params(l, fc, attn_norm, w_in, mu_shift, decay_base, w_decay_up, iclr_base, w_iclr_up,
                  w_gate_up, k_k, k_a, r_k, lnx_w, lnx_b, w_out_rwkv, q_norm, w_q_up, kv_norm,
                  w_kv_up, w_out_mla, w_out, ffn_norm, w_ffn_up, conv_w, conv_b, w_ffn_down):
    row = lambda x: x.reshape(1, -1)
    w = w_in[l]
    w_mla_src = w[:, SHIFT_COLS:SHIFT_COLS + MLA_COLS]
    lat = Q_LORA_RANK + KV_LORA_RANK
    w_mla = jnp.zeros((D_MODEL, MLA_PAD_COLS), F32)
    w_mla = w_mla.at[:, 0:lat].set(w_mla_src[:, 0:lat])
    w_mla = w_mla.at[:, lat + ROPE_LANE0:lat + ROPE_LANE0 + QK_ROPE_DIM].set(w_mla_src[:, lat:])
    w_lora = jnp.zeros((DECAY_LORA + ICLR_LORA, 2 * RWKV_DIM), F32)
    w_lora = w_lora.at[0:DECAY_LORA, 0:RWKV_DIM].set(w_decay_up[l])
    w_lora = w_lora.at[DECAY_LORA:, RWKV_DIM:].set(w_iclr_up[l])
    wl_hi, wl_lo = _hi_lo(w_lora)
    wg_hi, wg_lo = _hi_lo(w_gate_up[l])
    head = jnp.arange(MXU_WIDTH) // RWKV_HEAD_DIM
    head_ones = (head[:, None] == head[None, :]).astype(BF16)
    qk = QK_NOPE_DIM + QK_ROPE_DIM
    w_q = jnp.pad(w_q_up[l].reshape(Q_LORA_RANK, MLA_HEADS, qk), ((0, 0), (0, 0), (0, LANES - qk)))
    w_kv = w_kv_up[l].reshape(KV_LORA_RANK, MLA_HEADS, QK_NOPE_DIM + V_HEAD_DIM)
    w_k = jnp.pad(w_kv[:, :, :QK_NOPE_DIM], ((0, 0), (0, 0), (0, LANES - QK_NOPE_DIM)))
    nf = FFN_DIM // fc
    return dict(
        attn_norm=row(attn_norm[l]),
        w_rw=w[:, :SHIFT_COLS].astype(BF16),
        w_mla=w_mla.astype(BF16),
        w_g=w[:, SHIFT_COLS + MLA_COLS:].astype(BF16),
        mu=row(mu_shift[l]), wl_hi=wl_hi, wl_lo=wl_lo, wg_hi=wg_hi, wg_lo=wg_lo,
        decay_base=row(decay_base[l]), iclr_base=row(iclr_base[l]), k_k=row(k_k[l]), k_a=row(k_a[l]),
        r_k=row(r_k[l]), head_ones=head_ones, lnx_w=row(lnx_w[l]), lnx_b=row(lnx_b[l]),
        q_norm=row(q_norm[l]), kv_norm=row(kv_norm[l]),
        w_q=w_q.reshape(Q_LORA_RANK, MLA_HEADS * LANES).astype(BF16),
        w_k=w_k.reshape(KV_LORA_RANK, MLA_HEADS * LANES).astype(BF16),
        w_v=w_kv[:, :, QK_NOPE_DIM:].reshape(KV_LORA_RANK, MLA_HEADS * V_HEAD_DIM).astype(BF16),
        w_oa=w_out_rwkv[l].astype(BF16), w_ob=w_out_mla[l].astype(BF16), w_out=w_out[l].astype(BF16),
        ffn_norm=row(ffn_norm[l]), w_up=w_ffn_up[l].astype(BF16), w_down=w_ffn_down[l].astype(BF16),
        conv_w=jnp.pad(conv_w[l].reshape(3, nf, fc).transpose(1, 0, 2), ((0, 0), (0, SUBLANES - 3), (0, 0))),
        conv_b=conv_b[l].reshape(nf, 1, fc),
    )


def _tile_plan(seq):
    pick = lambda want: min(want, seq)
    return dict(proj=pick(256), attn=pick(512), merge=pick(256), ffn=pick(1024),
                ffn_cols=FFN_DIM // 2, rope=pick(512))


def kernel(x, positions, attn_norm, w_in, mu_shift, decay_base, w_decay_up, iclr_base, w_iclr_up, w_gate_up, k_k, k_a, r_k, lnx_w, lnx_b, w_out_rwkv, q_norm, w_q_up, kv_norm, w_kv_up, w_out_mla, w_out, ffn_norm, w_ffn_up, conv_w, conv_b, w_ffn_down, final_norm):
    batch, seq, _ = x.shape
    depth = w_in.shape[0]
    assert seq % WKV_CHUNK == 0 and x.shape[2] == D_MODEL
    plan = _tile_plan(seq)
    tables = _rope_tables(positions, plan["rope"])
    h = x.reshape(batch * seq, D_MODEL)
    for l in range(depth):
        prm = _layer_params(l, plan["ffn_cols"], attn_norm, w_in, mu_shift, decay_base, w_decay_up,
                            iclr_base, w_iclr_up, w_gate_up, k_k, k_a, r_k, lnx_w, lnx_b, w_out_rwkv,
                            q_norm, w_q_up, kv_norm, w_kv_up, w_out_mla, w_out, ffn_norm, w_ffn_up,
                            conv_w, conv_b, w_ffn_down)
        p_mla, gates, r, lw, k, v, a, b, gate, bonus = _inproj(h, prm, seq, plan["proj"])
        y = _wkv(r, lw, k, v, a, b, batch, seq)
        q_h, k_h, v_h = _mla_prep(p_mla, tables, prm, batch, seq, plan["attn"])
        o_mla = _mla_attn(q_h, k_h, v_h, batch, seq, plan["attn"])
        h = _merge(y, bonus, gate, o_mla, gates, h, prm, plan["merge"])
        last = final_norm if l == depth - 1 else None
        h = _conv_ffn(h, prm, last, seq, plan["ffn"], plan["ffn_cols"])
    return h.reshape(batch, seq, D_MODEL)
```

```python
import functools

import jax
import jax.numpy as jnp
import numpy as np
from jax import lax
from jax.experimental import pallas as pl
from jax.experimental.pallas import tpu as pltpu

F32 = jnp.float32
BF16 = jnp.bfloat16

D_MODEL = 1024
RWKV_HEADS = 8
RWKV_HEAD_DIM = 64
RWKV_DIM = RWKV_HEADS * RWKV_HEAD_DIM
DECAY_LORA = 64
ICLR_LORA = 64
GATE_LORA = 128
LNX_EPS = 64e-5
MLA_HEADS = 8
QK_NOPE_DIM = 64
QK_ROPE_DIM = 32
V_HEAD_DIM = 64
Q_LORA_RANK = 256
KV_LORA_RANK = 128
ROPE_THETA = 10000.0
NEG_INF = -1e30
FFN_DIM = 2816
NORM_EPS = 1e-6
SHIFT_COLS = 3 * RWKV_DIM + DECAY_LORA + ICLR_LORA + GATE_LORA
MLA_COLS = Q_LORA_RANK + KV_LORA_RANK + QK_ROPE_DIM

LANES = 128
SUBLANES = 8
MXU_WIDTH = 256
VMEM_LIMIT_BYTES = 56 * 1024 * 1024

MLA_PAD_COLS = 4 * LANES
ROPE_LANE0 = QK_NOPE_DIM
WKV_CHUNK = 64
PAIR = 2 * RWKV_HEAD_DIM
LOG2_E = float(np.log2(np.e))
V_ONES_ROWS = 16
V_EXT_ROWS = V_HEAD_DIM + V_ONES_ROWS
FFN_SUB_ROWS = 256


def _params(*semantics, flags=None):
    return pltpu.CompilerParams(dimension_semantics=semantics, vmem_limit_bytes=VMEM_LIMIT_BYTES,
                                flags=flags)


def _dot(a, b):
    return jnp.dot(a.astype(BF16), b.astype(BF16), preferred_element_type=F32)


def _dot_nt(a, b):
    return lax.dot_general(a.astype(BF16), b.astype(BF16), (((1,), (1,)), ((), ())),
                           preferred_element_type=F32)


def _split_bf16(x, terms):
    parts = []
    rest = x
    for _ in range(terms):
        p = rest.astype(BF16)
        parts.append(p)
        rest = rest - p.astype(F32)
    return parts


def _dot_split_lhs(a, b_bf16, terms):
    out = None
    for p in _split_bf16(a, terms):
        d = jnp.dot(p, b_bf16, preferred_element_type=F32)
        out = d if out is None else out + d
    return out


def _head_sums(x, ones):
    w = ones.shape[0]
    return jnp.concatenate([_dot_split_lhs(x[:, c:c + w], ones, 2) for c in range(0, x.shape[1], w)],
                           axis=1)


def _dot_x3(a, b_hi, b_lo):
    a_hi, a_lo = _split_bf16(a, 2)
    return (jnp.dot(a_hi, b_hi, preferred_element_type=F32)
            + jnp.dot(a_hi, b_lo, preferred_element_type=F32)
            + jnp.dot(a_lo, b_hi, preferred_element_type=F32))


def _rms(x, g):
    ms = jnp.mean(x * x, axis=-1, keepdims=True)
    return (x * lax.rsqrt(ms + NORM_EPS)) * g


def _const_spec(shape):
    return pl.BlockSpec(shape, lambda *_: (0,) * len(shape))


def _rope_tables_kernel(pos_ref, invf_ref, cos_ref, sin_ref, nsin_ref):
    ang = pos_ref[...] * invf_ref[...]
    s = jnp.sin(ang)
    cos_ref[...] = jnp.cos(ang)
    sin_ref[...] = s
    nsin_ref[...] = -s


def _rope_tables(positions):
    t = positions.size
    half = QK_ROPE_DIM // 2
    per_row = LANES // half
    inv_freq = jnp.power(ROPE_THETA, -jnp.arange(0, QK_ROPE_DIM, 2, dtype=F32) / QK_ROPE_DIM)
    invf = jnp.tile(inv_freq, per_row).reshape(1, LANES)
    posc = jnp.repeat(positions.reshape(t // per_row, per_row).astype(F32), half, axis=1)
    spec = pl.BlockSpec(posc.shape, lambda i: (0, 0))
    out = jax.ShapeDtypeStruct(posc.shape, F32)
    cos, sin, nsin = (x.reshape(t, half) for x in pl.pallas_call(
        _rope_tables_kernel,
        grid=(1,),
        in_specs=[spec, _const_spec((1, LANES))],
        out_specs=[spec, spec, spec],
        out_shape=[out, out, out],
        compiler_params=_params("arbitrary"),
        name="rope_tables",
    )(posc, invf))
    pad = lambda x, fill: jnp.pad(x, ((0, 0), (ROPE_LANE0, LANES - ROPE_LANE0 - x.shape[1])),
                                  constant_values=fill)
    zeros = jnp.zeros_like(sin)
    natural = (pad(jnp.concatenate([cos, cos], axis=1), 1.0),
               pad(jnp.concatenate([nsin, zeros], axis=1), 0.0),
               pad(jnp.concatenate([zeros, sin], axis=1), 0.0))
    return natural, (cos.T, sin.T)


def _rope(tile, cos, sina, sinb):
    half = QK_ROPE_DIM // 2
    return (tile * cos + pltpu.roll(tile, LANES - half, axis=1) * sina
            + pltpu.roll(tile, half, axis=1) * sinb)


def _inproj_kernel(blocks_per_seq, x_ref, g_ref, wrw_ref, wmla_ref, wgate_ref, mu_ref, wl_hi_ref,
                   wl_lo_ref, wg_hi_ref, wg_lo_ref, dbase_ref, ibase_ref, kk_ref, ka_ref, rk_ref, ones_ref,
                   pmla_ref, pg_ref, r_ref, lw_ref, k_ref, v_ref, a_ref, b_ref, gate_ref, bonus_ref,
                   tail_ref):
    i = pl.program_id(0)
    n = _rms(x_ref[...], g_ref[...]).astype(BF16)
    pmla_ref[...] = jnp.dot(n, wmla_ref[...], preferred_element_type=F32)
    pg_ref[...] = jnp.dot(n, wgate_ref[...], preferred_element_type=F32).astype(BF16)
    x = jnp.dot(n, wrw_ref[...], preferred_element_type=F32)
    tm = x.shape[0]
    prev = tail_ref[SUBLANES - 1:SUBLANES, :]
    prev = jnp.where(i % blocks_per_seq == 0, jnp.zeros_like(prev), prev)
    tail_ref[...] = x[tm - SUBLANES:tm]
    row = lax.broadcasted_iota(jnp.int32, (tm, 1), 0)
    shifted = jnp.where(row == 0, prev, pltpu.roll(x, 1, axis=0))
    xs = x + (shifted - x) * mu_ref[...]

    c0 = 3 * RWKV_DIM
    p_r = xs[:, 0:RWKV_DIM]
    p_k = xs[:, RWKV_DIM:2 * RWKV_DIM]
    p_v = xs[:, 2 * RWKV_DIM:c0]
    p_wa = xs[:, c0:c0 + DECAY_LORA + ICLR_LORA]
    p_g = xs[:, c0 + DECAY_LORA + ICLR_LORA:SHIFT_COLS]

    lane = lax.broadcasted_iota(jnp.int32, p_wa.shape, 1)
    lora_in = jnp.where(lane < DECAY_LORA, jnp.tanh(p_wa), p_wa)
    lora = _dot_x3(lora_in, wl_hi_ref[...], wl_lo_ref[...])
    log_w = -jax.nn.softplus(-(dbase_ref[...] + lora[:, 0:RWKV_DIM])) - 0.5
    iclr = jax.nn.sigmoid(ibase_ref[...] + lora[:, RWKV_DIM:2 * RWKV_DIM])
    gate = _dot_x3(jax.nn.sigmoid(p_g), wg_hi_ref[...], wg_lo_ref[...])

    ones = ones_ref[...]
    kk = p_k * kk_ref[...]
    kk = kk * lax.rsqrt(_head_sums(kk * kk, ones) + 1e-12)
    k = p_k * (1.0 + (iclr - 1.0) * ka_ref[...])

    r_ref[...] = p_r
    lw_ref[...] = -jnp.exp(log_w)
    k_ref[...] = k
    v_ref[...] = p_v
    a_ref[...] = -kk
    b_ref[...] = kk * iclr
    gate_ref[...] = gate
    bonus_ref[...] = _head_sums(p_r * k * rk_ref[...], ones) * p_v


def _inproj(h, prm, seq, tm):
    t = h.shape[0]
    row = lambda c: pl.BlockSpec((tm, c), lambda i: (i, 0))
    out = lambda c, dt: jax.ShapeDtypeStruct((t, c), dt)
    consts = [prm["attn_norm"], prm["w_rw"], prm["w_mla"], prm["w_g"], prm["mu"], prm["wl_hi"],
              prm["wl_lo"], prm["wg_hi"], prm["wg_lo"], prm["decay_base"], prm["iclr_base"], prm["k_k"],
              prm["k_a"], prm["r_k"], prm["head_ones"]]
    return pl.pallas_call(
        functools.partial(_inproj_kernel, seq // tm),
        grid=(t // tm,),
        in_specs=[row(D_MODEL)] + [_const_spec(c.shape) for c in consts],
        out_specs=[row(MLA_PAD_COLS), row(2 * D_MODEL)] + [row(RWKV_DIM)] * 8,
        out_shape=[out(MLA_PAD_COLS, F32), out(2 * D_MODEL, BF16)] + [out(RWKV_DIM, F32)] * 8,
        scratch_shapes=[pltpu.VMEM((SUBLANES, SHIFT_COLS), F32)],
        compiler_params=_params("arbitrary"),
        name="inproj",
    )(h, *consts)


def _wkv_pair(load, state_ref, idx, store_y, ltri, eye):
    c = WKV_CHUNK
    lane = lax.broadcasted_iota(jnp.int32, (1, PAIR), 1)
    lo = lane < RWKV_HEAD_DIM
    r, lw, k, v, a, b = load()

    cum = _dot_split_lhs_left(ltri, lw)
    yield
    cum_last = cum[c - 1:c, :]
    g_inv = jnp.exp(-cum)
    a_t = a * jnp.exp(cum - lw)
    r_t = r * jnp.exp(cum)
    zero = jnp.zeros_like(a_t)
    lhs = jnp.concatenate([jnp.where(lo, a_t, zero), jnp.where(lo, zero, a_t),
                           jnp.where(lo, r_t, zero), jnp.where(lo, zero, r_t)], axis=0)
    b_t = b * g_inv
    k_t = k * g_inv
    a_all = _dot_nt(lhs, jnp.concatenate([b_t, k_t], axis=0))
    yield
    swapped = pltpu.roll(a_all, c, axis=1)
    head0_rows = (lax.broadcasted_iota(jnp.int32, (4 * c, 1), 0) // c) % 2 == 0
    a_b = jnp.where(head0_rows, a_all, swapped)
    a_k = jnp.where(head0_rows, swapped, a_all)

    rr = lax.broadcasted_iota(jnp.int32, (2 * c, 2 * c), 0)
    cc = lax.broadcasted_iota(jnp.int32, (2 * c, 2 * c), 1)
    same_head = (rr // c) == (cc // c)
    strict = same_head & ((cc % c) < (rr % c))
    incl = same_head & ((cc % c) <= (rr % c))
    l_bd = jnp.where(strict, a_b[0:2 * c], 0.0)
    aak_bd = jnp.where(strict, a_k[0:2 * c], 0.0)
    arb_bd = jnp.where(incl, a_b[2 * c:4 * c], 0.0)
    ark_bd = jnp.where(incl, a_k[2 * c:4 * c], 0.0)
    v_st = jnp.concatenate([jnp.where(lo, v, zero), jnp.where(lo, zero, v)], axis=0)
    av = _dot(jnp.concatenate([aak_bd, ark_bd], axis=0), v_st)
    q_st = av[0:2 * c]
    arkv_st = av[2 * c:4 * c]

    t_bd = eye + l_bd
    p = _dot(l_bd, l_bd)
    yield
    n = 2
    while 2 * n < c:
        pp_pt = _dot(p, jnp.concatenate([p, t_bd], axis=1))
        yield
        p = pp_pt[:, 0:2 * c]
        t_bd = t_bd + pp_pt[:, 2 * c:4 * c]
        n *= 2
    t_bd = t_bd + _dot(p, t_bd)
    yield

    tq_ta = _dot(t_bd, jnp.concatenate([lhs[0:2 * c], q_st], axis=1))
    yield
    rq_ra = _dot(arb_bd, tq_ta)
    yield
    ta_st = tq_ta[:, 0:PAIR]
    tq_st = tq_ta[:, PAIR:2 * PAIR]
    rh_st = lhs[2 * c:4 * c] + rq_ra[:, 0:PAIR]
    yh_st = rq_ra[:, PAIR:2 * PAIR] + arkv_st
    unstack = lambda x: x[0:c] + x[c:2 * c]
    ta = unstack(ta_st)
    rh = unstack(rh_st)
    tq = unstack(tq_st)
    yh = unstack(yh_st)

    state = state_ref[idx]
    z = _dot_nt(state, jnp.concatenate([ta, rh], axis=0))
    yield
    z = z + jnp.concatenate([tq, yh], axis=0).T
    v_t = jnp.concatenate([v, v], axis=0).T
    lane2 = lax.broadcasted_iota(jnp.int32, (1, 2 * c), 1)
    g_out = jnp.exp(cum_last - cum)
    upd = _dot(jnp.where(lane2 < c, z, v_t), jnp.concatenate([b * g_out, k * g_out], axis=0))
    yield
    state_ref[idx] = state * jnp.exp(cum_last) + jnp.where(same_head, upd, 0.0)
    store_y(z.T[c:2 * c, :])


def _dot_split_lhs_left(m_bf16, x):
    out = None
    for p in _split_bf16(x, 3):
        d = jnp.dot(m_bf16, p, preferred_element_type=F32)
        out = d if out is None else out + d
    return out


def _wkv_kernel(r_ref, lw_ref, k_ref, v_ref, a_ref, b_ref, ltri_ref, eye_ref, y_ref, state_ref):
    @pl.when(pl.program_id(0) == 0)
    def _():
        state_ref[...] = jnp.zeros_like(state_ref)

    ltri = ltri_ref[...]
    eye = eye_ref[...]
    pairs = RWKV_DIM // PAIR
    chains = []
    for bi in range(r_ref.shape[0]):
        for p in range(pairs):
            sl = slice(p * PAIR, (p + 1) * PAIR)

            def load(bi=bi, sl=sl):
                return tuple(ref[bi, :, sl] for ref in (r_ref, lw_ref, k_ref, v_ref, a_ref, b_ref))

            def store_y(y, bi=bi, sl=sl):
                y_ref[bi, :, sl] = y

            chains.append(_wkv_pair(load, state_ref, bi * pairs + p, store_y, ltri, eye))
    while chains:
        chains = [ch for ch in chains if next(ch, StopIteration) is not StopIteration]


def _wkv(r, lw, k, v, a, b, batch, seq):
    c = WKV_CHUNK
    shp = (batch, seq, RWKV_DIM)
    args = [x.reshape(shp) for x in (r, lw, k, v, a, b)]
    ltri = jnp.tril(jnp.ones((c, c), F32)).astype(BF16)
    eye = jnp.eye(2 * c, dtype=F32)
    blk = pl.BlockSpec((batch, c, RWKV_DIM), lambda ci: (0, ci, 0))
    y = pl.pallas_call(
        _wkv_kernel,
        grid=(seq // c,),
        in_specs=[blk] * 6 + [_const_spec((c, c)), _const_spec((2 * c, 2 * c))],
        out_specs=blk,
        out_shape=jax.ShapeDtypeStruct(shp, F32),
        scratch_shapes=[pltpu.VMEM((batch * (RWKV_DIM // PAIR), PAIR, PAIR), F32)],
        compiler_params=_params("arbitrary"),
        name="wkv",
    )(*args, ltri, eye)
    return y.reshape(batch * seq, RWKV_DIM)


def _mla_prep_kernel(p_ref, cos_ref, sina_ref, sinb_ref, cost_ref, sint_ref, qn_ref, kvn_ref, wqt_ref,
                     wk_ref, wvt_ref, qt_ref, k_ref, vt_ref):
    x = p_ref[...]
    blk = x.shape[0]
    scale = (QK_NOPE_DIM + QK_ROPE_DIM) ** -0.5 * LOG2_E
    qn = _rms(x[:, 0:Q_LORA_RANK], qn_ref[...]).astype(BF16)
    kvn = _rms(x[:, Q_LORA_RANK:Q_LORA_RANK + KV_LORA_RANK], kvn_ref[...]).astype(BF16)
    k_pe = _rope(x[:, Q_LORA_RANK + KV_LORA_RANK:MLA_PAD_COLS], cos_ref[...], sina_ref[...], sinb_ref[...])
    kf = jnp.dot(kvn, wk_ref[...], preferred_element_type=F32)
    for h in range(MLA_HEADS):
        k_ref[0, h, 0] = (kf[:, h * LANES:(h + 1) * LANES] + k_pe).astype(BF16)
    q_t = _dot_nt(wqt_ref[...], qn)
    v_t = _dot_nt(wvt_ref[...], kvn)
    cos_t, sin_t = cost_ref[...], sint_ref[...]
    half = QK_ROPE_DIM // 2
    r0 = ROPE_LANE0
    for h in range(MLA_HEADS):
        tile = q_t[h * LANES:(h + 1) * LANES]
        x1 = tile[r0:r0 + half]
        x2 = tile[r0 + half:r0 + QK_ROPE_DIM]
        roped = jnp.concatenate([tile[0:r0], x1 * cos_t - x2 * sin_t, x1 * sin_t + x2 * cos_t,
                                 tile[r0 + QK_ROPE_DIM:]], axis=0)
        qt_ref[0, h] = (roped * scale).astype(BF16)
    ones = jnp.ones((V_ONES_ROWS, blk), F32)
    for h in range(0, MLA_HEADS, 2):
        rows = lambda g: v_t[g * V_HEAD_DIM:(g + 1) * V_HEAD_DIM]
        vt_ref[0, h // 2, 0] = jnp.concatenate([rows(h), ones, rows(h + 1), ones], axis=0).astype(BF16)


def _mla_prep(p_mla, tables, prm, batch, seq, blk):
    t = p_mla.shape[0]
    n = seq // blk
    natural, transposed = tables
    tab = pl.BlockSpec((blk, LANES), lambda i: (i, 0))
    tab_t = pl.BlockSpec((QK_ROPE_DIM // 2, blk), lambda i: (0, i))
    consts = [prm["q_norm"], prm["kv_norm"], prm["w_qt"], prm["w_k"], prm["w_vt"]]
    return pl.pallas_call(
        _mla_prep_kernel,
        grid=(t // blk,),
        in_specs=[pl.BlockSpec((blk, MLA_PAD_COLS), lambda i: (i, 0)), tab, tab, tab, tab_t, tab_t]
        + [_const_spec(c.shape) for c in consts],
        out_specs=[pl.BlockSpec((1, MLA_HEADS, LANES, blk), lambda i: (i // n, 0, 0, i % n)),
                   pl.BlockSpec((1, MLA_HEADS, 1, blk, LANES), lambda i: (i // n, 0, i % n, 0, 0)),
                   pl.BlockSpec((1, MLA_HEADS // 2, 1, 2 * V_EXT_ROWS, blk), lambda i: (i // n, 0, i % n, 0, 0))],
        out_shape=[jax.ShapeDtypeStruct((batch, MLA_HEADS, LANES, seq), BF16),
                   jax.ShapeDtypeStruct((batch, MLA_HEADS, n, blk, LANES), BF16),
                   jax.ShapeDtypeStruct((batch, MLA_HEADS // 2, n, 2 * V_EXT_ROWS, blk), BF16)],
        compiler_params=_params("parallel"),
        name="mla_prep",
    )(p_mla, *natural, *transposed, *consts)


def _attn_kernel(qt_ref, k_ref, vt_ref, o_ref, m_ref, acc_ref, s0_ref, s1_ref):
    i = pl.program_id(2)
    blk = qt_ref.shape[3]
    m_ref[...] = jnp.full_like(m_ref, -jnp.inf)
    acc_ref[...] = jnp.zeros_like(acc_ref)

    def scores(j, s_ref):
        for h in range(2):
            s_ref[h] = jnp.dot(k_ref[0, h, j], qt_ref[0, h], preferred_element_type=F32)

    def update(j, s_ref, masked):
        for h in range(2):
            s = s_ref[h]
            if masked:
                k_pos = lax.broadcasted_iota(jnp.int32, (blk, blk), 0)
                q_pos = lax.broadcasted_iota(jnp.int32, (blk, blk), 1)
                s = jnp.where(k_pos <= q_pos, s, NEG_INF)
            m_old = m_ref[h]
            m_new = jnp.maximum(m_old, jnp.max(s, axis=0, keepdims=True))
            vt = vt_ref[0, 0, j, h * V_EXT_ROWS:(h + 1) * V_EXT_ROWS, :]
            pv = jnp.dot(vt, jnp.exp2(s - m_new).astype(BF16), preferred_element_type=F32)
            acc_ref[h] = jnp.exp2(m_old - m_new) * acc_ref[h] + pv
            m_ref[h] = m_new

    scores(0, s0_ref)

    def body(jj, carry):
        scores(2 * jj + 1, s1_ref)
        update(2 * jj, s0_ref, False)
        scores(2 * jj + 2, s0_ref)
        update(2 * jj + 1, s1_ref, False)
        return carry

    lax.fori_loop(0, i // 2, body, 0)

    @pl.when(i % 2 == 0)
    def _():
        update(i, s0_ref, True)

    @pl.when(i % 2 == 1)
    def _():
        scores(i, s1_ref)
        update(i - 1, s0_ref, False)
        update(i, s1_ref, True)

    o_t = jnp.concatenate([acc_ref[h, 0:V_HEAD_DIM] / acc_ref[h, V_HEAD_DIM:V_HEAD_DIM + 1]
                           for h in range(2)], axis=0)
    o_ref[0] = o_t.T.astype(BF16)


def _mla_attn(qt, k, vt, batch, seq, blk):
    n = seq // blk
    hv = MLA_HEADS * V_HEAD_DIM
    o = pl.pallas_call(
        _attn_kernel,
        grid=(batch, MLA_HEADS // 2, n),
        in_specs=[pl.BlockSpec((1, 2, LANES, blk), lambda b, p, i: (b, p, 0, i)),
                  pl.BlockSpec((1, 2, n, blk, LANES), lambda b, p, i: (b, p, 0, 0, 0)),
                  pl.BlockSpec((1, 1, n, 2 * V_EXT_ROWS, blk), lambda b, p, i: (b, p, 0, 0, 0))],
        out_specs=pl.BlockSpec((1, blk, LANES), lambda b, p, i: (b, i, p)),
        out_shape=jax.ShapeDtypeStruct((batch, seq, hv), BF16),
        scratch_shapes=[pltpu.VMEM((2, 1, blk), F32), pltpu.VMEM((2, V_EXT_ROWS, blk), F32),
                        pltpu.VMEM((2, blk, blk), F32), pltpu.VMEM((2, blk, blk), F32)],
        compiler_params=_params("parallel", "parallel", "arbitrary"),
        name="mla_attn",
    )(qt, k, vt)
    return o.reshape(batch * seq, hv)


def _merge_kernel(y_ref, bonus_ref, gate_ref, o_ref, g_ref, h_ref, lnw_ref, lnb_ref, ones_ref,
                  woa_ref, wob_ref, wout_ref, out_ref):
    ones = ones_ref[...]
    inv_n = 1.0 / RWKV_HEAD_DIM
    y = y_ref[...]
    mu = _head_sums(y, ones) * inv_n
    d = y - mu
    var = _head_sums(d * d, ones) * inv_n
    yn = (d * lax.rsqrt(var + LNX_EPS)) * lnw_ref[...] + lnb_ref[...]
    z = (yn + bonus_ref[...]) * gate_ref[...]
    y_a = _dot(z, woa_ref[...])
    y_b = jnp.dot(o_ref[...], wob_ref[...], preferred_element_type=F32)
    g = g_ref[...].astype(F32)
    merged = jax.nn.sigmoid(g[:, 0:D_MODEL]) * y_a + jax.nn.sigmoid(g[:, D_MODEL:2 * D_MODEL]) * y_b
    out_ref[...] = h_ref[...] + _dot(merged, wout_ref[...])


def _merge(y, bonus, gate, o_mla, gates, h, prm, tm):
    t = h.shape[0]
    row = lambda c: pl.BlockSpec((tm, c), lambda i: (i, 0))
    consts = [prm["lnx_w"], prm["lnx_b"], prm["head_ones"], prm["w_oa"], prm["w_ob"], prm["w_out"]]
    return pl.pallas_call(
        _merge_kernel,
        grid=(t // tm,),
        in_specs=[row(RWKV_DIM)] * 4 + [row(2 * D_MODEL), row(D_MODEL)]
        + [_const_spec(c.shape) for c in consts],
        out_specs=row(D_MODEL),
        out_shape=jax.ShapeDtypeStruct((t, D_MODEL), F32),
        compiler_params=_params("parallel"),
        name="merge",
    )(y, bonus, gate, o_mla, gates, h, *consts)


def _ffn_kernel(blocks_per_seq, final, h_ref, g_ref, wg_ref, wv_ref, cw_ref, cb_ref, wd_ref, fg_ref,
                out_ref, n_ref, acc_ref, carry_ref):
    i = pl.program_id(0)
    f = pl.program_id(1)
    tm = h_ref.shape[0]
    rows = min(FFN_SUB_ROWS, tm)

    @pl.when(f == 0)
    def _():
        n_ref[...] = _rms(h_ref[...], g_ref[...]).astype(BF16)
        acc_ref[...] = jnp.zeros_like(acc_ref)

    def up(sb):
        n = n_ref[sb * rows:(sb + 1) * rows, :]
        return (jnp.dot(n, wg_ref[...], preferred_element_type=F32),
                jnp.dot(n, wv_ref[...], preferred_element_type=F32))

    tail = carry_ref[f]
    tail = jnp.where(i % blocks_per_seq == 0, jnp.zeros_like(tail), tail)
    cw = cw_ref[0]
    row = lax.broadcasted_iota(jnp.int32, (rows, 1), 0)
    sqrt_half = np.sqrt(0.5).astype(np.float32)
    nxt = up(0)
    for sb in range(tm // rows):
        u_gate, u_val = nxt
        if sb + 1 < tm // rows:
            nxt = up(sb + 1)
        back1 = jnp.where(row == 0, tail[SUBLANES - 1:SUBLANES], pltpu.roll(u_gate, 1, axis=0))
        back2 = jnp.where(row == 0, tail[SUBLANES - 2:SUBLANES - 1],
                          jnp.where(row == 1, tail[SUBLANES - 1:SUBLANES], pltpu.roll(u_gate, 2, axis=0)))
        c = cb_ref[0] + cw[0:1] * back2
        c = c + cw[1:2] * back1
        c = c + cw[2:3] * u_gate
        tail = u_gate[rows - SUBLANES:rows]
        act = 0.5 * c * (1.0 + lax.erf(c * sqrt_half))
        down = jnp.dot((act * u_val).astype(BF16), wd_ref[...], preferred_element_type=F32)
        acc_ref[sb * rows:(sb + 1) * rows, :] += down
    carry_ref[f] = tail

    @pl.when(f == pl.num_programs(1) - 1)
    def _():
        out = h_ref[...] + acc_ref[...]
        out_ref[...] = _rms(out, fg_ref[...]) if final else out


def _conv_ffn(h, prm, final_gain, seq, tm, fc):
    t = h.shape[0]
    nf = FFN_DIM // fc
    row = pl.BlockSpec((tm, D_MODEL), lambda i, f: (i, 0))
    final = final_gain is not None
    fg = final_gain.reshape(1, D_MODEL) if final else prm["ffn_norm"]
    return pl.pallas_call(
        functools.partial(_ffn_kernel, seq // tm, final),
        grid=(t // tm, nf),
        in_specs=[row, _const_spec((1, D_MODEL)),
                  pl.BlockSpec((D_MODEL, fc), lambda i, f: (0, f)),
                  pl.BlockSpec((D_MODEL, fc), lambda i, f: (0, nf + f)),
                  pl.BlockSpec((1, SUBLANES, fc), lambda i, f: (f, 0, 0)),
                  pl.BlockSpec((1, 1, fc), lambda i, f: (f, 0, 0)),
                  pl.BlockSpec((fc, D_MODEL), lambda i, f: (f, 0)),
                  _const_spec((1, D_MODEL))],
        out_specs=row,
        out_shape=jax.ShapeDtypeStruct((t, D_MODEL), F32),
        scratch_shapes=[pltpu.VMEM((tm, D_MODEL), BF16), pltpu.VMEM((tm, D_MODEL), F32),
                        pltpu.VMEM((nf, SUBLANES, fc), F32)],
        compiler_params=_params("arbitrary", "arbitrary"),
        name="conv_ffn",
    )(h, prm["ffn_norm"], prm["w_up"], prm["w_up"], prm["conv_w"], prm["conv_b"], prm["w_down"], fg)


def _hi_lo(w):
    hi = w.astype(BF16)
    return hi, (w - hi.astype(F32)).astype(BF16)


def _layer_params(l, fc, attn_norm, w_in, mu_shift, decay_base, w_decay_up, iclr_base, w_iclr_up,
                  w_gate_up, k_k, k_a, r_k, lnx_w, lnx_b, w_out_rwkv, q_norm, w_q_up, kv_norm,
                  w_kv_up, w_out_mla, w_out, ffn_norm, w_ffn_up, conv_w, conv_b, w_ffn_down):
    row = lambda x: x.reshape(1, -1)
    w = w_in[l]
    w_mla_src = w[:, SHIFT_COLS:SHIFT_COLS + MLA_COLS]
    lat = Q_LORA_RANK + KV_LORA_RANK
    w_mla = jnp.zeros((D_MODEL, MLA_PAD_COLS), F32)
    w_mla = w_mla.at[:, 0:lat].set(w_mla_src[:, 0:lat])
    w_mla = w_mla.at[:, lat + ROPE_LANE0:lat + ROPE_LANE0 + QK_ROPE_DIM].set(w_mla_src[:, lat:])
    w_lora = jnp.zeros((DECAY_LORA + ICLR_LORA, 2 * RWKV_DIM), F32)
    w_lora = w_lora.at[0:DECAY_LORA, 0:RWKV_DIM].set(w_decay_up[l])
    w_lora = w_lora.at[DECAY_LORA:, RWKV_DIM:].set(w_iclr_up[l])
    wl_hi, wl_lo = _hi_lo(w_lora)
    wg_hi, wg_lo = _hi_lo(w_gate_up[l])
    head = jnp.arange(MXU_WIDTH) // RWKV_HEAD_DIM
    head_ones = (head[:, None] == head[None, :]).astype(BF16)
    qk = QK_NOPE_DIM + QK_ROPE_DIM
    w_q = jnp.pad(w_q_up[l].reshape(Q_LORA_RANK, MLA_HEADS, qk), ((0, 0), (0, 0), (0, LANES - qk)))
    w_kv = w_kv_up[l].reshape(KV_LORA_RANK, MLA_HEADS, QK_NOPE_DIM + V_HEAD_DIM)
    w_k = jnp.pad(w_kv[:, :, :QK_NOPE_DIM], ((0, 0), (0, 0), (0, LANES - QK_NOPE_DIM)))
    nf = FFN_DIM // fc
    return dict(
        attn_norm=row(attn_norm[l]),
        w_rw=w[:, :SHIFT_COLS].astype(BF16),
        w_mla=w_mla.astype(BF16),
        w_g=w[:, SHIFT_COLS + MLA_COLS:].astype(BF16),
        mu=row(mu_shift[l]), wl_hi=wl_hi, wl_lo=wl_lo, wg_hi=wg_hi, wg_lo=wg_lo,
        decay_base=row(decay_base[l]), iclr_base=row(iclr_base[l]), k_k=row(k_k[l]), k_a=row(k_a[l]),
        r_k=row(r_k[l]), head_ones=head_ones, lnx_w=row(lnx_w[l]), lnx_b=row(lnx_b[l]),
        q_norm=row(q_norm[l]), kv_norm=row(kv_norm[l]),
        w_qt=w_q.reshape(Q_LORA_RANK, MLA_HEADS * LANES).T.astype(BF16),
        w_k=w_k.reshape(KV_LORA_RANK, MLA_HEADS * LANES).astype(BF16),
        w_vt=w_kv[:, :, QK_NOPE_DIM:].reshape(KV_LORA_RANK, MLA_HEADS * V_HEAD_DIM).T.astype(BF16),
        w_oa=w_out_rwkv[l].astype(BF16), w_ob=w_out_mla[l].astype(BF16), w_out=w_out[l].astype(BF16),
        ffn_norm=row(ffn_norm[l]), w_up=w_ffn_up[l].astype(BF16), w_down=w_ffn_down[l].astype(BF16),
        conv_w=jnp.pad(conv_w[l].reshape(3, nf, fc).transpose(1, 0, 2), ((0, 0), (0, SUBLANES - 3), (0, 0))),
        conv_b=conv_b[l].reshape(nf, 1, fc),
    )


def _tile_plan(seq):
    pick = lambda want: min(want, seq)
    return dict(proj=pick(512), attn=pick(512), merge=pick(512), ffn=pick(1024),
                ffn_cols=FFN_DIM // 2)


def kernel(x, positions, attn_norm, w_in, mu_shift, decay_base, w_decay_up, iclr_base, w_iclr_up, w_gate_up, k_k, k_a, r_k, lnx_w, lnx_b, w_out_rwkv, q_norm, w_q_up, kv_norm, w_kv_up, w_out_mla, w_out, ffn_norm, w_ffn_up, conv_w, conv_b, w_ffn_down, final_norm):
    batch, seq, _ = x.shape
    depth = w_in.shape[0]
    assert seq % WKV_CHUNK == 0 and x.shape[2] == D_MODEL
    plan = _tile_plan(seq)
    tables = _rope_tables(positions)
    h = x.reshape(batch * seq, D_MODEL)
    for l in range(depth):
        prm = _layer_params(l, plan["ffn_cols"], attn_norm, w_in, mu_shift, decay_base, w_decay_up,
                            iclr_base, w_iclr_up, w_gate_up, k_k, k_a, r_k, lnx_w, lnx_b, w_out_rwkv,
                            q_norm, w_q_up, kv_norm, w_kv_up, w_out_mla, w_out, ffn_norm, w_ffn_up,
                            conv_w, conv_b, w_ffn_down)
        p_mla, gates, r, lw, k, v, a, b, gate, bonus = _inproj(h, prm, seq, plan["proj"])
        y = _wkv(r, lw, k, v, a, b, batch, seq)
        q_h, k_h, v_h = _mla_prep(p_mla, tables, prm, batch, seq, plan["attn"])
        o_mla = _mla_attn(q_h, k_h, v_h, batch, seq, plan["attn"])
        h = _merge(y, bonus, gate, o_mla, gates, h, prm, plan["merge"])
        last = final_norm if l == depth - 1 else None
        h = _conv_ffn(h, prm, last, seq, plan["ffn"], plan["ffn_cols"])
    return h.reshape(batch, seq, D_MODEL)
```

```python
import functools

import jax
import jax.numpy as jnp
import numpy as np
from jax import lax
from jax.experimental import pallas as pl
from jax.experimental.pallas import tpu as pltpu

F32 = jnp.float32
BF16 = jnp.bfloat16

D_MODEL = 1024
RWKV_HEADS = 8
RWKV_HEAD_DIM = 64
RWKV_DIM = RWKV_HEADS * RWKV_HEAD_DIM
DECAY_LORA = 64
ICLR_LORA = 64
GATE_LORA = 128
LNX_EPS = 64e-5
MLA_HEADS = 8
QK_NOPE_DIM = 64
QK_ROPE_DIM = 32
V_HEAD_DIM = 64
Q_LORA_RANK = 256
KV_LORA_RANK = 128
ROPE_THETA = 10000.0
NEG_INF = -1e30
FFN_DIM = 2816
NORM_EPS = 1e-6
SHIFT_COLS = 3 * RWKV_DIM + DECAY_LORA + ICLR_LORA + GATE_LORA
MLA_COLS = Q_LORA_RANK + KV_LORA_RANK + QK_ROPE_DIM

LANES = 128
SUBLANES = 8
MXU_WIDTH = 256
VMEM_LIMIT_BYTES = 56 * 1024 * 1024

MLA_PAD_COLS = 4 * LANES
ROPE_LANE0 = QK_NOPE_DIM
WKV_CHUNK = 64
PAIR = 2 * RWKV_HEAD_DIM
LOG2_E = float(np.log2(np.e))
V_ONES_ROWS = 16
V_EXT_ROWS = V_HEAD_DIM + V_ONES_ROWS
FFN_SUB_ROWS = 256


def _params(*semantics, flags=None):
    return pltpu.CompilerParams(dimension_semantics=semantics, vmem_limit_bytes=VMEM_LIMIT_BYTES,
                                flags=flags)


def _dot(a, b):
    return jnp.dot(a.astype(BF16), b.astype(BF16), preferred_element_type=F32)


def _dot_nt(a, b):
    return lax.dot_general(a.astype(BF16), b.astype(BF16), (((1,), (1,)), ((), ())),
                           preferred_element_type=F32)


def _split_bf16(x, terms):
    parts = []
    rest = x
    for _ in range(terms):
        p = rest.astype(BF16)
        parts.append(p)
        rest = rest - p.astype(F32)
    return parts


def _dot_split_lhs(a, b_bf16, terms):
    out = None
    for p in _split_bf16(a, terms):
        d = jnp.dot(p, b_bf16, preferred_element_type=F32)
        out = d if out is None else out + d
    return out


def _head_sums(x, ones):
    w = ones.shape[0]
    return jnp.concatenate([_dot_split_lhs(x[:, c:c + w], ones, 2) for c in range(0, x.shape[1], w)],
                           axis=1)


def _dot_x3(a, b_hi, b_lo):
    a_hi, a_lo = _split_bf16(a, 2)
    return (jnp.dot(a_hi, b_hi, preferred_element_type=F32)
            + jnp.dot(a_hi, b_lo, preferred_element_type=F32)
            + jnp.dot(a_lo, b_hi, preferred_element_type=F32))


def _rms(x, g):
    ms = jnp.mean(x * x, axis=-1, keepdims=True)
    return (x * lax.rsqrt(ms + NORM_EPS)) * g


def _const_spec(shape):
    return pl.BlockSpec(shape, lambda *_: (0,) * len(shape))


def _rope_tables_kernel(pos_ref, invf_ref, cos_ref, sin_ref, nsin_ref):
    ang = pos_ref[...] * invf_ref[...]
    s = jnp.sin(ang)
    cos_ref[...] = jnp.cos(ang)
    sin_ref[...] = s
    nsin_ref[...] = -s


def _rope_tables(positions):
    t = positions.size
    half = QK_ROPE_DIM // 2
    per_row = LANES // half
    inv_freq = jnp.power(ROPE_THETA, -jnp.arange(0, QK_ROPE_DIM, 2, dtype=F32) / QK_ROPE_DIM)
    invf = jnp.tile(inv_freq, per_row).reshape(1, LANES)
    posc = jnp.repeat(positions.reshape(t // per_row, per_row).astype(F32), half, axis=1)
    spec = pl.BlockSpec(posc.shape, lambda i: (0, 0))
    out = jax.ShapeDtypeStruct(posc.shape, F32)
    cos, sin, nsin = (x.reshape(t, half) for x in pl.pallas_call(
        _rope_tables_kernel,
        grid=(1,),
        in_specs=[spec, _const_spec((1, LANES))],
        out_specs=[spec, spec, spec],
        out_shape=[out, out, out],
        compiler_params=_params("arbitrary"),
        name="rope_tables",
    )(posc, invf))
    pad = lambda x, fill: jnp.pad(x, ((0, 0), (ROPE_LANE0, LANES - ROPE_LANE0 - x.shape[1])),
                                  constant_values=fill)
    zeros = jnp.zeros_like(sin)
    natural = (pad(jnp.concatenate([cos, cos], axis=1), 1.0),
               pad(jnp.concatenate([nsin, zeros], axis=1), 0.0),
               pad(jnp.concatenate([zeros, sin], axis=1), 0.0))
    return natural, (cos.T, sin.T)


def _rope(tile, cos, sina, sinb):
    half = QK_ROPE_DIM // 2
    return (tile * cos + pltpu.roll(tile, LANES - half, axis=1) * sina
            + pltpu.roll(tile, half, axis=1) * sinb)


def _inproj_kernel(blocks_per_seq, x_ref, g_ref, wrw_ref, wmla_ref, wgate_ref, mu_ref, wl_hi_ref,
                   wl_lo_ref, wg_hi_ref, wg_lo_ref, dbase_ref, ibase_ref, kk_ref, ka_ref, rk_ref, ones_ref,
                   pmla_ref, pg_ref, r_ref, lw_ref, k_ref, v_ref, a_ref, b_ref, gate_ref, bonus_ref,
                   tail_ref):
    i = pl.program_id(0)
    n = _rms(x_ref[...], g_ref[...]).astype(BF16)
    pmla_ref[...] = jnp.dot(n, wmla_ref[...], preferred_element_type=F32)
    pg_ref[...] = jnp.dot(n, wgate_ref[...], preferred_element_type=F32).astype(BF16)
    x = jnp.dot(n, wrw_ref[...], preferred_element_type=F32)
    tm = x.shape[0]
    prev = tail_ref[SUBLANES - 1:SUBLANES, :]
    prev = jnp.where(i % blocks_per_seq == 0, jnp.zeros_like(prev), prev)
    tail_ref[...] = x[tm - SUBLANES:tm]
    row = lax.broadcasted_iota(jnp.int32, (tm, 1), 0)
    shifted = jnp.where(row == 0, prev, pltpu.roll(x, 1, axis=0))
    xs = x + (shifted - x) * mu_ref[...]

    c0 = 3 * RWKV_DIM
    p_r = xs[:, 0:RWKV_DIM]
    p_k = xs[:, RWKV_DIM:2 * RWKV_DIM]
    p_v = xs[:, 2 * RWKV_DIM:c0]
    p_wa = xs[:, c0:c0 + DECAY_LORA + ICLR_LORA]
    p_g = xs[:, c0 + DECAY_LORA + ICLR_LORA:SHIFT_COLS]

    lane = lax.broadcasted_iota(jnp.int32, p_wa.shape, 1)
    lora_in = jnp.where(lane < DECAY_LORA, jnp.tanh(p_wa), p_wa)
    lora = _dot_x3(lora_in, wl_hi_ref[...], wl_lo_ref[...])
    log_w = -jax.nn.softplus(-(dbase_ref[...] + lora[:, 0:RWKV_DIM])) - 0.5
    iclr = jax.nn.sigmoid(ibase_ref[...] + lora[:, RWKV_DIM:2 * RWKV_DIM])
    gate = _dot_x3(jax.nn.sigmoid(p_g), wg_hi_ref[...], wg_lo_ref[...])

    ones = ones_ref[...]
    kk = p_k * kk_ref[...]
    kk = kk * lax.rsqrt(_head_sums(kk * kk, ones) + 1e-12)
    k = p_k * (1.0 + (iclr - 1.0) * ka_ref[...])

    r_ref[...] = p_r
    lw_ref[...] = -jnp.exp(log_w)
    k_ref[...] = k
    v_ref[...] = p_v
    a_ref[...] = -kk
    b_ref[...] = kk * iclr
    gate_ref[...] = gate
    bonus_ref[...] = _head_sums(p_r * k * rk_ref[...], ones) * p_v


def _inproj(h, prm, seq, tm):
    t = h.shape[0]
    row = lambda c: pl.BlockSpec((tm, c), lambda i: (i, 0))
    out = lambda c, dt: jax.ShapeDtypeStruct((t, c), dt)
    consts = [prm["attn_norm"], prm["w_rw"], prm["w_mla"], prm["w_g"], prm["mu"], prm["wl_hi"],
              prm["wl_lo"], prm["wg_hi"], prm["wg_lo"], prm["decay_base"], prm["iclr_base"], prm["k_k"],
              prm["k_a"], prm["r_k"], prm["head_ones"]]
    return pl.pallas_call(
        functools.partial(_inproj_kernel, seq // tm),
        grid=(t // tm,),
        in_specs=[row(D_MODEL)] + [_const_spec(c.shape) for c in consts],
        out_specs=[row(MLA_PAD_COLS), row(2 * D_MODEL)] + [row(RWKV_DIM)] * 8,
        out_shape=[out(MLA_PAD_COLS, F32), out(2 * D_MODEL, BF16)] + [out(RWKV_DIM, F32)] * 8,
        scratch_shapes=[pltpu.VMEM((SUBLANES, SHIFT_COLS), F32)],
        compiler_params=_params("arbitrary"),
        name="inproj",
    )(h, *consts)


def _wkv_pair(load, state_ref, idx, store_y, zbuf, ltri, eye):
    c = WKV_CHUNK
    lane = lax.broadcasted_iota(jnp.int32, (1, PAIR), 1)
    lo = lane < RWKV_HEAD_DIM
    r, lw, k, v, a, b = load()

    cum = _dot_split_lhs_left(ltri, lw)
    yield
    cum_last = cum[c - 1:c, :]
    g_inv = jnp.exp(-cum)
    a_t = a * jnp.exp(cum - lw)
    r_t = r * jnp.exp(cum)
    zero = jnp.zeros_like(a_t)
    lhs = jnp.concatenate([jnp.where(lo, a_t, zero), jnp.where(lo, zero, a_t),
                           jnp.where(lo, r_t, zero), jnp.where(lo, zero, r_t)], axis=0)
    b_t = b * g_inv
    k_t = k * g_inv
    a_all = _dot_nt(lhs, jnp.concatenate([b_t, k_t], axis=0))
    yield
    swapped = pltpu.roll(a_all, c, axis=1)
    head0_rows = (lax.broadcasted_iota(jnp.int32, (4 * c, 1), 0) // c) % 2 == 0
    a_b = jnp.where(head0_rows, a_all, swapped)
    a_k = jnp.where(head0_rows, swapped, a_all)

    rr = lax.broadcasted_iota(jnp.int32, (2 * c, 2 * c), 0)
    cc = lax.broadcasted_iota(jnp.int32, (2 * c, 2 * c), 1)
    same_head = (rr // c) == (cc // c)
    strict = same_head & ((cc % c) < (rr % c))
    incl = same_head & ((cc % c) <= (rr % c))
    l_bd = jnp.where(strict, a_b[0:2 * c], 0.0)
    aak_bd = jnp.where(strict, a_k[0:2 * c], 0.0)
    arb_bd = jnp.where(incl, a_b[2 * c:4 * c], 0.0)
    ark_bd = jnp.where(incl, a_k[2 * c:4 * c], 0.0)
    v_st = jnp.concatenate([jnp.where(lo, v, zero), jnp.where(lo, zero, v)], axis=0)
    av = _dot(jnp.concatenate([aak_bd, ark_bd], axis=0), v_st)
    q_st = av[0:2 * c]
    arkv_st = av[2 * c:4 * c]

    t_bd = eye + l_bd
    p = _dot(l_bd, l_bd)
    yield
    n = 2
    while 2 * n < c:
        pp_pt = _dot(p, jnp.concatenate([p, t_bd], axis=1))
        yield
        p = pp_pt[:, 0:2 * c]
        t_bd = t_bd + pp_pt[:, 2 * c:4 * c]
        n *= 2
    t_bd = t_bd + _dot(p, t_bd)
    yield

    tq_ta = _dot(t_bd, jnp.concatenate([lhs[0:2 * c], q_st], axis=1))
    yield
    rq_ra = _dot(arb_bd, tq_ta)
    yield
    ta_st = tq_ta[:, 0:PAIR]
    tq_st = tq_ta[:, PAIR:2 * PAIR]
    rh_st = lhs[2 * c:4 * c] + rq_ra[:, 0:PAIR]
    yh_st = rq_ra[:, PAIR:2 * PAIR] + arkv_st
    unstack = lambda x: x[0:c] + x[c:2 * c]
    ta = unstack(ta_st)
    rh = unstack(rh_st)
    tq = unstack(tq_st)
    yh = unstack(yh_st)

    state = state_ref[idx]
    zbuf[...] = _dot_nt(state, jnp.concatenate([ta, rh], axis=0))
    yield
    z = zbuf[...] + jnp.concatenate([tq, yh], axis=0).T
    v_t = jnp.concatenate([v, v], axis=0).T
    lane2 = lax.broadcasted_iota(jnp.int32, (1, 2 * c), 1)
    g_out = jnp.exp(cum_last - cum)
    upd = _dot(jnp.where(lane2 < c, z, v_t), jnp.concatenate([b * g_out, k * g_out], axis=0))
    yield
    state_ref[idx] = state * jnp.exp(cum_last) + jnp.where(same_head, upd, 0.0)
    store_y(z.T[c:2 * c, :])


def _dot_split_lhs_left(m_bf16, x):
    out = None
    for p in _split_bf16(x, 3):
        d = jnp.dot(m_bf16, p, preferred_element_type=F32)
        out = d if out is None else out + d
    return out


def _wkv_kernel(r_ref, lw_ref, k_ref, v_ref, a_ref, b_ref, ltri_ref, eye_ref, y_ref, state_ref, z_ref):
    @pl.when(pl.program_id(0) == 0)
    def _():
        state_ref[...] = jnp.zeros_like(state_ref)

    ltri = ltri_ref[...]
    eye = eye_ref[...]
    pairs = RWKV_DIM // PAIR
    chains = []
    for bi in range(r_ref.shape[0]):
        for p in range(pairs):
            sl = slice(p * PAIR, (p + 1) * PAIR)

            def load(bi=bi, sl=sl):
                return tuple(ref[bi, :, sl] for ref in (r_ref, lw_ref, k_ref, v_ref, a_ref, b_ref))

            def store_y(y, bi=bi, sl=sl):
                y_ref[bi, :, sl] = y

            chain = bi * pairs + p
            chains.append(_wkv_pair(load, state_ref, chain, store_y, z_ref.at[chain], ltri, eye))
    while chains:
        chains = [ch for ch in chains if next(ch, StopIteration) is not StopIteration]


def _wkv(r, lw, k, v, a, b, batch, seq):
    c = WKV_CHUNK
    shp = (batch, seq, RWKV_DIM)
    args = [x.reshape(shp) for x in (r, lw, k, v, a, b)]
    ltri = jnp.tril(jnp.ones((c, c), F32)).astype(BF16)
    eye = jnp.eye(2 * c, dtype=F32)
    blk = pl.BlockSpec((batch, c, RWKV_DIM), lambda ci: (0, ci, 0))
    y = pl.pallas_call(
        _wkv_kernel,
        grid=(seq // c,),
        in_specs=[blk] * 6 + [_const_spec((c, c)), _const_spec((2 * c, 2 * c))],
        out_specs=blk,
        out_shape=jax.ShapeDtypeStruct(shp, F32),
        scratch_shapes=[pltpu.VMEM((batch * (RWKV_DIM // PAIR), PAIR, PAIR), F32)] * 2,
        compiler_params=_params("arbitrary"),
        name="wkv",
    )(*args, ltri, eye)
    return y.reshape(batch * seq, RWKV_DIM)


def _mla_prep_kernel(p_ref, cos_ref, sina_ref, sinb_ref, cost_ref, sint_ref, qn_ref, kvn_ref, wqt_ref,
                     wk_ref, wvt_ref, qt_ref, k_ref, vt_ref):
    x = p_ref[...]
    blk = x.shape[0]
    scale = (QK_NOPE_DIM + QK_ROPE_DIM) ** -0.5 * LOG2_E
    qn = _rms(x[:, 0:Q_LORA_RANK], qn_ref[...]).astype(BF16)
    kvn = _rms(x[:, Q_LORA_RANK:Q_LORA_RANK + KV_LORA_RANK], kvn_ref[...]).astype(BF16)
    k_pe = _rope(x[:, Q_LORA_RANK + KV_LORA_RANK:MLA_PAD_COLS], cos_ref[...], sina_ref[...], sinb_ref[...])
    kf = jnp.dot(kvn, wk_ref[...], preferred_element_type=F32)
    for h in range(MLA_HEADS):
        k_ref[0, h, 0] = (kf[:, h * LANES:(h + 1) * LANES] + k_pe).astype(BF16)
    q_t = _dot_nt(wqt_ref[...], qn)
    v_t = _dot_nt(wvt_ref[...], kvn)
    cos_t, sin_t = cost_ref[...], sint_ref[...]
    half = QK_ROPE_DIM // 2
    r0 = ROPE_LANE0
    for h in range(MLA_HEADS):
        tile = q_t[h * LANES:(h + 1) * LANES]
        x1 = tile[r0:r0 + half]
        x2 = tile[r0 + half:r0 + QK_ROPE_DIM]
        roped = jnp.concatenate([tile[0:r0], x1 * cos_t - x2 * sin_t, x1 * sin_t + x2 * cos_t,
                                 tile[r0 + QK_ROPE_DIM:]], axis=0)
        qt_ref[0, h] = (roped * scale).astype(BF16)
    ones = jnp.ones((V_ONES_ROWS, blk), F32)
    for h in range(0, MLA_HEADS, 2):
        rows = lambda g: v_t[g * V_HEAD_DIM:(g + 1) * V_HEAD_DIM]
        vt_ref[0, h // 2, 0] = jnp.concatenate([rows(h), ones, rows(h + 1), ones], axis=0).astype(BF16)


def _mla_prep(p_mla, tables, prm, batch, seq, blk):
    t = p_mla.shape[0]
    n = seq // blk
    natural, transposed = tables
    tab = pl.BlockSpec((blk, LANES), lambda i: (i, 0))
    tab_t = pl.BlockSpec((QK_ROPE_DIM // 2, blk), lambda i: (0, i))
    consts = [prm["q_norm"], prm["kv_norm"], prm["w_qt"], prm["w_k"], prm["w_vt"]]
    return pl.pallas_call(
        _mla_prep_kernel,
        grid=(t // blk,),
        in_specs=[pl.BlockSpec((blk, MLA_PAD_COLS), lambda i: (i, 0)), tab, tab, tab, tab_t, tab_t]
        + [_const_spec(c.shape) for c in consts],
        out_specs=[pl.BlockSpec((1, MLA_HEADS, LANES, blk), lambda i: (i // n, 0, 0, i % n)),
                   pl.BlockSpec((1, MLA_HEADS, 1, blk, LANES), lambda i: (i // n, 0, i % n, 0, 0)),
                   pl.BlockSpec((1, MLA_HEADS // 2, 1, 2 * V_EXT_ROWS, blk), lambda i: (i // n, 0, i % n, 0, 0))],
        out_shape=[jax.ShapeDtypeStruct((batch, MLA_HEADS, LANES, seq), BF16),
                   jax.ShapeDtypeStruct((batch, MLA_HEADS, n, blk, LANES), BF16),
                   jax.ShapeDtypeStruct((batch, MLA_HEADS // 2, n, 2 * V_EXT_ROWS, blk), BF16)],
        compiler_params=_params("parallel"),
        name="mla_prep",
    )(p_mla, *natural, *transposed, *consts)


def _attn_kernel(qt_ref, k_ref, vt_ref, o_ref, m_ref, acc_ref, s0_ref, s1_ref):
    i = pl.program_id(2)
    blk = qt_ref.shape[3]
    m_ref[...] = jnp.full_like(m_ref, -jnp.inf)
    acc_ref[...] = jnp.zeros_like(acc_ref)

    def scores(j, s_ref):
        for h in range(2):
            s_ref[h] = jnp.dot(k_ref[0, h, j], qt_ref[0, h], preferred_element_type=F32)

    def update(j, s_ref, masked):
        for h in range(2):
            s = s_ref[h]
            if masked:
                k_pos = lax.broadcasted_iota(jnp.int32, (blk, blk), 0)
                q_pos = lax.broadcasted_iota(jnp.int32, (blk, blk), 1)
                s = jnp.where(k_pos <= q_pos, s, NEG_INF)
            m_old = m_ref[h]
            m_new = jnp.maximum(m_old, jnp.max(s, axis=0, keepdims=True))
            vt = vt_ref[0, 0, j, h * V_EXT_ROWS:(h + 1) * V_EXT_ROWS, :]
            pv = jnp.dot(vt, jnp.exp2(s - m_new).astype(BF16), preferred_element_type=F32)
            acc_ref[h] = jnp.exp2(m_old - m_new) * acc_ref[h] + pv
            m_ref[h] = m_new

    scores(0, s0_ref)

    def body(jj, carry):
        scores(2 * jj + 1, s1_ref)
        update(2 * jj, s0_ref, False)
        scores(2 * jj + 2, s0_ref)
        update(2 * jj + 1, s1_ref, False)
        return carry

    lax.fori_loop(0, i // 2, body, 0)

    @pl.when(i % 2 == 0)
    def _():
        update(i, s0_ref, True)

    @pl.when(i % 2 == 1)
    def _():
        scores(i, s1_ref)
        update(i - 1, s0_ref, False)
        update(i, s1_ref, True)

    o_t = jnp.concatenate([acc_ref[h, 0:V_HEAD_DIM] / acc_ref[h, V_HEAD_DIM:V_HEAD_DIM + 1]
                           for h in range(2)], axis=0)
    o_ref[0] = o_t.T.astype(BF16)


def _mla_attn(qt, k, vt, batch, seq, blk):
    n = seq // blk
    hv = MLA_HEADS * V_HEAD_DIM
    o = pl.pallas_call(
        _attn_kernel,
        grid=(batch, MLA_HEADS // 2, n),
        in_specs=[pl.BlockSpec((1, 2, LANES, blk), lambda b, p, i: (b, p, 0, i)),
                  pl.BlockSpec((1, 2, n, blk, LANES), lambda b, p, i: (b, p, 0, 0, 0)),
                  pl.BlockSpec((1, 1, n, 2 * V_EXT_ROWS, blk), lambda b, p, i: (b, p, 0, 0, 0))],
        out_specs=pl.BlockSpec((1, blk, LANES), lambda b, p, i: (b, i, p)),
        out_shape=jax.ShapeDtypeStruct((batch, seq, hv), BF16),
        scratch_shapes=[pltpu.VMEM((2, 1, blk), F32), pltpu.VMEM((2, V_EXT_ROWS, blk), F32),
                        pltpu.VMEM((2, blk, blk), F32), pltpu.VMEM((2, blk, blk), F32)],
        compiler_params=_params("parallel", "parallel", "arbitrary"),
        name="mla_attn",
    )(qt, k, vt)
    return o.reshape(batch * seq, hv)


def _merge_kernel(y_ref, bonus_ref, gate_ref, o_ref, g_ref, h_ref, lnw_ref, lnb_ref, ones_ref,
                  woa_ref, wob_ref, wout_ref, out_ref):
    ones = ones_ref[...]
    inv_n = 1.0 / RWKV_HEAD_DIM
    y = y_ref[...]
    mu = _head_sums(y, ones) * inv_n
    d = y - mu
    var = _head_sums(d * d, ones) * inv_n
    yn = (d * lax.rsqrt(var + LNX_EPS)) * lnw_ref[...] + lnb_ref[...]
    z = (yn + bonus_ref[...]) * gate_ref[...]
    y_a = _dot(z, woa_ref[...])
    y_b = jnp.dot(o_ref[...], wob_ref[...], preferred_element_type=F32)
    g = g_ref[...].astype(F32)
    merged = jax.nn.sigmoid(g[:, 0:D_MODEL]) * y_a + jax.nn.sigmoid(g[:, D_MODEL:2 * D_MODEL]) * y_b
    out_ref[...] = h_ref[...] + _dot(merged, wout_ref[...])


def _merge(y, bonus, gate, o_mla, gates, h, prm, tm):
    t = h.shape[0]
    row = lambda c: pl.BlockSpec((tm, c), lambda i: (i, 0))
    consts = [prm["lnx_w"], prm["lnx_b"], prm["head_ones"], prm["w_oa"], prm["w_ob"], prm["w_out"]]
    return pl.pallas_call(
        _merge_kernel,
        grid=(t // tm,),
        in_specs=[row(RWKV_DIM)] * 4 + [row(2 * D_MODEL), row(D_MODEL)]
        + [_const_spec(c.shape) for c in consts],
        out_specs=row(D_MODEL),
        out_shape=jax.ShapeDtypeStruct((t, D_MODEL), F32),
        compiler_params=_params("parallel"),
        name="merge",
    )(y, bonus, gate, o_mla, gates, h, *consts)


def _ffn_kernel(blocks_per_seq, final, h_ref, g_ref, wg_ref, wv_ref, cw_ref, cb_ref, wd_ref, fg_ref,
                out_ref, n_ref, acc_ref, carry_ref):
    i = pl.program_id(0)
    f = pl.program_id(1)
    tm = h_ref.shape[0]
    rows = min(FFN_SUB_ROWS, tm)

    @pl.when(f == 0)
    def _():
        n_ref[...] = _rms(h_ref[...], g_ref[...]).astype(BF16)
        acc_ref[...] = jnp.zeros_like(acc_ref)

    def up(sb):
        n = n_ref[sb * rows:(sb + 1) * rows, :]
        return (jnp.dot(n, wg_ref[...], preferred_element_type=F32),
                jnp.dot(n, wv_ref[...], preferred_element_type=F32))

    tail = carry_ref[f]
    tail = jnp.where(i % blocks_per_seq == 0, jnp.zeros_like(tail), tail)
    cw = cw_ref[0]
    row = lax.broadcasted_iota(jnp.int32, (rows, 1), 0)
    sqrt_half = np.sqrt(0.5).astype(np.float32)
    nxt = up(0)
    for sb in range(tm // rows):
        u_gate, u_val = nxt
        if sb + 1 < tm // rows:
            nxt = up(sb + 1)
        back1 = jnp.where(row == 0, tail[SUBLANES - 1:SUBLANES], pltpu.roll(u_gate, 1, axis=0))
        back2 = jnp.where(row == 0, tail[SUBLANES - 2:SUBLANES - 1],
                          jnp.where(row == 1, tail[SUBLANES - 1:SUBLANES], pltpu.roll(u_gate, 2, axis=0)))
        c = cb_ref[0] + cw[0:1] * back2
        c = c + cw[1:2] * back1
        c = c + cw[2:3] * u_gate
        tail = u_gate[rows - SUBLANES:rows]
        act = 0.5 * c * (1.0 + lax.erf(c * sqrt_half))
        down = jnp.dot((act * u_val).astype(BF16), wd_ref[...], preferred_element_type=F32)
        acc_ref[sb * rows:(sb + 1) * rows, :] += down
    carry_ref[f] = tail

    @pl.when(f == pl.num_programs(1) - 1)
    def _():
        out = h_ref[...] + acc_ref[...]
        out_ref[...] = _rms(out, fg_ref[...]) if final else out


def _conv_ffn(h, prm, final_gain, seq, tm, fc):
    t = h.shape[0]
    nf = FFN_DIM // fc
    row = pl.BlockSpec((tm, D_MODEL), lambda i, f: (i, 0))
    final = final_gain is not None
    fg = final_gain.reshape(1, D_MODEL) if final else prm["ffn_norm"]
    return pl.pallas_call(
        functools.partial(_ffn_kernel, seq // tm, final),
        grid=(t // tm, nf),
        in_specs=[row, _const_spec((1, D_MODEL)),
                  pl.BlockSpec((D_MODEL, fc), lambda i, f: (0, f)),
                  pl.BlockSpec((D_MODEL, fc), lambda i, f: (0, nf + f)),
                  pl.BlockSpec((1, SUBLANES, fc), lambda i, f: (f, 0, 0)),
                  pl.BlockSpec((1, 1, fc), lambda i, f: (f, 0, 0)),
                  pl.BlockSpec((fc, D_MODEL), lambda i, f: (f, 0)),
                  _const_spec((1, D_MODEL))],
        out_specs=row,
        out_shape=jax.ShapeDtypeStruct((t, D_MODEL), F32),
        scratch_shapes=[pltpu.VMEM((tm, D_MODEL), BF16), pltpu.VMEM((tm, D_MODEL), F32),
                        pltpu.VMEM((nf, SUBLANES, fc), F32)],
        compiler_params=_params("arbitrary", "arbitrary"),
        name="conv_ffn",
    )(h, prm["ffn_norm"], prm["w_up"], prm["w_up"], prm["conv_w"], prm["conv_b"], prm["w_down"], fg)


def _hi_lo(w):
    hi = w.astype(BF16)
    return hi, (w - hi.astype(F32)).astype(BF16)


def _layer_params(l, fc, attn_norm, w_in, mu_shift, decay_base, w_decay_up, iclr_base, w_iclr_up,
                  w_gate_up, k_k, k_a, r_k, lnx_w, lnx_b, w_out_rwkv, q_norm, w_q_up, kv_norm,
                  w_kv_up, w_out_mla, w_out, ffn_norm, w_ffn_up, conv_w, conv_b, w_ffn_down):
    row = lambda x: x.reshape(1, -1)
    w = w_in[l]
    w_mla_src = w[:, SHIFT_COLS:SHIFT_COLS + MLA_COLS]
    lat = Q_LORA_RANK + KV_LORA_RANK
    w_mla = jnp.zeros((D_MODEL, MLA_PAD_COLS), F32)
    w_mla = w_mla.at[:, 0:lat].set(w_mla_src[:, 0:lat])
    w_mla = w_mla.at[:, lat + ROPE_LANE0:lat + ROPE_LANE0 + QK_ROPE_DIM].set(w_mla_src[:, lat:])
    w_lora = jnp.zeros((DECAY_LORA + ICLR_LORA, 2 * RWKV_DIM), F32)
    w_lora = w_lora.at[0:DECAY_LORA, 0:RWKV_DIM].set(w_decay_up[l])
    w_lora = w_lora.at[DECAY_LORA:, RWKV_DIM:].set(w_iclr_up[l])
    wl_hi, wl_lo = _hi_lo(w_lora)
    wg_hi, wg_lo = _hi_lo(w_gate_up[l])
    head = jnp.arange(MXU_WIDTH) // RWKV_HEAD_DIM
    head_ones = (head[:, None] == head[None, :]).astype(BF16)
    qk = QK_NOPE_DIM + QK_ROPE_DIM
    w_q = jnp.pad(w_q_up[l].reshape(Q_LORA_RANK, MLA_HEADS, qk), ((0, 0), (0, 0), (0, LANES - qk)))
    w_kv = w_kv_up[l].reshape(KV_LORA_RANK, MLA_HEADS, QK_NOPE_DIM + V_HEAD_DIM)
    w_k = jnp.pad(w_kv[:, :, :QK_NOPE_DIM], ((0, 0), (0, 0), (0, LANES - QK_NOPE_DIM)))
    nf = FFN_DIM // fc
    return dict(
        attn_norm=row(attn_norm[l]),
        w_rw=w[:, :SHIFT_COLS].astype(BF16),
        w_mla=w_mla.astype(BF16),
        w_g=w[:, SHIFT_COLS + MLA_COLS:].astype(BF16),
        mu=row(mu_shift[l]), wl_hi=wl_hi, wl_lo=wl_lo, wg_hi=wg_hi, wg_lo=wg_lo,
        decay_base=row(decay_base[l]), iclr_base=row(iclr_base[l]), k_k=row(k_k[l]), k_a=row(k_a[l]),
        r_k=row(r_k[l]), head_ones=head_ones, lnx_w=row(lnx_w[l]), lnx_b=row(lnx_b[l]),
        q_norm=row(q_norm[l]), kv_norm=row(kv_norm[l]),
        w_qt=w_q.reshape(Q_LORA_RANK, MLA_HEADS * LANES).T.astype(BF16),
        w_k=w_k.reshape(KV_LORA_RANK, MLA_HEADS * LANES).astype(BF16),
        w_vt=w_kv[:, :, QK_NOPE_DIM:].reshape(KV_LORA_RANK, MLA_HEADS * V_HEAD_DIM).T.astype(BF16),
        w_oa=w_out_rwkv[l].astype(BF16), w_ob=w_out_mla[l].astype(BF16), w_out=w_out[l].astype(BF16),
        ffn_norm=row(ffn_norm[l]), w_up=w_ffn_up[l].astype(BF16), w_down=w_ffn_down[l].astype(BF16),
        conv_w=jnp.pad(conv_w[l].reshape(3, nf, fc).transpose(1, 0, 2), ((0, 0), (0, SUBLANES - 3), (0, 0))),
        conv_b=conv_b[l].reshape(nf, 1, fc),
    )


def _tile_plan(seq):
    pick = lambda want: min(want, seq)
    return dict(proj=pick(512), attn=pick(512), merge=pick(512), ffn=pick(1024),
                ffn_cols=FFN_DIM // 2)


def kernel(x, positions, attn_norm, w_in, mu_shift, decay_base, w_decay_up, iclr_base, w_iclr_up, w_gate_up, k_k, k_a, r_k, lnx_w, lnx_b, w_out_rwkv, q_norm, w_q_up, kv_norm, w_kv_up, w_out_mla, w_out, ffn_norm, w_ffn_up, conv_w, conv_b, w_ffn_down, final_norm):
    batch, seq, _ = x.shape
    depth = w_in.shape[0]
    assert seq % WKV_CHUNK == 0 and x.shape[2] == D_MODEL
    plan = _tile_plan(seq)
    tables = _rope_tables(positions)
    h = x.reshape(batch * seq, D_MODEL)
    for l in range(depth):
        prm = _layer_params(l, plan["ffn_cols"], attn_norm, w_in, mu_shift, decay_base, w_decay_up,
                            iclr_base, w_iclr_up, w_gate_up, k_k, k_a, r_k, lnx_w, lnx_b, w_out_rwkv,
                            q_norm, w_q_up, kv_norm, w_kv_up, w_out_mla, w_out, ffn_norm, w_ffn_up,
                            conv_w, conv_b, w_ffn_down)
        p_mla, gates, r, lw, k, v, a, b, gate, bonus = _inproj(h, prm, seq, plan["proj"])
        y = _wkv(r, lw, k, v, a, b, batch, seq)
        q_h, k_h, v_h = _mla_prep(p_mla, tables, prm, batch, seq, plan["attn"])
        o_mla = _mla_attn(q_h, k_h, v_h, batch, seq, plan["attn"])
        h = _merge(y, bonus, gate, o_mla, gates, h, prm, plan["merge"])
        last = final_norm if l == depth - 1 else None
        h = _conv_ffn(h, prm, last, seq, plan["ffn"], plan["ffn_cols"])
    return h.reshape(batch, seq, D_MODEL)
```

```python
import functools

import jax
import jax.numpy as jnp
import numpy as np
from jax import lax
from jax.experimental import pallas as pl
from jax.experimental.pallas import tpu as pltpu

F32 = jnp.float32
BF16 = jnp.bfloat16

D_MODEL = 1024
RWKV_HEADS = 8
RWKV_HEAD_DIM = 64
RWKV_DIM = RWKV_HEADS * RWKV_HEAD_DIM
DECAY_LORA = 64
ICLR_LORA = 64
GATE_LORA = 128
LNX_EPS = 64e-5
MLA_HEADS = 8
QK_NOPE_DIM = 64
QK_ROPE_DIM = 32
V_HEAD_DIM = 64
Q_LORA_RANK = 256
KV_LORA_RANK = 128
ROPE_THETA = 10000.0
NEG_INF = -1e30
FFN_DIM = 2816
NORM_EPS = 1e-6
SHIFT_COLS = 3 * RWKV_DIM + DECAY_LORA + ICLR_LORA + GATE_LORA
MLA_COLS = Q_LORA_RANK + KV_LORA_RANK + QK_ROPE_DIM

LANES = 128
SUBLANES = 8
MXU_WIDTH = 256
VMEM_LIMIT_BYTES = 56 * 1024 * 1024

MLA_PAD_COLS = 4 * LANES
ROPE_LANE0 = QK_NOPE_DIM
WKV_CHUNK = 64
PAIR = 2 * RWKV_HEAD_DIM
LOG2_E = float(np.log2(np.e))
V_ONES_ROWS = 16
V_EXT_ROWS = V_HEAD_DIM + V_ONES_ROWS
FFN_SUB_ROWS = 256


def _params(*semantics, flags=None):
    return pltpu.CompilerParams(dimension_semantics=semantics, vmem_limit_bytes=VMEM_LIMIT_BYTES,
                                flags=flags)


def _dot(a, b):
    return jnp.dot(a.astype(BF16), b.astype(BF16), preferred_element_type=F32)


def _dot_nt(a, b):
    return lax.dot_general(a.astype(BF16), b.astype(BF16), (((1,), (1,)), ((), ())),
                           preferred_element_type=F32)


def _split_bf16(x, terms):
    parts = []
    rest = x
    for _ in range(terms):
        p = rest.astype(BF16)
        parts.append(p)
        rest = rest - p.astype(F32)
    return parts


def _dot_split_lhs(a, b_bf16, terms):
    out = None
    for p in _split_bf16(a, terms):
        d = jnp.dot(p, b_bf16, preferred_element_type=F32)
        out = d if out is None else out + d
    return out


def _head_sums(x, ones):
    w = ones.shape[0]
    return jnp.concatenate([_dot_split_lhs(x[:, c:c + w], ones, 2) for c in range(0, x.shape[1], w)],
                           axis=1)


def _dot_x3(a, b_hi, b_lo):
    a_hi, a_lo = _split_bf16(a, 2)
    return (jnp.dot(a_hi, b_hi, preferred_element_type=F32)
            + jnp.dot(a_hi, b_lo, preferred_element_type=F32)
            + jnp.dot(a_lo, b_hi, preferred_element_type=F32))


def _rms(x, g):
    ms = jnp.mean(x * x, axis=-1, keepdims=True)
    return (x * lax.rsqrt(ms + NORM_EPS)) * g


def _const_spec(shape):
    return pl.BlockSpec(shape, lambda *_: (0,) * len(shape))


def _rope_tables_kernel(pos_ref, invf_ref, cos_ref, sin_ref, nsin_ref):
    ang = pos_ref[...] * invf_ref[...]
    s = jnp.sin(ang)
    cos_ref[...] = jnp.cos(ang)
    sin_ref[...] = s
    nsin_ref[...] = -s


def _rope_tables(positions):
    t = positions.size
    half = QK_ROPE_DIM // 2
    per_row = LANES // half
    inv_freq = jnp.power(ROPE_THETA, -jnp.arange(0, QK_ROPE_DIM, 2, dtype=F32) / QK_ROPE_DIM)
    invf = jnp.tile(inv_freq, per_row).reshape(1, LANES)
    posc = jnp.repeat(positions.reshape(t // per_row, per_row).astype(F32), half, axis=1)
    spec = pl.BlockSpec(posc.shape, lambda i: (0, 0))
    out = jax.ShapeDtypeStruct(posc.shape, F32)
    cos, sin, nsin = (x.reshape(t, half) for x in pl.pallas_call(
        _rope_tables_kernel,
        grid=(1,),
        in_specs=[spec, _const_spec((1, LANES))],
        out_specs=[spec, spec, spec],
        out_shape=[out, out, out],
        compiler_params=_params("arbitrary"),
        name="rope_tables",
    )(posc, invf))
    pad = lambda x, fill: jnp.pad(x, ((0, 0), (ROPE_LANE0, LANES - ROPE_LANE0 - x.shape[1])),
                                  constant_values=fill)
    zeros = jnp.zeros_like(sin)
    natural = (pad(jnp.concatenate([cos, cos], axis=1), 1.0),
               pad(jnp.concatenate([nsin, zeros], axis=1), 0.0),
               pad(jnp.concatenate([zeros, sin], axis=1), 0.0))
    return natural, (cos.T, sin.T)


def _rope(tile, cos, sina, sinb):
    half = QK_ROPE_DIM // 2
    return (tile * cos + pltpu.roll(tile, LANES - half, axis=1) * sina
            + pltpu.roll(tile, half, axis=1) * sinb)


def _inproj_kernel(blocks_per_seq, x_ref, g_ref, wrw_ref, wmla_ref, wgate_ref, mu_ref, wl_hi_ref,
                   wl_lo_ref, wg_hi_ref, wg_lo_ref, dbase_ref, ibase_ref, kk_ref, ka_ref, rk_ref, ones_ref,
                   pmla_ref, pg_ref, r_ref, lw_ref, k_ref, v_ref, a_ref, b_ref, gate_ref, bonus_ref,
                   tail_ref):
    i = pl.program_id(0)
    n = _rms(x_ref[...], g_ref[...]).astype(BF16)
    pmla_ref[...] = jnp.dot(n, wmla_ref[...], preferred_element_type=F32)
    pg_ref[...] = jnp.dot(n, wgate_ref[...], preferred_element_type=F32).astype(BF16)
    x = jnp.dot(n, wrw_ref[...], preferred_element_type=F32)
    tm = x.shape[0]
    prev = tail_ref[SUBLANES - 1:SUBLANES, :]
    prev = jnp.where(i % blocks_per_seq == 0, jnp.zeros_like(prev), prev)
    tail_ref[...] = x[tm - SUBLANES:tm]
    row = lax.broadcasted_iota(jnp.int32, (tm, 1), 0)
    shifted = jnp.where(row == 0, prev, pltpu.roll(x, 1, axis=0))
    xs = x + (shifted - x) * mu_ref[...]

    c0 = 3 * RWKV_DIM
    p_r = xs[:, 0:RWKV_DIM]
    p_k = xs[:, RWKV_DIM:2 * RWKV_DIM]
    p_v = xs[:, 2 * RWKV_DIM:c0]
    p_wa = xs[:, c0:c0 + DECAY_LORA + ICLR_LORA]
    p_g = xs[:, c0 + DECAY_LORA + ICLR_LORA:SHIFT_COLS]

    lane = lax.broadcasted_iota(jnp.int32, p_wa.shape, 1)
    lora_in = jnp.where(lane < DECAY_LORA, jnp.tanh(p_wa), p_wa)
    lora = _dot_x3(lora_in, wl_hi_ref[...], wl_lo_ref[...])
    log_w = -jax.nn.softplus(-(dbase_ref[...] + lora[:, 0:RWKV_DIM])) - 0.5
    iclr = jax.nn.sigmoid(ibase_ref[...] + lora[:, RWKV_DIM:2 * RWKV_DIM])
    gate = _dot_x3(jax.nn.sigmoid(p_g), wg_hi_ref[...], wg_lo_ref[...])

    ones = ones_ref[...]
    kk = p_k * kk_ref[...]
    kk = kk * lax.rsqrt(_head_sums(kk * kk, ones) + 1e-12)
    k = p_k * (1.0 + (iclr - 1.0) * ka_ref[...])

    r_ref[...] = p_r
    lw_ref[...] = -jnp.exp(log_w)
    k_ref[...] = k
    v_ref[...] = p_v
    a_ref[...] = -kk
    b_ref[...] = kk * iclr
    gate_ref[...] = gate
    bonus_ref[...] = _head_sums(p_r * k * rk_ref[...], ones) * p_v


def _inproj(h, prm, seq, tm):
    t = h.shape[0]
    row = lambda c: pl.BlockSpec((tm, c), lambda i: (i, 0))
    out = lambda c, dt: jax.ShapeDtypeStruct((t, c), dt)
    consts = [prm["attn_norm"], prm["w_rw"], prm["w_mla"], prm["w_g"], prm["mu"], prm["wl_hi"],
              prm["wl_lo"], prm["wg_hi"], prm["wg_lo"], prm["decay_base"], prm["iclr_base"], prm["k_k"],
              prm["k_a"], prm["r_k"], prm["head_ones"]]
    return pl.pallas_call(
        functools.partial(_inproj_kernel, seq // tm),
        grid=(t // tm,),
        in_specs=[row(D_MODEL)] + [_const_spec(c.shape) for c in consts],
        out_specs=[row(MLA_PAD_COLS), row(2 * D_MODEL)] + [row(RWKV_DIM)] * 8,
        out_shape=[out(MLA_PAD_COLS, F32), out(2 * D_MODEL, BF16)] + [out(RWKV_DIM, F32)] * 8,
        scratch_shapes=[pltpu.VMEM((SUBLANES, SHIFT_COLS), F32)],
        compiler_params=_params("arbitrary"),
        name="inproj",
    )(h, *consts)


def _wkv_pair(load, state_ref, idx, store_y, zbuf, ltri, eye):
    c = WKV_CHUNK
    lane = lax.broadcasted_iota(jnp.int32, (1, PAIR), 1)
    lo = lane < RWKV_HEAD_DIM
    r, lw, k, v, a, b = load()

    cum = _dot_split_lhs_left(ltri, lw)
    yield
    cum_last = cum[c - 1:c, :]
    g_inv = jnp.exp(-cum)
    a_t = a * jnp.exp(cum - lw)
    r_t = r * jnp.exp(cum)
    zero = jnp.zeros_like(a_t)
    lhs = jnp.concatenate([jnp.where(lo, a_t, zero), jnp.where(lo, zero, a_t),
                           jnp.where(lo, r_t, zero), jnp.where(lo, zero, r_t)], axis=0)
    b_t = b * g_inv
    k_t = k * g_inv
    a_all = _dot_nt(lhs, jnp.concatenate([b_t, k_t], axis=0))
    yield
    swapped = pltpu.roll(a_all, c, axis=1)
    head0_rows = (lax.broadcasted_iota(jnp.int32, (4 * c, 1), 0) // c) % 2 == 0
    a_b = jnp.where(head0_rows, a_all, swapped)
    a_k = jnp.where(head0_rows, swapped, a_all)

    rr = lax.broadcasted_iota(jnp.int32, (2 * c, 2 * c), 0)
    cc = lax.broadcasted_iota(jnp.int32, (2 * c, 2 * c), 1)
    same_head = (rr // c) == (cc // c)
    strict = same_head & ((cc % c) < (rr % c))
    incl = same_head & ((cc % c) <= (rr % c))
    l_bd = jnp.where(strict, a_b[0:2 * c], 0.0)
    aak_bd = jnp.where(strict, a_k[0:2 * c], 0.0)
    arb_bd = jnp.where(incl, a_b[2 * c:4 * c], 0.0)
    ark_bd = jnp.where(incl, a_k[2 * c:4 * c], 0.0)
    v_st = jnp.concatenate([jnp.where(lo, v, zero), jnp.where(lo, zero, v)], axis=0)
    av = _dot(jnp.concatenate([aak_bd, ark_bd], axis=0), v_st)
    q_st = av[0:2 * c]
    arkv_st = av[2 * c:4 * c]

    t_bd = eye + l_bd
    p = _dot(l_bd, l_bd)
    yield
    n = 2
    while 2 * n < c:
        pp_pt = _dot(p, jnp.concatenate([p, t_bd], axis=1))
        yield
        p = pp_pt[:, 0:2 * c]
        t_bd = t_bd + pp_pt[:, 2 * c:4 * c]
        n *= 2
    t_bd = t_bd + _dot(p, t_bd)
    yield

    tq_ta = _dot(t_bd, jnp.concatenate([lhs[0:2 * c], q_st], axis=1))
    yield
    rq_ra = _dot(arb_bd, tq_ta)
    yield
    ta_st = tq_ta[:, 0:PAIR]
    tq_st = tq_ta[:, PAIR:2 * PAIR]
    rh_st = lhs[2 * c:4 * c] + rq_ra[:, 0:PAIR]
    yh_st = rq_ra[:, PAIR:2 * PAIR] + arkv_st
    unstack = lambda x: x[0:c] + x[c:2 * c]
    ta = unstack(ta_st)
    rh = unstack(rh_st)
    tq = unstack(tq_st)
    yh = unstack(yh_st)

    state = state_ref[idx]
    zbuf[...] = _dot_nt(state, jnp.concatenate([ta, rh], axis=0))
    yield
    z = zbuf[...] + jnp.concatenate([tq, yh], axis=0).T
    v_t = jnp.concatenate([v, v], axis=0).T
    lane2 = lax.broadcasted_iota(jnp.int32, (1, 2 * c), 1)
    g_out = jnp.exp(cum_last - cum)
    upd = _dot(jnp.where(lane2 < c, z, v_t), jnp.concatenate([b * g_out, k * g_out], axis=0))
    yield
    state_ref[idx] = state * jnp.exp(cum_last) + jnp.where(same_head, upd, 0.0)
    store_y(z.T[c:2 * c, :])


def _dot_split_lhs_left(m_bf16, x):
    out = None
    for p in _split_bf16(x, 3):
        d = jnp.dot(m_bf16, p, preferred_element_type=F32)
        out = d if out is None else out + d
    return out


def _wkv_kernel(r_ref, lw_ref, k_ref, v_ref, a_ref, b_ref, ltri_ref, eye_ref, y_ref, state_ref, z_ref):
    @pl.when(pl.program_id(0) == 0)
    def _():
        state_ref[...] = jnp.zeros_like(state_ref)

    ltri = ltri_ref[...]
    eye = eye_ref[...]
    pairs = RWKV_DIM // PAIR
    chains = []
    for bi in range(r_ref.shape[0]):
        for p in range(pairs):
            sl = slice(p * PAIR, (p + 1) * PAIR)

            def load(bi=bi, sl=sl):
                return tuple(ref[bi, :, sl] for ref in (r_ref, lw_ref, k_ref, v_ref, a_ref, b_ref))

            def store_y(y, bi=bi, sl=sl):
                y_ref[bi, :, sl] = y

            chain = bi * pairs + p
            chains.append(_wkv_pair(load, state_ref, chain, store_y, z_ref.at[chain], ltri, eye))
    while chains:
        chains = [ch for ch in chains if next(ch, StopIteration) is not StopIteration]


def _wkv(r, lw, k, v, a, b, batch, seq):
    c = WKV_CHUNK
    shp = (batch, seq, RWKV_DIM)
    args = [x.reshape(shp) for x in (r, lw, k, v, a, b)]
    ltri = jnp.tril(jnp.ones((c, c), F32)).astype(BF16)
    eye = jnp.eye(2 * c, dtype=F32)
    blk = pl.BlockSpec((batch, c, RWKV_DIM), lambda ci: (0, ci, 0))
    y = pl.pallas_call(
        _wkv_kernel,
        grid=(seq // c,),
        in_specs=[blk] * 6 + [_const_spec((c, c)), _const_spec((2 * c, 2 * c))],
        out_specs=blk,
        out_shape=jax.ShapeDtypeStruct(shp, F32),
        scratch_shapes=[pltpu.VMEM((batch * (RWKV_DIM // PAIR), PAIR, PAIR), F32)] * 2,
        compiler_params=_params("arbitrary"),
        name="wkv",
    )(*args, ltri, eye)
    return y.reshape(batch * seq, RWKV_DIM)


def _mla_prep_kernel(p_ref, cos_ref, sina_ref, sinb_ref, cost_ref, sint_ref, qn_ref, kvn_ref, wqt_ref,
                     wk_ref, wvt_ref, qt_ref, k_ref, vt_ref):
    x = p_ref[...]
    blk = x.shape[0]
    scale = (QK_NOPE_DIM + QK_ROPE_DIM) ** -0.5 * LOG2_E
    qn = _rms(x[:, 0:Q_LORA_RANK], qn_ref[...]).astype(BF16)
    kvn = _rms(x[:, Q_LORA_RANK:Q_LORA_RANK + KV_LORA_RANK], kvn_ref[...]).astype(BF16)
    k_pe = _rope(x[:, Q_LORA_RANK + KV_LORA_RANK:MLA_PAD_COLS], cos_ref[...], sina_ref[...], sinb_ref[...])
    kf = jnp.dot(kvn, wk_ref[...], preferred_element_type=F32)
    for h in range(MLA_HEADS):
        k_ref[0, h, 0] = (kf[:, h * LANES:(h + 1) * LANES] + k_pe).astype(BF16)
    q_t = _dot_nt(wqt_ref[...], qn)
    v_t = _dot_nt(wvt_ref[...], kvn)
    cos_t, sin_t = cost_ref[...], sint_ref[...]
    half = QK_ROPE_DIM // 2
    r0 = ROPE_LANE0
    for h in range(MLA_HEADS):
        tile = q_t[h * LANES:(h + 1) * LANES]
        x1 = tile[r0:r0 + half]
        x2 = tile[r0 + half:r0 + QK_ROPE_DIM]
        roped = jnp.concatenate([tile[0:r0], x1 * cos_t - x2 * sin_t, x1 * sin_t + x2 * cos_t,
                                 tile[r0 + QK_ROPE_DIM:]], axis=0)
        qt_ref[0, h] = (roped * scale).astype(BF16)
    ones = jnp.ones((V_ONES_ROWS, blk), F32)
    for h in range(0, MLA_HEADS, 2):
        rows = lambda g: v_t[g * V_HEAD_DIM:(g + 1) * V_HEAD_DIM]
        vt_ref[0, h // 2, 0] = jnp.concatenate([rows(h), ones, rows(h + 1), ones], axis=0).astype(BF16)


def _mla_prep(p_mla, tables, prm, batch, seq, blk):
    t = p_mla.shape[0]
    n = seq // blk
    natural, transposed = tables
    tab = pl.BlockSpec((blk, LANES), lambda i: (i, 0))
    tab_t = pl.BlockSpec((QK_ROPE_DIM // 2, blk), lambda i: (0, i))
    consts = [prm["q_norm"], prm["kv_norm"], prm["w_qt"], prm["w_k"], prm["w_vt"]]
    return pl.pallas_call(
        _mla_prep_kernel,
        grid=(t // blk,),
        in_specs=[pl.BlockSpec((blk, MLA_PAD_COLS), lambda i: (i, 0)), tab, tab, tab, tab_t, tab_t]
        + [_const_spec(c.shape) for c in consts],
        out_specs=[pl.BlockSpec((1, MLA_HEADS, LANES, blk), lambda i: (i // n, 0, 0, i % n)),
                   pl.BlockSpec((1, MLA_HEADS, 1, blk, LANES), lambda i: (i // n, 0, i % n, 0, 0)),
                   pl.BlockSpec((1, MLA_HEADS // 2, 1, 2 * V_EXT_ROWS, blk), lambda i: (i // n, 0, i % n, 0, 0))],
        out_shape=[jax.ShapeDtypeStruct((batch, MLA_HEADS, LANES, seq), BF16),
                   jax.ShapeDtypeStruct((batch, MLA_HEADS, n, blk, LANES), BF16),
                   jax.ShapeDtypeStruct((batch, MLA_HEADS // 2, n, 2 * V_EXT_ROWS, blk), BF16)],
        compiler_params=_params("parallel"),
        name="mla_prep",
    )(p_mla, *natural, *transposed, *consts)


def _attn_kernel(qt_ref, k_ref, vt_ref, o_ref, m_ref, acc_ref, s0_ref, s1_ref):
    n_blk, blk = k_ref.shape[2], k_ref.shape[3]
    units = [(i, j) for i in range(n_blk) for j in range(i + 1)]
    bufs = (s0_ref, s1_ref)

    def scores(unit, s_ref):
        i, j = unit
        for h in range(2):
            s_ref[h] = jnp.dot(k_ref[0, h, j], qt_ref[0, h, :, i * blk:(i + 1) * blk],
                               preferred_element_type=F32)

    def update(unit, s_ref):
        i, j = unit
        if j == 0:
            m_ref[...] = jnp.full_like(m_ref, -jnp.inf)
            acc_ref[...] = jnp.zeros_like(acc_ref)
        for h in range(2):
            s = s_ref[h]
            if j == i:
                k_pos = lax.broadcasted_iota(jnp.int32, (blk, blk), 0)
                q_pos = lax.broadcasted_iota(jnp.int32, (blk, blk), 1)
                s = jnp.where(k_pos <= q_pos, s, NEG_INF)
            m_old = m_ref[h]
            m_new = jnp.maximum(m_old, jnp.max(s, axis=0, keepdims=True))
            vt = vt_ref[0, 0, j, h * V_EXT_ROWS:(h + 1) * V_EXT_ROWS, :]
            pv = jnp.dot(vt, jnp.exp2(s - m_new).astype(BF16), preferred_element_type=F32)
            acc_ref[h] = jnp.exp2(m_old - m_new) * acc_ref[h] + pv
            m_ref[h] = m_new
        if j == i:
            o_t = jnp.concatenate([acc_ref[h, 0:V_HEAD_DIM] / acc_ref[h, V_HEAD_DIM:V_HEAD_DIM + 1]
                                   for h in range(2)], axis=0)
            o_ref[0, i * blk:(i + 1) * blk, :] = o_t.T.astype(BF16)

    scores(units[0], bufs[0])
    for n, unit in enumerate(units):
        if n + 1 < len(units):
            scores(units[n + 1], bufs[(n + 1) % 2])
        update(unit, bufs[n % 2])


def _mla_attn(qt, k, vt, batch, seq, blk):
    n = seq // blk
    hv = MLA_HEADS * V_HEAD_DIM
    o = pl.pallas_call(
        _attn_kernel,
        grid=(batch, MLA_HEADS // 2),
        in_specs=[pl.BlockSpec((1, 2, LANES, seq), lambda b, p: (b, p, 0, 0)),
                  pl.BlockSpec((1, 2, n, blk, LANES), lambda b, p: (b, p, 0, 0, 0)),
                  pl.BlockSpec((1, 1, n, 2 * V_EXT_ROWS, blk), lambda b, p: (b, p, 0, 0, 0))],
        out_specs=pl.BlockSpec((1, seq, LANES), lambda b, p: (b, 0, p)),
        out_shape=jax.ShapeDtypeStruct((batch, seq, hv), BF16),
        scratch_shapes=[pltpu.VMEM((2, 1, blk), F32), pltpu.VMEM((2, V_EXT_ROWS, blk), F32),
                        pltpu.VMEM((2, blk, blk), F32), pltpu.VMEM((2, blk, blk), F32)],
        compiler_params=_params("parallel", "parallel"),
        name="mla_attn",
    )(qt, k, vt)
    return o.reshape(batch * seq, hv)


def _merge_kernel(y_ref, bonus_ref, gate_ref, o_ref, g_ref, h_ref, lnw_ref, lnb_ref, ones_ref,
                  woa_ref, wob_ref, wout_ref, out_ref):
    ones = ones_ref[...]
    inv_n = 1.0 / RWKV_HEAD_DIM
    y = y_ref[...]
    mu = _head_sums(y, ones) * inv_n
    d = y - mu
    var = _head_sums(d * d, ones) * inv_n
    yn = (d * lax.rsqrt(var + LNX_EPS)) * lnw_ref[...] + lnb_ref[...]
    z = (yn + bonus_ref[...]) * gate_ref[...]
    y_a = _dot(z, woa_ref[...])
    y_b = jnp.dot(o_ref[...], wob_ref[...], preferred_element_type=F32)
    g = g_ref[...].astype(F32)
    merged = jax.nn.sigmoid(g[:, 0:D_MODEL]) * y_a + jax.nn.sigmoid(g[:, D_MODEL:2 * D_MODEL]) * y_b
    out_ref[...] = h_ref[...] + _dot(merged, wout_ref[...])


def _merge(y, bonus, gate, o_mla, gates, h, prm, tm):
    t = h.shape[0]
    row = lambda c: pl.BlockSpec((tm, c), lambda i: (i, 0))
    consts = [prm["lnx_w"], prm["lnx_b"], prm["head_ones"], prm["w_oa"], prm["w_ob"], prm["w_out"]]
    return pl.pallas_call(
        _merge_kernel,
        grid=(t // tm,),
        in_specs=[row(RWKV_DIM)] * 4 + [row(2 * D_MODEL), row(D_MODEL)]
        + [_const_spec(c.shape) for c in consts],
        out_specs=row(D_MODEL),
        out_shape=jax.ShapeDtypeStruct((t, D_MODEL), F32),
        compiler_params=_params("parallel"),
        name="merge",
    )(y, bonus, gate, o_mla, gates, h, *consts)


def _ffn_kernel(blocks_per_seq, final, h_ref, g_ref, wg_ref, wv_ref, cw_ref, cb_ref, wd_ref, fg_ref,
                out_ref, n_ref, acc_ref, carry_ref):
    i = pl.program_id(0)
    f = pl.program_id(1)
    tm = h_ref.shape[0]
    rows = min(FFN_SUB_ROWS, tm)

    @pl.when(f == 0)
    def _():
        n_ref[...] = _rms(h_ref[...], g_ref[...]).astype(BF16)
        acc_ref[...] = jnp.zeros_like(acc_ref)

    def up(sb):
        n = n_ref[sb * rows:(sb + 1) * rows, :]
        return (jnp.dot(n, wg_ref[...], preferred_element_type=F32),
                jnp.dot(n, wv_ref[...], preferred_element_type=F32))

    tail = carry_ref[f]
    tail = jnp.where(i % blocks_per_seq == 0, jnp.zeros_like(tail), tail)
    cw = cw_ref[0]
    row = lax.broadcasted_iota(jnp.int32, (rows, 1), 0)
    sqrt_half = np.sqrt(0.5).astype(np.float32)
    nxt = up(0)
    for sb in range(tm // rows):
        u_gate, u_val = nxt
        if sb + 1 < tm // rows:
            nxt = up(sb + 1)
        back1 = jnp.where(row == 0, tail[SUBLANES - 1:SUBLANES], pltpu.roll(u_gate, 1, axis=0))
        back2 = jnp.where(row == 0, tail[SUBLANES - 2:SUBLANES - 1],
                          jnp.where(row == 1, tail[SUBLANES - 1:SUBLANES], pltpu.roll(u_gate, 2, axis=0)))
        c = cb_ref[0] + cw[0:1] * back2
        c = c + cw[1:2] * back1
        c = c + cw[2:3] * u_gate
        tail = u_gate[rows - SUBLANES:rows]
        act = 0.5 * c * (1.0 + lax.erf(c * sqrt_half))
        down = jnp.dot((act * u_val).astype(BF16), wd_ref[...], preferred_element_type=F32)
        acc_ref[sb * rows:(sb + 1) * rows, :] += down
    carry_ref[f] = tail

    @pl.when(f == pl.num_programs(1) - 1)
    def _():
        out = h_ref[...] + acc_ref[...]
        out_ref[...] = _rms(out, fg_ref[...]) if final else out


def _conv_ffn(h, prm, final_gain, seq, tm, fc):
    t = h.shape[0]
    nf = FFN_DIM // fc
    row = pl.BlockSpec((tm, D_MODEL), lambda i, f: (i, 0))
    final = final_gain is not None
    fg = final_gain.reshape(1, D_MODEL) if final else prm["ffn_norm"]
    return pl.pallas_call(
        functools.partial(_ffn_kernel, seq // tm, final),
        grid=(t // tm, nf),
        in_specs=[row, _const_spec((1, D_MODEL)),
                  pl.BlockSpec((D_MODEL, fc), lambda i, f: (0, f)),
                  pl.BlockSpec((D_MODEL, fc), lambda i, f: (0, nf + f)),
                  pl.BlockSpec((1, SUBLANES, fc), lambda i, f: (f, 0, 0)),
                  pl.BlockSpec((1, 1, fc), lambda i, f: (f, 0, 0)),
                  pl.BlockSpec((fc, D_MODEL), lambda i, f: (f, 0)),
                  _const_spec((1, D_MODEL))],
        out_specs=row,
        out_shape=jax.ShapeDtypeStruct((t, D_MODEL), F32),
        scratch_shapes=[pltpu.VMEM((tm, D_MODEL), BF16), pltpu.VMEM((tm, D_MODEL), F32),
                        pltpu.VMEM((nf, SUBLANES, fc), F32)],
        compiler_params=_params("arbitrary", "arbitrary"),
        name="conv_ffn",
    )(h, prm["ffn_norm"], prm["w_up"], prm["w_up"], prm["conv_w"], prm["conv_b"], prm["w_down"], fg)


def _hi_lo(w):
    hi = w.astype(BF16)
    return hi, (w - hi.astype(F32)).astype(BF16)


def _layer_params(l, fc, attn_norm, w_in, mu_shift, decay_base, w_decay_up, iclr_base, w_iclr_up,
                  w_gate_up, k_k, k_a, r_k, lnx_w, lnx_b, w_out_rwkv, q_norm, w_q_up, kv_norm,
                  w_kv_up, w_out_mla, w_out, ffn_norm, w_ffn_up, conv_w, conv_b, w_ffn_down):
    row = lambda x: x.reshape(1, -1)
    w = w_in[l]
    w_mla_src = w[:, SHIFT_COLS:SHIFT_COLS + MLA_COLS]
    lat = Q_LORA_RANK + KV_LORA_RANK
    w_mla = jnp.zeros((D_MODEL, MLA_PAD_COLS), F32)
    w_mla = w_mla.at[:, 0:lat].set(w_mla_src[:, 0:lat])
    w_mla = w_mla.at[:, lat + ROPE_LANE0:lat + ROPE_LANE0 + QK_ROPE_DIM].set(w_mla_src[:, lat:])
    w_lora = jnp.zeros((DECAY_LORA + ICLR_LORA, 2 * RWKV_DIM), F32)
    w_lora = w_lora.at[0:DECAY_LORA, 0:RWKV_DIM].set(w_decay_up[l])
    w_lora = w_lora.at[DECAY_LORA:, RWKV_DIM:].set(w_iclr_up[l])
    wl_hi, wl_lo = _hi_lo(w_lora)
    wg_hi, wg_lo = _hi_lo(w_gate_up[l])
    head = jnp.arange(MXU_WIDTH) // RWKV_HEAD_DIM
    head_ones = (head[:, None] == head[None, :]).astype(BF16)
    qk = QK_NOPE_DIM + QK_ROPE_DIM
    w_q = jnp.pad(w_q_up[l].reshape(Q_LORA_RANK, MLA_HEADS, qk), ((0, 0), (0, 0), (0, LANES - qk)))
    w_kv = w_kv_up[l].reshape(KV_LORA_RANK, MLA_HEADS, QK_NOPE_DIM + V_HEAD_DIM)
    w_k = jnp.pad(w_kv[:, :, :QK_NOPE_DIM], ((0, 0), (0, 0), (0, LANES - QK_NOPE_DIM)))
    nf = FFN_DIM // fc
    return dict(
        attn_norm=row(attn_norm[l]),
        w_rw=w[:, :SHIFT_COLS].astype(BF16),
        w_mla=w_mla.astype(BF16),
        w_g=w[:, SHIFT_COLS + MLA_COLS:].astype(BF16),
        mu=row(mu_shift[l]), wl_hi=wl_hi, wl_lo=wl_lo, wg_hi=wg_hi, wg_lo=wg_lo,
        decay_base=row(decay_base[l]), iclr_base=row(iclr_base[l]), k_k=row(k_k[l]), k_a=row(k_a[l]),
        r_k=row(r_k[l]), head_ones=head_ones, lnx_w=row(lnx_w[l]), lnx_b=row(lnx_b[l]),
        q_norm=row(q_norm[l]), kv_norm=row(kv_norm[l]),
        w_qt=w_q.reshape(Q_LORA_RANK, MLA_HEADS * LANES).T.astype(BF16),
        w_k=w_k.reshape(KV_LORA_RANK, MLA_HEADS * LANES).astype(BF16),
        w_vt=w_kv[:, :, QK_NOPE_DIM:].reshape(KV_LORA_RANK, MLA_HEADS * V_HEAD_DIM).T.astype(BF16),
        w_oa=w_out_rwkv[l].astype(BF16), w_ob=w_out_mla[l].astype(BF16), w_out=w_out[l].astype(BF16),
        ffn_norm=row(ffn_norm[l]), w_up=w_ffn_up[l].astype(BF16), w_down=w_ffn_down[l].astype(BF16),
        conv_w=jnp.pad(conv_w[l].reshape(3, nf, fc).transpose(1, 0, 2), ((0, 0), (0, SUBLANES - 3), (0, 0))),
        conv_b=conv_b[l].reshape(nf, 1, fc),
    )


def _tile_plan(seq):
    pick = lambda want: min(want, seq)
    return dict(proj=pick(512), attn=pick(512), merge=pick(512), ffn=pick(1024),
                ffn_cols=FFN_DIM // 2)


def kernel(x, positions, attn_norm, w_in, mu_shift, decay_base, w_decay_up, iclr_base, w_iclr_up, w_gate_up, k_k, k_a, r_k, lnx_w, lnx_b, w_out_rwkv, q_norm, w_q_up, kv_norm, w_kv_up, w_out_mla, w_out, ffn_norm, w_ffn_up, conv_w, conv_b, w_ffn_down, final_norm):
    batch, seq, _ = x.shape
    depth = w_in.shape[0]
    assert seq % WKV_CHUNK == 0 and x.shape[2] == D_MODEL
    plan = _tile_plan(seq)
    tables = _rope_tables(positions)
    h = x.reshape(batch * seq, D_MODEL)
    for l in range(depth):
        prm = _layer_params(l, plan["ffn_cols"], attn_norm, w_in, mu_shift, decay_base, w_decay_up,
                            iclr_base, w_iclr_up, w_gate_up, k_k, k_a, r_k, lnx_w, lnx_b, w_out_rwkv,
                            q_norm, w_q_up, kv_norm, w_kv_up, w_out_mla, w_out, ffn_norm, w_ffn_up,
                            conv_w, conv_b, w_ffn_down)
        p_mla, gates, r, lw, k, v, a, b, gate, bonus = _inproj(h, prm, seq, plan["proj"])
        y = _wkv(r, lw, k, v, a, b, batch, seq)
        q_h, k_h, v_h = _mla_prep(p_mla, tables, prm, batch, seq, plan["attn"])
        o_mla = _mla_attn(q_h, k_h, v_h, batch, seq, plan["attn"])
        h = _merge(y, bonus, gate, o_mla, gates, h, prm, plan["merge"])
        last = final_norm if l == depth - 1 else None
        h = _conv_ffn(h, prm, last, seq, plan["ffn"], plan["ffn_cols"])
    return h.reshape(batch, seq, D_MODEL)
```

```python
import functools

import jax
import jax.numpy as jnp
import numpy as np
from jax import lax
from jax.experimental import pallas as pl
from jax.experimental.pallas import tpu as pltpu

F32 = jnp.float32
BF16 = jnp.bfloat16

D_MODEL = 1024
RWKV_HEADS = 8
RWKV_HEAD_DIM = 64
RWKV_DIM = RWKV_HEADS * RWKV_HEAD_DIM
DECAY_LORA = 64
ICLR_LORA = 64
GATE_LORA = 128
LNX_EPS = 64e-5
MLA_HEADS = 8
QK_NOPE_DIM = 64
QK_ROPE_DIM = 32
V_HEAD_DIM = 64
Q_LORA_RANK = 256
KV_LORA_RANK = 128
ROPE_THETA = 10000.0
NEG_INF = -1e30
FFN_DIM = 2816
NORM_EPS = 1e-6
SHIFT_COLS = 3 * RWKV_DIM + DECAY_LORA + ICLR_LORA + GATE_LORA
MLA_COLS = Q_LORA_RANK + KV_LORA_RANK + QK_ROPE_DIM

LANES = 128
SUBLANES = 8
MXU_WIDTH = 256
VMEM_LIMIT_BYTES = 56 * 1024 * 1024

MLA_PAD_COLS = 4 * LANES
ROPE_LANE0 = QK_NOPE_DIM
WKV_CHUNK = 64
PAIR = 2 * RWKV_HEAD_DIM
LOG2_E = float(np.log2(np.e))
V_ONES_ROWS = 16
V_EXT_ROWS = V_HEAD_DIM + V_ONES_ROWS
ATTN_Q_PIECE = 256
FFN_SUB_ROWS = 256


def _params(*semantics, flags=None):
    return pltpu.CompilerParams(dimension_semantics=semantics, vmem_limit_bytes=VMEM_LIMIT_BYTES,
                                flags=flags)


def _dot(a, b):
    return jnp.dot(a.astype(BF16), b.astype(BF16), preferred_element_type=F32)


def _dot_nt(a, b):
    return lax.dot_general(a.astype(BF16), b.astype(BF16), (((1,), (1,)), ((), ())),
                           preferred_element_type=F32)


def _split_bf16(x, terms):
    parts = []
    rest = x
    for _ in range(terms):
        p = rest.astype(BF16)
        parts.append(p)
        rest = rest - p.astype(F32)
    return parts


def _dot_split_lhs(a, b_bf16, terms):
    out = None
    for p in _split_bf16(a, terms):
        d = jnp.dot(p, b_bf16, preferred_element_type=F32)
        out = d if out is None else out + d
    return out


def _head_sums(x, ones):
    w = ones.shape[0]
    return jnp.concatenate([_dot_split_lhs(x[:, c:c + w], ones, 2) for c in range(0, x.shape[1], w)],
                           axis=1)


def _rms(x, g):
    ms = jnp.mean(x * x, axis=-1, keepdims=True)
    return (x * lax.rsqrt(ms + NORM_EPS)) * g


def _const_spec(shape):
    return pl.BlockSpec(shape, lambda *_: (0,) * len(shape))


def _rope_tables_kernel(pos_ref, invf_ref, cos_ref, sin_ref, nsin_ref):
    ang = pos_ref[...] * invf_ref[...]
    s = jnp.sin(ang)
    cos_ref[...] = jnp.cos(ang)
    sin_ref[...] = s
    nsin_ref[...] = -s


def _rope_tables(positions):
    t = positions.size
    half = QK_ROPE_DIM // 2
    per_row = LANES // half
    inv_freq = jnp.power(ROPE_THETA, -jnp.arange(0, QK_ROPE_DIM, 2, dtype=F32) / QK_ROPE_DIM)
    invf = jnp.tile(inv_freq, per_row).reshape(1, LANES)
    posc = jnp.repeat(positions.reshape(t // per_row, per_row).astype(F32), half, axis=1)
    spec = pl.BlockSpec(posc.shape, lambda i: (0, 0))
    out = jax.ShapeDtypeStruct(posc.shape, F32)
    cos, sin, nsin = (x.reshape(t, half) for x in pl.pallas_call(
        _rope_tables_kernel,
        grid=(1,),
        in_specs=[spec, _const_spec((1, LANES))],
        out_specs=[spec, spec, spec],
        out_shape=[out, out, out],
        compiler_params=_params("arbitrary"),
        name="rope_tables",
    )(posc, invf))
    pad = lambda x, fill: jnp.pad(x, ((0, 0), (ROPE_LANE0, LANES - ROPE_LANE0 - x.shape[1])),
                                  constant_values=fill)
    zeros = jnp.zeros_like(sin)
    natural = (pad(jnp.concatenate([cos, cos], axis=1), 1.0),
               pad(jnp.concatenate([nsin, zeros], axis=1), 0.0),
               pad(jnp.concatenate([zeros, sin], axis=1), 0.0))
    return natural, (cos.T, sin.T)


def _rope(tile, cos, sina, sinb):
    half = QK_ROPE_DIM // 2
    return (tile * cos + pltpu.roll(tile, LANES - half, axis=1) * sina
            + pltpu.roll(tile, half, axis=1) * sinb)


def _inproj_kernel(blocks_per_seq, x_ref, g_ref, wrw_ref, wmla_ref, wgate_ref, mu_ref, wl_ref,
                   wg_ref, dbase_ref, ibase_ref, kk_ref, ka_ref, rk_ref, ones_ref,
                   pmla_ref, pg_ref, r_ref, lw_ref, k_ref, v_ref, a_ref, b_ref, gate_ref, bonus_ref,
                   tail_ref):
    i = pl.program_id(0)
    n = _rms(x_ref[...], g_ref[...]).astype(BF16)
    pmla_ref[...] = jnp.dot(n, wmla_ref[...], preferred_element_type=F32)
    pg_ref[...] = jnp.dot(n, wgate_ref[...], preferred_element_type=F32).astype(BF16)
    x = jnp.dot(n, wrw_ref[...], preferred_element_type=F32)
    tm = x.shape[0]
    prev = tail_ref[SUBLANES - 1:SUBLANES, :]
    prev = jnp.where(i % blocks_per_seq == 0, jnp.zeros_like(prev), prev)
    tail_ref[...] = x[tm - SUBLANES:tm]
    row = lax.broadcasted_iota(jnp.int32, (tm, 1), 0)
    shifted = jnp.where(row == 0, prev, pltpu.roll(x, 1, axis=0))
    xs = x + (shifted - x) * mu_ref[...]

    c0 = 3 * RWKV_DIM
    p_r = xs[:, 0:RWKV_DIM]
    p_k = xs[:, RWKV_DIM:2 * RWKV_DIM]
    p_v = xs[:, 2 * RWKV_DIM:c0]
    p_wa = xs[:, c0:c0 + DECAY_LORA + ICLR_LORA]
    p_g = xs[:, c0 + DECAY_LORA + ICLR_LORA:SHIFT_COLS]

    lane = lax.broadcasted_iota(jnp.int32, p_wa.shape, 1)
    lora_in = jnp.where(lane < DECAY_LORA, jnp.tanh(p_wa), p_wa)
    lora = _dot(lora_in, wl_ref[...])
    log_w = -jax.nn.softplus(-(dbase_ref[...] + lora[:, 0:RWKV_DIM])) - 0.5
    iclr = jax.nn.sigmoid(ibase_ref[...] + lora[:, RWKV_DIM:2 * RWKV_DIM])
    gate = _dot(jax.nn.sigmoid(p_g), wg_ref[...])

    ones = ones_ref[...]
    kk = p_k * kk_ref[...]
    kk = kk * lax.rsqrt(_head_sums(kk * kk, ones) + 1e-12)
    k = p_k * (1.0 + (iclr - 1.0) * ka_ref[...])

    r_ref[...] = p_r
    lw_ref[...] = -jnp.exp(log_w)
    k_ref[...] = k
    v_ref[...] = p_v
    a_ref[...] = -kk
    b_ref[...] = kk * iclr
    gate_ref[...] = gate
    bonus_ref[...] = _head_sums(p_r * k * rk_ref[...], ones) * p_v


def _inproj(h, prm, seq, tm):
    t = h.shape[0]
    row = lambda c: pl.BlockSpec((tm, c), lambda i: (i, 0))
    out = lambda c, dt: jax.ShapeDtypeStruct((t, c), dt)
    consts = [prm["attn_norm"], prm["w_rw"], prm["w_mla"], prm["w_g"], prm["mu"], prm["w_lora"],
              prm["w_gate_up"], prm["decay_base"], prm["iclr_base"], prm["k_k"], prm["k_a"], prm["r_k"],
              prm["head_ones"]]
    return pl.pallas_call(
        functools.partial(_inproj_kernel, seq // tm),
        grid=(t // tm,),
        in_specs=[row(D_MODEL)] + [_const_spec(c.shape) for c in consts],
        out_specs=[row(MLA_PAD_COLS), row(2 * D_MODEL)] + [row(RWKV_DIM)] * 8,
        out_shape=[out(MLA_PAD_COLS, F32), out(2 * D_MODEL, BF16)] + [out(RWKV_DIM, F32)] * 8,
        scratch_shapes=[pltpu.VMEM((SUBLANES, SHIFT_COLS), F32)],
        compiler_params=_params("arbitrary"),
        name="inproj",
    )(h, *consts)


def _wkv_pair(load, state_ref, idx, store_y, zbuf, ltri, eye):
    c = WKV_CHUNK
    lane = lax.broadcasted_iota(jnp.int32, (1, PAIR), 1)
    lo = lane < RWKV_HEAD_DIM
    r, lw, k, v, a, b = load()

    cum = _dot_split_lhs_left(ltri, lw)
    yield
    cum_last = cum[c - 1:c, :]
    g_inv = jnp.exp(-cum)
    a_t = a * jnp.exp(cum - lw)
    r_t = r * jnp.exp(cum)
    zero = jnp.zeros_like(a_t)
    lhs = jnp.concatenate([jnp.where(lo, a_t, zero), jnp.where(lo, zero, a_t),
                           jnp.where(lo, r_t, zero), jnp.where(lo, zero, r_t)], axis=0)
    b_t = b * g_inv
    k_t = k * g_inv
    a_all = _dot_nt(lhs, jnp.concatenate([b_t, k_t], axis=0))
    yield
    swapped = pltpu.roll(a_all, c, axis=1)
    head0_rows = (lax.broadcasted_iota(jnp.int32, (4 * c, 1), 0) // c) % 2 == 0
    a_b = jnp.where(head0_rows, a_all, swapped)
    a_k = jnp.where(head0_rows, swapped, a_all)

    rr = lax.broadcasted_iota(jnp.int32, (2 * c, 2 * c), 0)
    cc = lax.broadcasted_iota(jnp.int32, (2 * c, 2 * c), 1)
    same_head = (rr // c) == (cc // c)
    strict = same_head & ((cc % c) < (rr % c))
    incl = same_head & ((cc % c) <= (rr % c))
    l_bd = jnp.where(strict, a_b[0:2 * c], 0.0)
    aak_bd = jnp.where(strict, a_k[0:2 * c], 0.0)
    arb_bd = jnp.where(incl, a_b[2 * c:4 * c], 0.0)
    ark_bd = jnp.where(incl, a_k[2 * c:4 * c], 0.0)
    v_st = jnp.concatenate([jnp.where(lo, v, zero), jnp.where(lo, zero, v)], axis=0)
    av = _dot(jnp.concatenate([aak_bd, ark_bd], axis=0), v_st)
    q_st = av[0:2 * c]
    arkv_st = av[2 * c:4 * c]

    t_bd = eye + l_bd
    p = _dot(l_bd, l_bd)
    yield
    n = 2
    while 2 * n < c:
        pp_pt = _dot(p, jnp.concatenate([p, t_bd], axis=1))
        yield
        p = pp_pt[:, 0:2 * c]
        t_bd = t_bd + pp_pt[:, 2 * c:4 * c]
        n *= 2
    t_bd = t_bd + _dot(p, t_bd)
    yield

    tq_ta = _dot(t_bd, jnp.concatenate([lhs[0:2 * c], q_st], axis=1))
    yield
    rq_ra = _dot(arb_bd, tq_ta)
    yield
    ta_st = tq_ta[:, 0:PAIR]
    tq_st = tq_ta[:, PAIR:2 * PAIR]
    rh_st = lhs[2 * c:4 * c] + rq_ra[:, 0:PAIR]
    yh_st = rq_ra[:, PAIR:2 * PAIR] + arkv_st
    unstack = lambda x: x[0:c] + x[c:2 * c]
    ta = unstack(ta_st)
    rh = unstack(rh_st)
    tq = unstack(tq_st)
    yh = unstack(yh_st)

    state = state_ref[idx]
    zbuf[...] = _dot_nt(state, jnp.concatenate([ta, rh], axis=0))
    yield
    z = zbuf[...] + jnp.concatenate([tq, yh], axis=0).T
    v_t = jnp.concatenate([v, v], axis=0).T
    lane2 = lax.broadcasted_iota(jnp.int32, (1, 2 * c), 1)
    g_out = jnp.exp(cum_last - cum)
    upd = _dot(jnp.where(lane2 < c, z, v_t), jnp.concatenate([b * g_out, k * g_out], axis=0))
    yield
    state_ref[idx] = state * jnp.exp(cum_last) + jnp.where(same_head, upd, 0.0)
    store_y(z.T[c:2 * c, :])


def _dot_split_lhs_left(m_bf16, x):
    out = None
    for p in _split_bf16(x, 3):
        d = jnp.dot(m_bf16, p, preferred_element_type=F32)
        out = d if out is None else out + d
    return out


def _wkv_kernel(r_ref, lw_ref, k_ref, v_ref, a_ref, b_ref, ltri_ref, eye_ref, y_ref, state_ref, z_ref):
    @pl.when(pl.program_id(0) == 0)
    def _():
        state_ref[...] = jnp.zeros_like(state_ref)

    ltri = ltri_ref[...]
    eye = eye_ref[...]
    pairs = RWKV_DIM // PAIR
    chains = []
    for bi in range(r_ref.shape[0]):
        for p in range(pairs):
            sl = slice(p * PAIR, (p + 1) * PAIR)

            def load(bi=bi, sl=sl):
                return tuple(ref[bi, :, sl] for ref in (r_ref, lw_ref, k_ref, v_ref, a_ref, b_ref))

            def store_y(y, bi=bi, sl=sl):
                y_ref[bi, :, sl] = y

            chain = bi * pairs + p
            chains.append(_wkv_pair(load, state_ref, chain, store_y, z_ref.at[chain], ltri, eye))
    while chains:
        chains = [ch for ch in chains if next(ch, StopIteration) is not StopIteration]


def _wkv(r, lw, k, v, a, b, batch, seq):
    c = WKV_CHUNK
    shp = (batch, seq, RWKV_DIM)
    args = [x.reshape(shp) for x in (r, lw, k, v, a, b)]
    ltri = jnp.tril(jnp.ones((c, c), F32)).astype(BF16)
    eye = jnp.eye(2 * c, dtype=F32)
    blk = pl.BlockSpec((batch, c, RWKV_DIM), lambda ci: (0, ci, 0))
    y = pl.pallas_call(
        _wkv_kernel,
        grid=(seq // c,),
        in_specs=[blk] * 6 + [_const_spec((c, c)), _const_spec((2 * c, 2 * c))],
        out_specs=blk,
        out_shape=jax.ShapeDtypeStruct(shp, F32),
        scratch_shapes=[pltpu.VMEM((batch * (RWKV_DIM // PAIR), PAIR, PAIR), F32)] * 2,
        compiler_params=_params("arbitrary"),
        name="wkv",
    )(*args, ltri, eye)
    return y.reshape(batch * seq, RWKV_DIM)


def _mla_prep_kernel(p_ref, cos_ref, sina_ref, sinb_ref, cost_ref, sint_ref, qn_ref, kvn_ref, wqt_ref,
                     wk_ref, wvt_ref, qt_ref, k_ref, vt_ref):
    x = p_ref[...]
    blk = x.shape[0]
    scale = (QK_NOPE_DIM + QK_ROPE_DIM) ** -0.5 * LOG2_E
    qn = _rms(x[:, 0:Q_LORA_RANK], qn_ref[...]).astype(BF16)
    kvn = _rms(x[:, Q_LORA_RANK:Q_LORA_RANK + KV_LORA_RANK], kvn_ref[...]).astype(BF16)
    k_pe = _rope(x[:, Q_LORA_RANK + KV_LORA_RANK:MLA_PAD_COLS], cos_ref[...], sina_ref[...], sinb_ref[...])
    kf = jnp.dot(kvn, wk_ref[...], preferred_element_type=F32)
    for h in range(MLA_HEADS):
        k_ref[0, h, 0] = (kf[:, h * LANES:(h + 1) * LANES] + k_pe).astype(BF16)
    q_t = _dot_nt(wqt_ref[...], qn)
    v_t = _dot_nt(wvt_ref[...], kvn)
    cos_t, sin_t = cost_ref[...], sint_ref[...]
    half = QK_ROPE_DIM // 2
    r0 = ROPE_LANE0
    for h in range(MLA_HEADS):
        tile = q_t[h * LANES:(h + 1) * LANES]
        x1 = tile[r0:r0 + half]
        x2 = tile[r0 + half:r0 + QK_ROPE_DIM]
        roped = jnp.concatenate([tile[0:r0], x1 * cos_t - x2 * sin_t, x1 * sin_t + x2 * cos_t,
                                 tile[r0 + QK_ROPE_DIM:]], axis=0)
        qt_ref[0, h] = (roped * scale).astype(BF16)
    ones = jnp.ones((V_ONES_ROWS, blk), F32)
    for h in range(0, MLA_HEADS, 2):
        rows = lambda g: v_t[g * V_HEAD_DIM:(g + 1) * V_HEAD_DIM]
        vt_ref[0, h // 2, 0] = jnp.concatenate([rows(h), ones, rows(h + 1), ones], axis=0).astype(BF16)


def _mla_prep(p_mla, tables, prm, batch, seq, blk):
    t = p_mla.shape[0]
    n = seq // blk
    natural, transposed = tables
    tab = pl.BlockSpec((blk, LANES), lambda i: (i, 0))
    tab_t = pl.BlockSpec((QK_ROPE_DIM // 2, blk), lambda i: (0, i))
    consts = [prm["q_norm"], prm["kv_norm"], prm["w_qt"], prm["w_k"], prm["w_vt"]]
    return pl.pallas_call(
        _mla_prep_kernel,
        grid=(t // blk,),
        in_specs=[pl.BlockSpec((blk, MLA_PAD_COLS), lambda i: (i, 0)), tab, tab, tab, tab_t, tab_t]
        + [_const_spec(c.shape) for c in consts],
        out_specs=[pl.BlockSpec((1, MLA_HEADS, LANES, blk), lambda i: (i // n, 0, 0, i % n)),
                   pl.BlockSpec((1, MLA_HEADS, 1, blk, LANES), lambda i: (i // n, 0, i % n, 0, 0)),
                   pl.BlockSpec((1, MLA_HEADS // 2, 1, 2 * V_EXT_ROWS, blk), lambda i: (i // n, 0, i % n, 0, 0))],
        out_shape=[jax.ShapeDtypeStruct((batch, MLA_HEADS, LANES, seq), BF16),
                   jax.ShapeDtypeStruct((batch, MLA_HEADS, n, blk, LANES), BF16),
                   jax.ShapeDtypeStruct((batch, MLA_HEADS // 2, n, 2 * V_EXT_ROWS, blk), BF16)],
        compiler_params=_params("parallel"),
        name="mla_prep",
    )(p_mla, *natural, *transposed, *consts)


def _attn_kernel(qt_ref, k_ref, vt_ref, o_ref, m_ref, acc_ref, s0_ref, s1_ref):
    n_blk, blk = k_ref.shape[2], k_ref.shape[3]
    units = [(i, j) for i in range(n_blk) for j in range(i + 1)]
    bufs = (s0_ref, s1_ref)

    def scores(unit, s_ref):
        i, j = unit
        for h in range(2):
            s_ref[h] = jnp.dot(k_ref[0, h, j], qt_ref[0, h, :, i * blk:(i + 1) * blk],
                               preferred_element_type=F32)

    def update(unit, s_ref):
        i, j = unit
        if j == 0:
            m_ref[...] = jnp.full_like(m_ref, -jnp.inf)
            acc_ref[...] = jnp.zeros_like(acc_ref)
        for h in range(2):
            vt = vt_ref[0, 0, j, h * V_EXT_ROWS:(h + 1) * V_EXT_ROWS, :]
            for c0 in range(0, blk, ATTN_Q_PIECE):
                cols = slice(c0, c0 + ATTN_Q_PIECE)
                rows = c0 + ATTN_Q_PIECE if j == i else blk
                s = s_ref[h, 0:rows, cols]
                if j == i:
                    k_pos = lax.broadcasted_iota(jnp.int32, s.shape, 0)
                    q_pos = lax.broadcasted_iota(jnp.int32, s.shape, 1) + c0
                    s = jnp.where(k_pos <= q_pos, s, NEG_INF)
                m_old = m_ref[h, :, cols]
                m_new = jnp.maximum(m_old, jnp.max(s, axis=0, keepdims=True))
                pv = jnp.dot(vt[:, 0:rows], jnp.exp2(s - m_new).astype(BF16), preferred_element_type=F32)
                acc_ref[h, :, cols] = jnp.exp2(m_old - m_new) * acc_ref[h, :, cols] + pv
                m_ref[h, :, cols] = m_new
        if j == i:
            o_t = jnp.concatenate([acc_ref[h, 0:V_HEAD_DIM] / acc_ref[h, V_HEAD_DIM:V_HEAD_DIM + 1]
                                   for h in range(2)], axis=0)
            o_ref[0, i * blk:(i + 1) * blk, :] = o_t.T.astype(BF16)

    scores(units[0], bufs[0])
    for n, unit in enumerate(units):
        if n + 1 < len(units):
            scores(units[n + 1], bufs[(n + 1) % 2])
        update(unit, bufs[n % 2])


def _mla_attn(qt, k, vt, batch, seq, blk):
    n = seq // blk
    hv = MLA_HEADS * V_HEAD_DIM
    o = pl.pallas_call(
        _attn_kernel,
        grid=(batch, MLA_HEADS // 2),
        in_specs=[pl.BlockSpec((1, 2, LANES, seq), lambda b, p: (b, p, 0, 0)),
                  pl.BlockSpec((1, 2, n, blk, LANES), lambda b, p: (b, p, 0, 0, 0)),
                  pl.BlockSpec((1, 1, n, 2 * V_EXT_ROWS, blk), lambda b, p: (b, p, 0, 0, 0))],
        out_specs=pl.BlockSpec((1, seq, LANES), lambda b, p: (b, 0, p)),
        out_shape=jax.ShapeDtypeStruct((batch, seq, hv), BF16),
        scratch_shapes=[pltpu.VMEM((2, 1, blk), F32), pltpu.VMEM((2, V_EXT_ROWS, blk), F32),
                        pltpu.VMEM((2, blk, blk), F32), pltpu.VMEM((2, blk, blk), F32)],
        compiler_params=_params("parallel", "parallel"),
        name="mla_attn",
    )(qt, k, vt)
    return o.reshape(batch * seq, hv)


def _merge_kernel(y_ref, bonus_ref, gate_ref, o_ref, g_ref, h_ref, lnw_ref, lnb_ref, ones_ref,
                  woa_ref, wob_ref, wout_ref, out_ref):
    ones = ones_ref[...]
    inv_n = 1.0 / RWKV_HEAD_DIM
    y = y_ref[...]
    mu = _head_sums(y, ones) * inv_n
    d = y - mu
    var = _head_sums(d * d, ones) * inv_n
    yn = (d * lax.rsqrt(var + LNX_EPS)) * lnw_ref[...] + lnb_ref[...]
    z = (yn + bonus_ref[...]) * gate_ref[...]
    y_a = _dot(z, woa_ref[...])
    y_b = jnp.dot(o_ref[...], wob_ref[...], preferred_element_type=F32)
    g = g_ref[...].astype(F32)
    merged = jax.nn.sigmoid(g[:, 0:D_MODEL]) * y_a + jax.nn.sigmoid(g[:, D_MODEL:2 * D_MODEL]) * y_b
    out_ref[...] = h_ref[...] + _dot(merged, wout_ref[...])


def _merge(y, bonus, gate, o_mla, gates, h, prm, tm):
    t = h.shape[0]
    row = lambda c: pl.BlockSpec((tm, c), lambda i: (i, 0))
    consts = [prm["lnx_w"], prm["lnx_b"], prm["head_ones"], prm["w_oa"], prm["w_ob"], prm["w_out"]]
    return pl.pallas_call(
        _merge_kernel,
        grid=(t // tm,),
        in_specs=[row(RWKV_DIM)] * 4 + [row(2 * D_MODEL), row(D_MODEL)]
        + [_const_spec(c.shape) for c in consts],
        out_specs=row(D_MODEL),
        out_shape=jax.ShapeDtypeStruct((t, D_MODEL), F32),
        compiler_params=_params("parallel"),
        name="merge",
    )(y, bonus, gate, o_mla, gates, h, *consts)


def _ffn_kernel(blocks_per_seq, final, h_ref, g_ref, wg_ref, wv_ref, cw_ref, cb_ref, wd_ref, fg_ref,
                out_ref, n_ref, acc_ref, carry_ref):
    i = pl.program_id(0)
    f = pl.program_id(1)
    tm = h_ref.shape[0]
    rows = min(FFN_SUB_ROWS, tm)

    @pl.when(f == 0)
    def _():
        n_ref[...] = _rms(h_ref[...], g_ref[...]).astype(BF16)
        acc_ref[...] = jnp.zeros_like(acc_ref)

    def up(sb):
        n = n_ref[sb * rows:(sb + 1) * rows, :]
        return (jnp.dot(n, wg_ref[...], preferred_element_type=F32),
                jnp.dot(n, wv_ref[...], preferred_element_type=F32))

    tail = carry_ref[f]
    tail = jnp.where(i % blocks_per_seq == 0, jnp.zeros_like(tail), tail)
    cw = cw_ref[0]
    row = lax.broadcasted_iota(jnp.int32, (rows, 1), 0)
    sqrt_half = np.sqrt(0.5).astype(np.float32)
    nxt = up(0)
    for sb in range(tm // rows):
        u_gate, u_val = nxt
        if sb + 1 < tm // rows:
            nxt = up(sb + 1)
        back1 = jnp.where(row == 0, tail[SUBLANES - 1:SUBLANES], pltpu.roll(u_gate, 1, axis=0))
        back2 = jnp.where(row == 0, tail[SUBLANES - 2:SUBLANES - 1],
                          jnp.where(row == 1, tail[SUBLANES - 1:SUBLANES], pltpu.roll(u_gate, 2, axis=0)))
        c = cb_ref[0] + cw[0:1] * back2
        c = c + cw[1:2] * back1
        c = c + cw[2:3] * u_gate
        tail = u_gate[rows - SUBLANES:rows]
        act = 0.5 * c * (1.0 + lax.erf(c * sqrt_half))
        down = jnp.dot((act * u_val).astype(BF16), wd_ref[...], preferred_element_type=F32)
        acc_ref[sb * rows:(sb + 1) * rows, :] += down
    carry_ref[f] = tail

    @pl.when(f == pl.num_programs(1) - 1)
    def _():
        out = h_ref[...] + acc_ref[...]
        out_ref[...] = _rms(out, fg_ref[...]) if final else out


def _conv_ffn(h, prm, final_gain, seq, tm, fc):
    t = h.shape[0]
    nf = FFN_DIM // fc
    row = pl.BlockSpec((tm, D_MODEL), lambda i, f: (i, 0))
    final = final_gain is not None
    fg = final_gain.reshape(1, D_MODEL) if final else prm["ffn_norm"]
    return pl.pallas_call(
        functools.partial(_ffn_kernel, seq // tm, final),
        grid=(t // tm, nf),
        in_specs=[row, _const_spec((1, D_MODEL)),
                  pl.BlockSpec((D_MODEL, fc), lambda i, f: (0, f)),
                  pl.BlockSpec((D_MODEL, fc), lambda i, f: (0, nf + f)),
                  pl.BlockSpec((1, SUBLANES, fc), lambda i, f: (f, 0, 0)),
                  pl.BlockSpec((1, 1, fc), lambda i, f: (f, 0, 0)),
                  pl.BlockSpec((fc, D_MODEL), lambda i, f: (f, 0)),
                  _const_spec((1, D_MODEL))],
        out_specs=row,
        out_shape=jax.ShapeDtypeStruct((t, D_MODEL), F32),
        scratch_shapes=[pltpu.VMEM((tm, D_MODEL), BF16), pltpu.VMEM((tm, D_MODEL), F32),
                        pltpu.VMEM((nf, SUBLANES, fc), F32)],
        compiler_params=_params("arbitrary", "arbitrary"),
        name="conv_ffn",
    )(h, prm["ffn_norm"], prm["w_up"], prm["w_up"], prm["conv_w"], prm["conv_b"], prm["w_down"], fg)


def _layer_params(l, fc, attn_norm, w_in, mu_shift, decay_base, w_decay_up, iclr_base, w_iclr_up,
                  w_gate_up, k_k, k_a, r_k, lnx_w, lnx_b, w_out_rwkv, q_norm, w_q_up, kv_norm,
                  w_kv_up, w_out_mla, w_out, ffn_norm, w_ffn_up, conv_w, conv_b, w_ffn_down):
    row = lambda x: x.reshape(1, -1)
    w = w_in[l]
    w_mla_src = w[:, SHIFT_COLS:SHIFT_COLS + MLA_COLS]
    lat = Q_LORA_RANK + KV_LORA_RANK
    w_mla = jnp.zeros((D_MODEL, MLA_PAD_COLS), F32)
    w_mla = w_mla.at[:, 0:lat].set(w_mla_src[:, 0:lat])
    w_mla = w_mla.at[:, lat + ROPE_LANE0:lat + ROPE_LANE0 + QK_ROPE_DIM].set(w_mla_src[:, lat:])
    w_lora = jnp.zeros((DECAY_LORA + ICLR_LORA, 2 * RWKV_DIM), F32)
    w_lora = w_lora.at[0:DECAY_LORA, 0:RWKV_DIM].set(w_decay_up[l])
    w_lora = w_lora.at[DECAY_LORA:, RWKV_DIM:].set(w_iclr_up[l])
    head = jnp.arange(MXU_WIDTH) // RWKV_HEAD_DIM
    head_ones = (head[:, None] == head[None, :]).astype(BF16)
    qk = QK_NOPE_DIM + QK_ROPE_DIM
    w_q = jnp.pad(w_q_up[l].reshape(Q_LORA_RANK, MLA_HEADS, qk), ((0, 0), (0, 0), (0, LANES - qk)))
    w_kv = w_kv_up[l].reshape(KV_LORA_RANK, MLA_HEADS, QK_NOPE_DIM + V_HEAD_DIM)
    w_k = jnp.pad(w_kv[:, :, :QK_NOPE_DIM], ((0, 0), (0, 0), (0, LANES - QK_NOPE_DIM)))
    nf = FFN_DIM // fc
    return dict(
        attn_norm=row(attn_norm[l]),
        w_rw=w[:, :SHIFT_COLS].astype(BF16),
        w_mla=w_mla.astype(BF16),
        w_g=w[:, SHIFT_COLS + MLA_COLS:].astype(BF16),
        mu=row(mu_shift[l]), w_lora=w_lora.astype(BF16), w_gate_up=w_gate_up[l].astype(BF16),
        decay_base=row(decay_base[l]), iclr_base=row(iclr_base[l]), k_k=row(k_k[l]), k_a=row(k_a[l]),
        r_k=row(r_k[l]), head_ones=head_ones, lnx_w=row(lnx_w[l]), lnx_b=row(lnx_b[l]),
        q_norm=row(q_norm[l]), kv_norm=row(kv_norm[l]),
        w_qt=w_q.reshape(Q_LORA_RANK, MLA_HEADS * LANES).T.astype(BF16),
        w_k=w_k.reshape(KV_LORA_RANK, MLA_HEADS * LANES).astype(BF16),
        w_vt=w_kv[:, :, QK_NOPE_DIM:].reshape(KV_LORA_RANK, MLA_HEADS * V_HEAD_DIM).T.astype(BF16),
        w_oa=w_out_rwkv[l].astype(BF16), w_ob=w_out_mla[l].astype(BF16), w_out=w_out[l].astype(BF16),
        ffn_norm=row(ffn_norm[l]), w_up=w_ffn_up[l].astype(BF16), w_down=w_ffn_down[l].astype(BF16),
        conv_w=jnp.pad(conv_w[l].reshape(3, nf, fc).transpose(1, 0, 2), ((0, 0), (0, SUBLANES - 3), (0, 0))),
        conv_b=conv_b[l].reshape(nf, 1, fc),
    )


def _tile_plan(seq):
    pick = lambda want: min(want, seq)
    return dict(proj=pick(512), attn=pick(512), merge=pick(512), ffn=pick(1024),
                ffn_cols=FFN_DIM // 2)


def kernel(x, positions, attn_norm, w_in, mu_shift, decay_base, w_decay_up, iclr_base, w_iclr_up, w_gate_up, k_k, k_a, r_k, lnx_w, lnx_b, w_out_rwkv, q_norm, w_q_up, kv_norm, w_kv_up, w_out_mla, w_out, ffn_norm, w_ffn_up, conv_w, conv_b, w_ffn_down, final_norm):
    batch, seq, _ = x.shape
    depth = w_in.shape[0]
    assert seq % WKV_CHUNK == 0 and x.shape[2] == D_MODEL
    plan = _tile_plan(seq)
    tables = _rope_tables(positions)
    h = x.reshape(batch * seq, D_MODEL)
    for l in range(depth):
        prm = _layer_params(l, plan["ffn_cols"], attn_norm, w_in, mu_shift, decay_base, w_decay_up,
                            iclr_base, w_iclr_up, w_gate_up, k_k, k_a, r_k, lnx_w, lnx_b, w_out_rwkv,
                            q_norm, w_q_up, kv_norm, w_kv_up, w_out_mla, w_out, ffn_norm, w_ffn_up,
                            conv_w, conv_b, w_ffn_down)
        p_mla, gates, r, lw, k, v, a, b, gate, bonus = _inproj(h, prm, seq, plan["proj"])
        y = _wkv(r, lw, k, v, a, b, batch, seq)
        q_h, k_h, v_h = _mla_prep(p_mla, tables, prm, batch, seq, plan["attn"])
        o_mla = _mla_attn(q_h, k_h, v_h, batch, seq, plan["attn"])
        h = _merge(y, bonus, gate, o_mla, gates, h, prm, plan["merge"])
        last = final_norm if l == depth - 1 else None
        h = _conv_ffn(h, prm, last, seq, plan["ffn"], plan["ffn_cols"])
    return h.reshape(batch, seq, D_MODEL)
```

```python
import functools

import jax
import jax.numpy as jnp
import numpy as np
from jax import lax
from jax.experimental import pallas as pl
from jax.experimental.pallas import tpu as pltpu

F32 = jnp.float32
BF16 = jnp.bfloat16

D_MODEL = 1024
RWKV_HEADS = 8
RWKV_HEAD_DIM = 64
RWKV_DIM = RWKV_HEADS * RWKV_HEAD_DIM
DECAY_LORA = 64
ICLR_LORA = 64
GATE_LORA = 128
LNX_EPS = 64e-5
MLA_HEADS = 8
QK_NOPE_DIM = 64
QK_ROPE_DIM = 32
V_HEAD_DIM = 64
Q_LORA_RANK = 256
KV_LORA_RANK = 128
ROPE_THETA = 10000.0
NEG_INF = -1e30
FFN_DIM = 2816
NORM_EPS = 1e-6
SHIFT_COLS = 3 * RWKV_DIM + DECAY_LORA + ICLR_LORA + GATE_LORA
MLA_COLS = Q_LORA_RANK + KV_LORA_RANK + QK_ROPE_DIM

LANES = 128
SUBLANES = 8
MXU_WIDTH = 256
VMEM_LIMIT_BYTES = 56 * 1024 * 1024

MLA_PAD_COLS = 4 * LANES
ROPE_LANE0 = QK_NOPE_DIM
WKV_CHUNK = 64
PAIR = 2 * RWKV_HEAD_DIM
LOG2_E = float(np.log2(np.e))
V_ONES_ROWS = 16
V_EXT_ROWS = V_HEAD_DIM + V_ONES_ROWS
ATTN_Q_PIECE = 256
FFN_SUB_ROWS = 256


def _params(*semantics, flags=None):
    return pltpu.CompilerParams(dimension_semantics=semantics, vmem_limit_bytes=VMEM_LIMIT_BYTES,
                                flags=flags)


def _dot(a, b):
    return jnp.dot(a.astype(BF16), b.astype(BF16), preferred_element_type=F32)


def _dot_nt(a, b):
    return lax.dot_general(a.astype(BF16), b.astype(BF16), (((1,), (1,)), ((), ())),
                           preferred_element_type=F32)


def _split_bf16(x, terms):
    parts = []
    rest = x
    for _ in range(terms):
        p = rest.astype(BF16)
        parts.append(p)
        rest = rest - p.astype(F32)
    return parts


def _dot_split_lhs(a, b_bf16, terms):
    out = None
    for p in _split_bf16(a, terms):
        d = jnp.dot(p, b_bf16, preferred_element_type=F32)
        out = d if out is None else out + d
    return out


def _head_sums(x, ones):
    w = ones.shape[0]
    return jnp.concatenate([_dot_split_lhs(x[:, c:c + w], ones, 2) for c in range(0, x.shape[1], w)],
                           axis=1)


def _rms(x, g):
    ms = jnp.mean(x * x, axis=-1, keepdims=True)
    return (x * lax.rsqrt(ms + NORM_EPS)) * g


def _const_spec(shape):
    return pl.BlockSpec(shape, lambda *_: (0,) * len(shape))


def _rope_tables_kernel(pos_ref, invf_ref, cos_ref, sin_ref):
    ang = pos_ref[...] * invf_ref[...]
    cos_ref[...] = jnp.cos(ang)
    sin_ref[...] = jnp.sin(ang)


def _rope_tables(positions):
    t = positions.size
    half = QK_ROPE_DIM // 2
    per_row = LANES // half
    inv_freq = jnp.power(ROPE_THETA, -jnp.arange(0, QK_ROPE_DIM, 2, dtype=F32) / QK_ROPE_DIM)
    invf = jnp.tile(inv_freq, per_row).reshape(1, LANES)
    posc = jnp.repeat(positions.reshape(t // per_row, per_row).astype(F32), half, axis=1)
    spec = pl.BlockSpec(posc.shape, lambda i: (0, 0))
    out = jax.ShapeDtypeStruct(posc.shape, F32)
    cos, sin = pl.pallas_call(
        _rope_tables_kernel,
        grid=(1,),
        in_specs=[spec, _const_spec((1, LANES))],
        out_specs=[spec, spec],
        out_shape=[out, out],
        compiler_params=_params("arbitrary"),
        name="rope_tables",
    )(posc, invf)
    return cos.reshape(t, half).T, sin.reshape(t, half).T


def _rope_t(tile, cos_t, sin_t):
    half = QK_ROPE_DIM // 2
    r0 = ROPE_LANE0
    x1 = tile[r0:r0 + half]
    x2 = tile[r0 + half:r0 + QK_ROPE_DIM]
    return jnp.concatenate([tile[0:r0], x1 * cos_t - x2 * sin_t, x1 * sin_t + x2 * cos_t,
                            tile[r0 + QK_ROPE_DIM:]], axis=0)


def _inproj_kernel(blocks_per_seq, x_ref, g_ref, wrw_ref, wmla_ref, wgate_ref, mu_ref, wl_ref,
                   wg_ref, dbase_ref, ibase_ref, kk_ref, ka_ref, rk_ref, ones_ref,
                   pmla_ref, pg_ref, r_ref, lw_ref, k_ref, v_ref, a_ref, b_ref, gate_ref, bonus_ref,
                   tail_ref):
    i = pl.program_id(0)
    n = _rms(x_ref[...], g_ref[...]).astype(BF16)
    pmla_ref[...] = jnp.dot(n, wmla_ref[...], preferred_element_type=F32)
    pg_ref[...] = jnp.dot(n, wgate_ref[...], preferred_element_type=F32).astype(BF16)
    x = jnp.dot(n, wrw_ref[...], preferred_element_type=F32)
    tm = x.shape[0]
    prev = tail_ref[SUBLANES - 1:SUBLANES, :]
    prev = jnp.where(i % blocks_per_seq == 0, jnp.zeros_like(prev), prev)
    tail_ref[...] = x[tm - SUBLANES:tm]
    row = lax.broadcasted_iota(jnp.int32, (tm, 1), 0)
    shifted = jnp.where(row == 0, prev, pltpu.roll(x, 1, axis=0))
    xs = x + (shifted - x) * mu_ref[...]

    c0 = 3 * RWKV_DIM
    p_r = xs[:, 0:RWKV_DIM]
    p_k = xs[:, RWKV_DIM:2 * RWKV_DIM]
    p_v = xs[:, 2 * RWKV_DIM:c0]
    p_wa = xs[:, c0:c0 + DECAY_LORA + ICLR_LORA]
    p_g = xs[:, c0 + DECAY_LORA + ICLR_LORA:SHIFT_COLS]

    lane = lax.broadcasted_iota(jnp.int32, p_wa.shape, 1)
    lora_in = jnp.where(lane < DECAY_LORA, jnp.tanh(p_wa), p_wa)
    lora = _dot(lora_in, wl_ref[...])
    log_w = -jax.nn.softplus(-(dbase_ref[...] + lora[:, 0:RWKV_DIM])) - 0.5
    iclr = jax.nn.sigmoid(ibase_ref[...] + lora[:, RWKV_DIM:2 * RWKV_DIM])
    gate = _dot(jax.nn.sigmoid(p_g), wg_ref[...])

    ones = ones_ref[...]
    kk = p_k * kk_ref[...]
    kk = kk * lax.rsqrt(_head_sums(kk * kk, ones) + 1e-12)
    k = p_k * (1.0 + (iclr - 1.0) * ka_ref[...])

    r_ref[...] = p_r
    lw_ref[...] = -jnp.exp(log_w)
    k_ref[...] = k
    v_ref[...] = p_v
    a_ref[...] = -kk
    b_ref[...] = kk * iclr
    gate_ref[...] = gate
    bonus_ref[...] = _head_sums(p_r * k * rk_ref[...], ones) * p_v


def _inproj(h, prm, seq, tm):
    t = h.shape[0]
    row = lambda c: pl.BlockSpec((tm, c), lambda i: (i, 0))
    out = lambda c, dt: jax.ShapeDtypeStruct((t, c), dt)
    consts = [prm["attn_norm"], prm["w_rw"], prm["w_mla"], prm["w_g"], prm["mu"], prm["w_lora"],
              prm["w_gate_up"], prm["decay_base"], prm["iclr_base"], prm["k_k"], prm["k_a"], prm["r_k"],
              prm["head_ones"]]
    return pl.pallas_call(
        functools.partial(_inproj_kernel, seq // tm),
        grid=(t // tm,),
        in_specs=[row(D_MODEL)] + [_const_spec(c.shape) for c in consts],
        out_specs=[row(MLA_PAD_COLS), row(2 * D_MODEL)] + [row(RWKV_DIM)] * 8,
        out_shape=[out(MLA_PAD_COLS, F32), out(2 * D_MODEL, BF16)] + [out(RWKV_DIM, F32)] * 8,
        scratch_shapes=[pltpu.VMEM((SUBLANES, SHIFT_COLS), F32)],
        compiler_params=_params("arbitrary"),
        name="inproj",
    )(h, *consts)


def _wkv_pair(load, state_ref, idx, store_y, zbuf, ltri, eye):
    c = WKV_CHUNK
    lane = lax.broadcasted_iota(jnp.int32, (1, PAIR), 1)
    lo = lane < RWKV_HEAD_DIM
    r, lw, k, v, a, b = load()

    cum = _dot_split_lhs_left(ltri, lw)
    yield
    cum_last = cum[c - 1:c, :]
    g_inv = jnp.exp(-cum)
    a_t = a * jnp.exp(cum - lw)
    r_t = r * jnp.exp(cum)
    zero = jnp.zeros_like(a_t)
    lhs = jnp.concatenate([jnp.where(lo, a_t, zero), jnp.where(lo, zero, a_t),
                           jnp.where(lo, r_t, zero), jnp.where(lo, zero, r_t)], axis=0)
    b_t = b * g_inv
    k_t = k * g_inv
    a_all = _dot_nt(lhs, jnp.concatenate([b_t, k_t], axis=0))
    yield
    swapped = pltpu.roll(a_all, c, axis=1)
    head0_rows = (lax.broadcasted_iota(jnp.int32, (4 * c, 1), 0) // c) % 2 == 0
    a_b = jnp.where(head0_rows, a_all, swapped)
    a_k = jnp.where(head0_rows, swapped, a_all)

    rr = lax.broadcasted_iota(jnp.int32, (2 * c, 2 * c), 0)
    cc = lax.broadcasted_iota(jnp.int32, (2 * c, 2 * c), 1)
    same_head = (rr // c) == (cc // c)
    strict = same_head & ((cc % c) < (rr % c))
    incl = same_head & ((cc % c) <= (rr % c))
    l_bd = jnp.where(strict, a_b[0:2 * c], 0.0)
    aak_bd = jnp.where(strict, a_k[0:2 * c], 0.0)
    arb_bd = jnp.where(incl, a_b[2 * c:4 * c], 0.0)
    ark_bd = jnp.where(incl, a_k[2 * c:4 * c], 0.0)
    v_st = jnp.concatenate([jnp.where(lo, v, zero), jnp.where(lo, zero, v)], axis=0)
    av = _dot(jnp.concatenate([aak_bd, ark_bd], axis=0), v_st)
    q_st = av[0:2 * c]
    arkv_st = av[2 * c:4 * c]

    t_bd = eye + l_bd
    p = _dot(l_bd, l_bd)
    yield
    n = 2
    while 2 * n < c:
        pp_pt = _dot(p, jnp.concatenate([p, t_bd], axis=1))
        yield
        p = pp_pt[:, 0:2 * c]
        t_bd = t_bd + pp_pt[:, 2 * c:4 * c]
        n *= 2
    t_bd = t_bd + _dot(p, t_bd)
    yield

    tq_ta = _dot(t_bd, jnp.concatenate([lhs[0:2 * c], q_st], axis=1))
    yield
    rq_ra = _dot(arb_bd, tq_ta)
    yield
    ta_st = tq_ta[:, 0:PAIR]
    tq_st = tq_ta[:, PAIR:2 * PAIR]
    rh_st = lhs[2 * c:4 * c] + rq_ra[:, 0:PAIR]
    yh_st = rq_ra[:, PAIR:2 * PAIR] + arkv_st
    unstack = lambda x: x[0:c] + x[c:2 * c]
    ta = unstack(ta_st)
    rh = unstack(rh_st)
    tq = unstack(tq_st)
    yh = unstack(yh_st)

    state = state_ref[idx]
    zbuf[...] = _dot_nt(state, jnp.concatenate([ta, rh], axis=0))
    yield
    z = zbuf[...] + jnp.concatenate([tq, yh], axis=0).T
    v_t = jnp.concatenate([v, v], axis=0).T
    lane2 = lax.broadcasted_iota(jnp.int32, (1, 2 * c), 1)
    g_out = jnp.exp(cum_last - cum)
    upd = _dot(jnp.where(lane2 < c, z, v_t), jnp.concatenate([b * g_out, k * g_out], axis=0))
    yield
    state_ref[idx] = state * jnp.exp(cum_last) + jnp.where(same_head, upd, 0.0)
    store_y(z.T[c:2 * c, :])


def _dot_split_lhs_left(m_bf16, x):
    out = None
    for p in _split_bf16(x, 3):
        d = jnp.dot(m_bf16, p, preferred_element_type=F32)
        out = d if out is None else out + d
    return out


def _wkv_kernel(r_ref, lw_ref, k_ref, v_ref, a_ref, b_ref, ltri_ref, eye_ref, y_ref, state_ref, z_ref):
    @pl.when(pl.program_id(0) == 0)
    def _():
        state_ref[...] = jnp.zeros_like(state_ref)

    ltri = ltri_ref[...]
    eye = eye_ref[...]
    pairs = RWKV_DIM // PAIR
    chains = []
    for bi in range(r_ref.shape[0]):
        for p in range(pairs):
            sl = slice(p * PAIR, (p + 1) * PAIR)

            def load(bi=bi, sl=sl):
                return tuple(ref[bi, :, sl] for ref in (r_ref, lw_ref, k_ref, v_ref, a_ref, b_ref))

            def store_y(y, bi=bi, sl=sl):
                y_ref[bi, :, sl] = y

            chain = bi * pairs + p
            chains.append(_wkv_pair(load, state_ref, chain, store_y, z_ref.at[chain], ltri, eye))
    while chains:
        chains = [ch for ch in chains if next(ch, StopIteration) is not StopIteration]


def _wkv(r, lw, k, v, a, b, batch, seq):
    c = WKV_CHUNK
    shp = (batch, seq, RWKV_DIM)
    args = [x.reshape(shp) for x in (r, lw, k, v, a, b)]
    ltri = jnp.tril(jnp.ones((c, c), F32)).astype(BF16)
    eye = jnp.eye(2 * c, dtype=F32)
    blk = pl.BlockSpec((batch, c, RWKV_DIM), lambda ci: (0, ci, 0))
    y = pl.pallas_call(
        _wkv_kernel,
        grid=(seq // c,),
        in_specs=[blk] * 6 + [_const_spec((c, c)), _const_spec((2 * c, 2 * c))],
        out_specs=blk,
        out_shape=jax.ShapeDtypeStruct(shp, F32),
        scratch_shapes=[pltpu.VMEM((batch * (RWKV_DIM // PAIR), PAIR, PAIR), F32)] * 2,
        compiler_params=_params("arbitrary"),
        name="wkv",
    )(*args, ltri, eye)
    return y.reshape(batch * seq, RWKV_DIM)


def _mla_prep_kernel(p_ref, cost_ref, sint_ref, qn_ref, kvn_ref, wqt_ref, wk_ref, wvt_ref,
                     qt_ref, k_ref, vt_ref):
    x = p_ref[...]
    blk = x.shape[0]
    cos_t, sin_t = cost_ref[...], sint_ref[...]
    scale = (QK_NOPE_DIM + QK_ROPE_DIM) ** -0.5 * LOG2_E
    qn = _rms(x[:, 0:Q_LORA_RANK], qn_ref[...]).astype(BF16)
    kvn = _rms(x[:, Q_LORA_RANK:Q_LORA_RANK + KV_LORA_RANK], kvn_ref[...]).astype(BF16)
    k_pe = _rope_t(x[:, Q_LORA_RANK + KV_LORA_RANK:MLA_PAD_COLS].T, cos_t, sin_t).T
    kf = jnp.dot(kvn, wk_ref[...], preferred_element_type=F32)
    for h in range(MLA_HEADS):
        k_ref[0, h, 0] = (kf[:, h * LANES:(h + 1) * LANES] + k_pe).astype(BF16)
    q_t = _dot_nt(wqt_ref[...], qn)
    v_t = _dot_nt(wvt_ref[...], kvn)
    for h in range(MLA_HEADS):
        qt_ref[0, h] = (_rope_t(q_t[h * LANES:(h + 1) * LANES], cos_t, sin_t) * scale).astype(BF16)
    ones = jnp.ones((V_ONES_ROWS, blk), F32)
    for h in range(0, MLA_HEADS, 2):
        rows = lambda g: v_t[g * V_HEAD_DIM:(g + 1) * V_HEAD_DIM]
        vt_ref[0, h // 2, 0] = jnp.concatenate([rows(h), ones, rows(h + 1), ones], axis=0).astype(BF16)


def _mla_prep(p_mla, tables, prm, batch, seq, blk):
    t = p_mla.shape[0]
    n = seq // blk
    tab_t = pl.BlockSpec((QK_ROPE_DIM // 2, blk), lambda i: (0, i))
    consts = [prm["q_norm"], prm["kv_norm"], prm["w_qt"], prm["w_k"], prm["w_vt"]]
    return pl.pallas_call(
        _mla_prep_kernel,
        grid=(t // blk,),
        in_specs=[pl.BlockSpec((blk, MLA_PAD_COLS), lambda i: (i, 0)), tab_t, tab_t]
        + [_const_spec(c.shape) for c in consts],
        out_specs=[pl.BlockSpec((1, MLA_HEADS, LANES, blk), lambda i: (i // n, 0, 0, i % n)),
                   pl.BlockSpec((1, MLA_HEADS, 1, blk, LANES), lambda i: (i // n, 0, i % n, 0, 0)),
                   pl.BlockSpec((1, MLA_HEADS // 2, 1, 2 * V_EXT_ROWS, blk), lambda i: (i // n, 0, i % n, 0, 0))],
        out_shape=[jax.ShapeDtypeStruct((batch, MLA_HEADS, LANES, seq), BF16),
                   jax.ShapeDtypeStruct((batch, MLA_HEADS, n, blk, LANES), BF16),
                   jax.ShapeDtypeStruct((batch, MLA_HEADS // 2, n, 2 * V_EXT_ROWS, blk), BF16)],
        compiler_params=_params("parallel"),
        name="mla_prep",
    )(p_mla, *tables, *consts)


def _attn_kernel(qt_ref, k_ref, vt_ref, o_ref, m_ref, acc_ref, s0_ref, s1_ref):
    n_blk, blk = k_ref.shape[2], k_ref.shape[3]
    units = [(i, j) for i in range(n_blk) for j in range(i + 1)]
    bufs = (s0_ref, s1_ref)

    def scores(unit, s_ref):
        i, j = unit
        for h in range(2):
            s_ref[h] = jnp.dot(k_ref[0, h, j], qt_ref[0, h, :, i * blk:(i + 1) * blk],
                               preferred_element_type=F32)

    def update(unit, s_ref):
        i, j = unit
        if j == 0:
            m_ref[...] = jnp.full_like(m_ref, -jnp.inf)
            acc_ref[...] = jnp.zeros_like(acc_ref)
        for h in range(2):
            vt = vt_ref[0, 0, j, h * V_EXT_ROWS:(h + 1) * V_EXT_ROWS, :]
            for c0 in range(0, blk, ATTN_Q_PIECE):
                cols = slice(c0, c0 + ATTN_Q_PIECE)
                rows = c0 + ATTN_Q_PIECE if j == i else blk
                s = s_ref[h, 0:rows, cols]
                if j == i:
                    k_pos = lax.broadcasted_iota(jnp.int32, s.shape, 0)
                    q_pos = lax.broadcasted_iota(jnp.int32, s.shape, 1) + c0
                    s = jnp.where(k_pos <= q_pos, s, NEG_INF)
                m_old = m_ref[h, :, cols]
                m_new = jnp.maximum(m_old, jnp.max(s, axis=0, keepdims=True))
                pv = jnp.dot(vt[:, 0:rows], jnp.exp2(s - m_new).astype(BF16), preferred_element_type=F32)
                acc_ref[h, :, cols] = jnp.exp2(m_old - m_new) * acc_ref[h, :, cols] + pv
                m_ref[h, :, cols] = m_new
        if j == i:
            o_t = jnp.concatenate([acc_ref[h, 0:V_HEAD_DIM] / acc_ref[h, V_HEAD_DIM:V_HEAD_DIM + 1]
                                   for h in range(2)], axis=0)
            o_ref[0, i * blk:(i + 1) * blk, :] = o_t.T.astype(BF16)

    scores(units[0], bufs[0])
    for n, unit in enumerate(units):
        if n + 1 < len(units):
            scores(units[n + 1], bufs[(n + 1) % 2])
        update(unit, bufs[n % 2])


def _mla_attn(qt, k, vt, batch, seq, blk):
    n = seq // blk
    hv = MLA_HEADS * V_HEAD_DIM
    o = pl.pallas_call(
        _attn_kernel,
        grid=(batch, MLA_HEADS // 2),
        in_specs=[pl.BlockSpec((1, 2, LANES, seq), lambda b, p: (b, p, 0, 0)),
                  pl.BlockSpec((1, 2, n, blk, LANES), lambda b, p: (b, p, 0, 0, 0)),
                  pl.BlockSpec((1, 1, n, 2 * V_EXT_ROWS, blk), lambda b, p: (b, p, 0, 0, 0))],
        out_specs=pl.BlockSpec((1, seq, LANES), lambda b, p: (b, 0, p)),
        out_shape=jax.ShapeDtypeStruct((batch, seq, hv), BF16),
        scratch_shapes=[pltpu.VMEM((2, 1, blk), F32), pltpu.VMEM((2, V_EXT_ROWS, blk), F32),
                        pltpu.VMEM((2, blk, blk), F32), pltpu.VMEM((2, blk, blk), F32)],
        compiler_params=_params("parallel", "parallel"),
        name="mla_attn",
    )(qt, k, vt)
    return o.reshape(batch * seq, hv)


def _merge_kernel(y_ref, bonus_ref, gate_ref, o_ref, g_ref, h_ref, lnw_ref, lnb_ref, ones_ref,
                  woa_ref, wob_ref, wout_ref, out_ref):
    ones = ones_ref[...]
    inv_n = 1.0 / RWKV_HEAD_DIM
    y = y_ref[...]
    mu = _head_sums(y, ones) * inv_n
    d = y - mu
    var = _head_sums(d * d, ones) * inv_n
    yn = (d * lax.rsqrt(var + LNX_EPS)) * lnw_ref[...] + lnb_ref[...]
    z = (yn + bonus_ref[...]) * gate_ref[...]
    y_a = _dot(z, woa_ref[...])
    y_b = jnp.dot(o_ref[...], wob_ref[...], preferred_element_type=F32)
    g = g_ref[...].astype(F32)
    merged = jax.nn.sigmoid(g[:, 0:D_MODEL]) * y_a + jax.nn.sigmoid(g[:, D_MODEL:2 * D_MODEL]) * y_b
    out_ref[...] = h_ref[...] + _dot(merged, wout_ref[...])


def _merge(y, bonus, gate, o_mla, gates, h, prm, tm):
    t = h.shape[0]
    row = lambda c: pl.BlockSpec((tm, c), lambda i: (i, 0))
    consts = [prm["lnx_w"], prm["lnx_b"], prm["head_ones"], prm["w_oa"], prm["w_ob"], prm["w_out"]]
    return pl.pallas_call(
        _merge_kernel,
        grid=(t // tm,),
        in_specs=[row(RWKV_DIM)] * 4 + [row(2 * D_MODEL), row(D_MODEL)]
        + [_const_spec(c.shape) for c in consts],
        out_specs=row(D_MODEL),
        out_shape=jax.ShapeDtypeStruct((t, D_MODEL), F32),
        compiler_params=_params("parallel"),
        name="merge",
    )(y, bonus, gate, o_mla, gates, h, *consts)


def _ffn_kernel(blocks_per_seq, final, h_ref, g_ref, wg_ref, wv_ref, cw_ref, cb_ref, wd_ref, fg_ref,
                out_ref, n_ref, acc_ref, carry_ref):
    i = pl.program_id(0)
    f = pl.program_id(1)
    tm = h_ref.shape[0]
    rows = min(FFN_SUB_ROWS, tm)

    @pl.when(f == 0)
    def _():
        n_ref[...] = _rms(h_ref[...], g_ref[...]).astype(BF16)
        acc_ref[...] = jnp.zeros_like(acc_ref)

    def up(sb):
        n = n_ref[sb * rows:(sb + 1) * rows, :]
        return (jnp.dot(n, wg_ref[...], preferred_element_type=F32),
                jnp.dot(n, wv_ref[...], preferred_element_type=F32))

    tail = carry_ref[f]
    tail = jnp.where(i % blocks_per_seq == 0, jnp.zeros_like(tail), tail)
    cw = cw_ref[0]
    row = lax.broadcasted_iota(jnp.int32, (rows, 1), 0)
    sqrt_half = np.sqrt(0.5).astype(np.float32)
    nxt = up(0)
    for sb in range(tm // rows):
        u_gate, u_val = nxt
        if sb + 1 < tm // rows:
            nxt = up(sb + 1)
        back1 = jnp.where(row == 0, tail[SUBLANES - 1:SUBLANES], pltpu.roll(u_gate, 1, axis=0))
        back2 = jnp.where(row == 0, tail[SUBLANES - 2:SUBLANES - 1],
                          jnp.where(row == 1, tail[SUBLANES - 1:SUBLANES], pltpu.roll(u_gate, 2, axis=0)))
        c = cb_ref[0] + cw[0:1] * back2
        c = c + cw[1:2] * back1
        c = c + cw[2:3] * u_gate
        tail = u_gate[rows - SUBLANES:rows]
        act = 0.5 * c * (1.0 + lax.erf(c * sqrt_half))
        down = jnp.dot((act * u_val).astype(BF16), wd_ref[...], preferred_element_type=F32)
        acc_ref[sb * rows:(sb + 1) * rows, :] += down
    carry_ref[f] = tail

    @pl.when(f == pl.num_programs(1) - 1)
    def _():
        out = h_ref[...] + acc_ref[...]
        out_ref[...] = _rms(out, fg_ref[...]) if final else out


def _conv_ffn(h, prm, final_gain, seq, tm, fc):
    t = h.shape[0]
    nf = FFN_DIM // fc
    row = pl.BlockSpec((tm, D_MODEL), lambda i, f: (i, 0))
    final = final_gain is not None
    fg = final_gain.reshape(1, D_MODEL) if final else prm["ffn_norm"]
    return pl.pallas_call(
        functools.partial(_ffn_kernel, seq // tm, final),
        grid=(t // tm, nf),
        in_specs=[row, _const_spec((1, D_MODEL)),
                  pl.BlockSpec((D_MODEL, fc), lambda i, f: (0, f)),
                  pl.BlockSpec((D_MODEL, fc), lambda i, f: (0, nf + f)),
                  pl.BlockSpec((1, SUBLANES, fc), lambda i, f: (f, 0, 0)),
                  pl.BlockSpec((1, 1, fc), lambda i, f: (f, 0, 0)),
                  pl.BlockSpec((fc, D_MODEL), lambda i, f: (f, 0)),
                  _const_spec((1, D_MODEL))],
        out_specs=row,
        out_shape=jax.ShapeDtypeStruct((t, D_MODEL), F32),
        scratch_shapes=[pltpu.VMEM((tm, D_MODEL), BF16), pltpu.VMEM((tm, D_MODEL), F32),
                        pltpu.VMEM((nf, SUBLANES, fc), F32)],
        compiler_params=_params("arbitrary", "arbitrary"),
        name="conv_ffn",
    )(h, prm["ffn_norm"], prm["w_up"], prm["w_up"], prm["conv_w"], prm["conv_b"], prm["w_down"], fg)


def _layer_params(l, fc, attn_norm, w_in, mu_shift, decay_base, w_decay_up, iclr_base, w_iclr_up,
                  w_gate_up, k_k, k_a, r_k, lnx_w, lnx_b, w_out_rwkv, q_norm, w_q_up, kv_norm,
                  w_kv_up, w_out_mla, w_out, ffn_norm, w_ffn_up, conv_w, conv_b, w_ffn_down):
    row = lambda x: x.reshape(1, -1)
    w = w_in[l]
    w_mla_src = w[:, SHIFT_COLS:SHIFT_COLS + MLA_COLS]
    lat = Q_LORA_RANK + KV_LORA_RANK
    w_mla = jnp.zeros((D_MODEL, MLA_PAD_COLS), F32)
    w_mla = w_mla.at[:, 0:lat].set(w_mla_src[:, 0:lat])
    w_mla = w_mla.at[:, lat + ROPE_LANE0:lat + ROPE_LANE0 + QK_ROPE_DIM].set(w_mla_src[:, lat:])
    w_lora = jnp.zeros((DECAY_LORA + ICLR_LORA, 2 * RWKV_DIM), F32)
    w_lora = w_lora.at[0:DECAY_LORA, 0:RWKV_DIM].set(w_decay_up[l])
    w_lora = w_lora.at[DECAY_LORA:, RWKV_DIM:].set(w_iclr_up[l])
    head = jnp.arange(MXU_WIDTH) // RWKV_HEAD_DIM
    head_ones = (head[:, None] == head[None, :]).astype(BF16)
    qk = QK_NOPE_DIM + QK_ROPE_DIM
    w_q = jnp.pad(w_q_up[l].reshape(Q_LORA_RANK, MLA_HEADS, qk), ((0, 0), (0, 0), (0, LANES - qk)))
    w_kv = w_kv_up[l].reshape(KV_LORA_RANK, MLA_HEADS, QK_NOPE_DIM + V_HEAD_DIM)
    w_k = jnp.pad(w_kv[:, :, :QK_NOPE_DIM], ((0, 0), (0, 0), (0, LANES - QK_NOPE_DIM)))
    nf = FFN_DIM // fc
    return dict(
        attn_norm=row(attn_norm[l]),
        w_rw=w[:, :SHIFT_COLS].astype(BF16),
        w_mla=w_mla.astype(BF16),
        w_g=w[:, SHIFT_COLS + MLA_COLS:].astype(BF16),
        mu=row(mu_shift[l]), w_lora=w_lora.astype(BF16), w_gate_up=w_gate_up[l].astype(BF16),
        decay_base=row(decay_base[l]), iclr_base=row(iclr_base[l]), k_k=row(k_k[l]), k_a=row(k_a[l]),
        r_k=row(r_k[l]), head_ones=head_ones, lnx_w=row(lnx_w[l]), lnx_b=row(lnx_b[l]),
        q_norm=row(q_norm[l]), kv_norm=row(kv_norm[l]),
        w_qt=w_q.reshape(Q_LORA_RANK, MLA_HEADS * LANES).T.astype(BF16),
        w_k=w_k.reshape(KV_LORA_RANK, MLA_HEADS * LANES).astype(BF16),
        w_vt=w_kv[:, :, QK_NOPE_DIM:].reshape(KV_LORA_RANK, MLA_HEADS * V_HEAD_DIM).T.astype(BF16),
        w_oa=w_out_rwkv[l].astype(BF16), w_ob=w_out_mla[l].astype(BF16), w_out=w_out[l].astype(BF16),
        ffn_norm=row(ffn_norm[l]), w_up=w_ffn_up[l].astype(BF16), w_down=w_ffn_down[l].astype(BF16),
        conv_w=jnp.pad(conv_w[l].reshape(3, nf, fc).transpose(1, 0, 2), ((0, 0), (0, SUBLANES - 3), (0, 0))),
        conv_b=conv_b[l].reshape(nf, 1, fc),
    )


def _tile_plan(seq):
    pick = lambda want: min(want, seq)
    return dict(proj=pick(512), attn=pick(512), merge=pick(512), ffn=pick(1024),
                ffn_cols=FFN_DIM // 2)


def kernel(x, positions, attn_norm, w_in, mu_shift, decay_base, w_decay_up, iclr_base, w_iclr_up, w_gate_up, k_k, k_a, r_k, lnx_w, lnx_b, w_out_rwkv, q_norm, w_q_up, kv_norm, w_kv_up, w_out_mla, w_out, ffn_norm, w_ffn_up, conv_w, conv_b, w_ffn_down, final_norm):
    batch, seq, _ = x.shape
    depth = w_in.shape[0]
    assert seq % WKV_CHUNK == 0 and x.shape[2] == D_MODEL
    plan = _tile_plan(seq)
    tables = _rope_tables(positions)
    h = x.reshape(batch * seq, D_MODEL)
    for l in range(depth):
        prm = _layer_params(l, plan["ffn_cols"], attn_norm, w_in, mu_shift, decay_base, w_decay_up,
                            iclr_base, w_iclr_up, w_gate_up, k_k, k_a, r_k, lnx_w, lnx_b, w_out_rwkv,
                            q_norm, w_q_up, kv_norm, w_kv_up, w_out_mla, w_out, ffn_norm, w_ffn_up,
                            conv_w, conv_b, w_ffn_down)
        p_mla, gates, r, lw, k, v, a, b, gate, bonus = _inproj(h, prm, seq, plan["proj"])
        y = _wkv(r, lw, k, v, a, b, batch, seq)
        q_h, k_h, v_h = _mla_prep(p_mla, tables, prm, batch, seq, plan["attn"])
        o_mla = _mla_attn(q_h, k_h, v_h, batch, seq, plan["attn"])
        h = _merge(y, bonus, gate, o_mla, gates, h, prm, plan["merge"])
        last = final_norm if l == depth - 1 else None
        h = _conv_ffn(h, prm, last, seq, plan["ffn"], plan["ffn_cols"])
    return h.reshape(batch, seq, D_MODEL)
```

```python
import functools

import jax
import jax.numpy as jnp
import numpy as np
from jax import lax
from jax.experimental import pallas as pl
from jax.experimental.pallas import tpu as pltpu

F32 = jnp.float32
BF16 = jnp.bfloat16

D_MODEL = 1024
RWKV_HEADS = 8
RWKV_HEAD_DIM = 64
RWKV_DIM = RWKV_HEADS * RWKV_HEAD_DIM
DECAY_LORA = 64
ICLR_LORA = 64
GATE_LORA = 128
LNX_EPS = 64e-5
MLA_HEADS = 8
QK_NOPE_DIM = 64
QK_ROPE_DIM = 32
V_HEAD_DIM = 64
Q_LORA_RANK = 256
KV_LORA_RANK = 128
ROPE_THETA = 10000.0
NEG_INF = -1e30
FFN_DIM = 2816
NORM_EPS = 1e-6
SHIFT_COLS = 3 * RWKV_DIM + DECAY_LORA + ICLR_LORA + GATE_LORA
MLA_COLS = Q_LORA_RANK + KV_LORA_RANK + QK_ROPE_DIM

LANES = 128
SUBLANES = 8
MXU_WIDTH = 256
VMEM_LIMIT_BYTES = 56 * 1024 * 1024

MLA_PAD_COLS = 4 * LANES
ROPE_LANE0 = QK_NOPE_DIM
WKV_CHUNK = 64
PAIR = 2 * RWKV_HEAD_DIM
LOG2_E = float(np.log2(np.e))
V_ONES_ROWS = 16
V_EXT_ROWS = V_HEAD_DIM + V_ONES_ROWS
ATTN_Q_PIECE = 256
FFN_SUB_ROWS = 256


def _params(*semantics, flags=None):
    return pltpu.CompilerParams(dimension_semantics=semantics, vmem_limit_bytes=VMEM_LIMIT_BYTES,
                                flags=flags)


def _dot(a, b):
    return jnp.dot(a.astype(BF16), b.astype(BF16), preferred_element_type=F32)


def _dot_nt(a, b):
    return lax.dot_general(a.astype(BF16), b.astype(BF16), (((1,), (1,)), ((), ())),
                           preferred_element_type=F32)


def _split_bf16(x, terms):
    parts = []
    rest = x
    for _ in range(terms):
        p = rest.astype(BF16)
        parts.append(p)
        rest = rest - p.astype(F32)
    return parts


def _dot_split_lhs(a, b_bf16, terms):
    out = None
    for p in _split_bf16(a, terms):
        d = jnp.dot(p, b_bf16, preferred_element_type=F32)
        out = d if out is None else out + d
    return out


def _head_sums(x, ones):
    w = ones.shape[0]
    return jnp.concatenate([_dot_split_lhs(x[:, c:c + w], ones, 2) for c in range(0, x.shape[1], w)],
                           axis=1)


def _rms(x, g):
    ms = jnp.mean(x * x, axis=-1, keepdims=True)
    return (x * lax.rsqrt(ms + NORM_EPS)) * g


def _const_spec(shape):
    return pl.BlockSpec(shape, lambda *_: (0,) * len(shape))


def _rope_tables_kernel(pos_ref, invf_ref, cos_ref, sin_ref):
    ang = pos_ref[...] * invf_ref[...]
    cos_ref[...] = jnp.cos(ang)
    sin_ref[...] = jnp.sin(ang)


def _rope_tables(positions):
    t = positions.size
    half = QK_ROPE_DIM // 2
    per_row = LANES // half
    inv_freq = jnp.power(ROPE_THETA, -jnp.arange(0, QK_ROPE_DIM, 2, dtype=F32) / QK_ROPE_DIM)
    invf = jnp.tile(inv_freq, per_row).reshape(1, LANES)
    posc = jnp.repeat(positions.reshape(t // per_row, per_row).astype(F32), half, axis=1)
    spec = pl.BlockSpec(posc.shape, lambda i: (0, 0))
    out = jax.ShapeDtypeStruct(posc.shape, F32)
    cos, sin = pl.pallas_call(
        _rope_tables_kernel,
        grid=(1,),
        in_specs=[spec, _const_spec((1, LANES))],
        out_specs=[spec, spec],
        out_shape=[out, out],
        compiler_params=_params("arbitrary"),
        name="rope_tables",
    )(posc, invf)
    return cos.reshape(t, half).T, sin.reshape(t, half).T


def _rope_t(tile, cos_t, sin_t):
    half = QK_ROPE_DIM // 2
    r0 = ROPE_LANE0
    x1 = tile[r0:r0 + half]
    x2 = tile[r0 + half:r0 + QK_ROPE_DIM]
    return jnp.concatenate([tile[0:r0], x1 * cos_t - x2 * sin_t, x1 * sin_t + x2 * cos_t,
                            tile[r0 + QK_ROPE_DIM:]], axis=0)


def _inproj_kernel(blocks_per_seq, x_ref, g_ref, wrw_ref, wmla_ref, wgate_ref, mu_ref, wl_ref,
                   wg_ref, dbase_ref, ibase_ref, kk_ref, ka_ref, rk_ref, ones_ref,
                   pmla_ref, pg_ref, r_ref, lw_ref, k_ref, v_ref, a_ref, b_ref, gate_ref, bonus_ref,
                   tail_ref):
    i = pl.program_id(0)
    n = _rms(x_ref[...], g_ref[...]).astype(BF16)
    x = jnp.dot(n, wrw_ref[...], preferred_element_type=F32)
    pmla_ref[...] = jnp.dot(n, wmla_ref[...], preferred_element_type=F32)
    pg_ref[...] = jnp.dot(n, wgate_ref[...], preferred_element_type=F32).astype(BF16)
    tm = x.shape[0]
    prev = tail_ref[SUBLANES - 1:SUBLANES, :]
    prev = jnp.where(i % blocks_per_seq == 0, jnp.zeros_like(prev), prev)
    tail_ref[...] = x[tm - SUBLANES:tm]
    row = lax.broadcasted_iota(jnp.int32, (tm, 1), 0)
    shifted = jnp.where(row == 0, prev, pltpu.roll(x, 1, axis=0))
    xs = x + (shifted - x) * mu_ref[...]

    c0 = 3 * RWKV_DIM
    p_r = xs[:, 0:RWKV_DIM]
    p_k = xs[:, RWKV_DIM:2 * RWKV_DIM]
    p_v = xs[:, 2 * RWKV_DIM:c0]
    p_wa = xs[:, c0:c0 + DECAY_LORA + ICLR_LORA]
    p_g = xs[:, c0 + DECAY_LORA + ICLR_LORA:SHIFT_COLS]

    lane = lax.broadcasted_iota(jnp.int32, p_wa.shape, 1)
    lora_in = jnp.where(lane < DECAY_LORA, jnp.tanh(p_wa), p_wa)
    lora = _dot(lora_in, wl_ref[...])
    log_w = -jax.nn.softplus(-(dbase_ref[...] + lora[:, 0:RWKV_DIM])) - 0.5
    iclr = jax.nn.sigmoid(ibase_ref[...] + lora[:, RWKV_DIM:2 * RWKV_DIM])
    gate = _dot(jax.nn.sigmoid(p_g), wg_ref[...])

    ones = ones_ref[...]
    kk = p_k * kk_ref[...]
    kk = kk * lax.rsqrt(_head_sums(kk * kk, ones) + 1e-12)
    k = p_k * (1.0 + (iclr - 1.0) * ka_ref[...])

    r_ref[...] = p_r
    lw_ref[...] = -jnp.exp(log_w)
    k_ref[...] = k
    v_ref[...] = p_v
    a_ref[...] = -kk
    b_ref[...] = kk * iclr
    gate_ref[...] = gate
    bonus_ref[...] = _head_sums(p_r * k * rk_ref[...], ones) * p_v


def _inproj(h, prm, seq, tm):
    t = h.shape[0]
    row = lambda c: pl.BlockSpec((tm, c), lambda i: (i, 0))
    out = lambda c, dt: jax.ShapeDtypeStruct((t, c), dt)
    consts = [prm["attn_norm"], prm["w_rw"], prm["w_mla"], prm["w_g"], prm["mu"], prm["w_lora"],
              prm["w_gate_up"], prm["decay_base"], prm["iclr_base"], prm["k_k"], prm["k_a"], prm["r_k"],
              prm["head_ones"]]
    return pl.pallas_call(
        functools.partial(_inproj_kernel, seq // tm),
        grid=(t // tm,),
        in_specs=[row(D_MODEL)] + [_const_spec(c.shape) for c in consts],
        out_specs=[row(MLA_PAD_COLS), row(2 * D_MODEL)] + [row(RWKV_DIM)] * 8,
        out_shape=[out(MLA_PAD_COLS, F32), out(2 * D_MODEL, BF16)] + [out(RWKV_DIM, F32)] * 8,
        scratch_shapes=[pltpu.VMEM((SUBLANES, SHIFT_COLS), F32)],
        compiler_params=_params("arbitrary"),
        name="inproj",
    )(h, *consts)


def _wkv_pair(load, state_ref, idx, store_y, zbuf, ltri, eye):
    c = WKV_CHUNK
    lane = lax.broadcasted_iota(jnp.int32, (1, PAIR), 1)
    lo = lane < RWKV_HEAD_DIM
    r, lw, k, v, a, b = load()

    cum = _dot_split_lhs_left(ltri, lw)
    yield
    cum_last = cum[c - 1:c, :]
    g_inv = jnp.exp(-cum)
    a_t = a * jnp.exp(cum - lw)
    r_t = r * jnp.exp(cum)
    zero = jnp.zeros_like(a_t)
    lhs = jnp.concatenate([jnp.where(lo, a_t, zero), jnp.where(lo, zero, a_t),
                           jnp.where(lo, r_t, zero), jnp.where(lo, zero, r_t)], axis=0)
    b_t = b * g_inv
    k_t = k * g_inv
    a_all = _dot_nt(lhs, jnp.concatenate([b_t, k_t], axis=0))
    yield
    swapped = pltpu.roll(a_all, c, axis=1)
    head0_rows = (lax.broadcasted_iota(jnp.int32, (4 * c, 1), 0) // c) % 2 == 0
    a_b = jnp.where(head0_rows, a_all, swapped)
    a_k = jnp.where(head0_rows, swapped, a_all)

    rr = lax.broadcasted_iota(jnp.int32, (2 * c, 2 * c), 0)
    cc = lax.broadcasted_iota(jnp.int32, (2 * c, 2 * c), 1)
    same_head = (rr // c) == (cc // c)
    strict = same_head & ((cc % c) < (rr % c))
    incl = same_head & ((cc % c) <= (rr % c))
    l_bd = jnp.where(strict, a_b[0:2 * c], 0.0)
    aak_bd = jnp.where(strict, a_k[0:2 * c], 0.0)
    arb_bd = jnp.where(incl, a_b[2 * c:4 * c], 0.0)
    ark_bd = jnp.where(incl, a_k[2 * c:4 * c], 0.0)
    v_st = jnp.concatenate([jnp.where(lo, v, zero), jnp.where(lo, zero, v)], axis=0)
    av = _dot(jnp.concatenate([aak_bd, ark_bd], axis=0), v_st)
    q_st = av[0:2 * c]
    arkv_st = av[2 * c:4 * c]

    t_bd = eye + l_bd
    p = _dot(l_bd, l_bd)
    yield
    n = 2
    while 2 * n < c:
        pp_pt = _dot(p, jnp.concatenate([p, t_bd], axis=1))
        yield
        p = pp_pt[:, 0:2 * c]
        t_bd = t_bd + pp_pt[:, 2 * c:4 * c]
        n *= 2
    t_bd = t_bd + _dot(p, t_bd)
    yield

    tq_ta = _dot(t_bd, jnp.concatenate([lhs[0:2 * c], q_st], axis=1))
    yield
    rq_ra = _dot(arb_bd, tq_ta)
    yield
    ta_st = tq_ta[:, 0:PAIR]
    tq_st = tq_ta[:, PAIR:2 * PAIR]
    rh_st = lhs[2 * c:4 * c] + rq_ra[:, 0:PAIR]
    yh_st = rq_ra[:, PAIR:2 * PAIR] + arkv_st
    unstack = lambda x: x[0:c] + x[c:2 * c]
    ta = unstack(ta_st)
    rh = unstack(rh_st)
    tq = unstack(tq_st)
    yh = unstack(yh_st)

    state = state_ref[idx]
    zbuf[...] = _dot_nt(state, jnp.concatenate([ta, rh], axis=0))
    yield
    z = zbuf[...] + jnp.concatenate([tq, yh], axis=0).T
    v_t = jnp.concatenate([v, v], axis=0).T
    lane2 = lax.broadcasted_iota(jnp.int32, (1, 2 * c), 1)
    g_out = jnp.exp(cum_last - cum)
    upd = _dot(jnp.where(lane2 < c, z, v_t), jnp.concatenate([b * g_out, k * g_out], axis=0))
    yield
    state_ref[idx] = state * jnp.exp(cum_last) + jnp.where(same_head, upd, 0.0)
    store_y(z.T[c:2 * c, :])


def _dot_split_lhs_left(m_bf16, x):
    out = None
    for p in _split_bf16(x, 3):
        d = jnp.dot(m_bf16, p, preferred_element_type=F32)
        out = d if out is None else out + d
    return out


def _wkv_kernel(r_ref, lw_ref, k_ref, v_ref, a_ref, b_ref, ltri_ref, eye_ref, y_ref, state_ref, z_ref):
    @pl.when(pl.program_id(0) == 0)
    def _():
        state_ref[...] = jnp.zeros_like(state_ref)

    ltri = ltri_ref[...]
    eye = eye_ref[...]
    pairs = RWKV_DIM // PAIR
    chains = []
    for bi in range(r_ref.shape[0]):
        for p in range(pairs):
            sl = slice(p * PAIR, (p + 1) * PAIR)

            def load(bi=bi, sl=sl):
                return tuple(ref[bi, :, sl] for ref in (r_ref, lw_ref, k_ref, v_ref, a_ref, b_ref))

            def store_y(y, bi=bi, sl=sl):
                y_ref[bi, :, sl] = y

            chain = bi * pairs + p
            chains.append(_wkv_pair(load, state_ref, chain, store_y, z_ref.at[chain], ltri, eye))
    while chains:
        chains = [ch for ch in chains if next(ch, StopIteration) is not StopIteration]


def _wkv(r, lw, k, v, a, b, batch, seq):
    c = WKV_CHUNK
    shp = (batch, seq, RWKV_DIM)
    args = [x.reshape(shp) for x in (r, lw, k, v, a, b)]
    ltri = jnp.tril(jnp.ones((c, c), F32)).astype(BF16)
    eye = jnp.eye(2 * c, dtype=F32)
    blk = pl.BlockSpec((batch, c, RWKV_DIM), lambda ci: (0, ci, 0))
    y = pl.pallas_call(
        _wkv_kernel,
        grid=(seq // c,),
        in_specs=[blk] * 6 + [_const_spec((c, c)), _const_spec((2 * c, 2 * c))],
        out_specs=blk,
        out_shape=jax.ShapeDtypeStruct(shp, F32),
        scratch_shapes=[pltpu.VMEM((batch * (RWKV_DIM // PAIR), PAIR, PAIR), F32)] * 2,
        compiler_params=_params("arbitrary"),
        name="wkv",
    )(*args, ltri, eye)
    return y.reshape(batch * seq, RWKV_DIM)


def _mla_prep_kernel(p_ref, cost_ref, sint_ref, qn_ref, kvn_ref, wqt_ref, wk_ref, wvt_ref,
                     qt_ref, k_ref, vt_ref):
    x = p_ref[...]
    blk = x.shape[0]
    cos_t, sin_t = cost_ref[...], sint_ref[...]
    scale = (QK_NOPE_DIM + QK_ROPE_DIM) ** -0.5 * LOG2_E
    qn = _rms(x[:, 0:Q_LORA_RANK], qn_ref[...]).astype(BF16)
    kvn = _rms(x[:, Q_LORA_RANK:Q_LORA_RANK + KV_LORA_RANK], kvn_ref[...]).astype(BF16)
    k_pe = _rope_t(x[:, Q_LORA_RANK + KV_LORA_RANK:MLA_PAD_COLS].T, cos_t, sin_t).T
    kf = jnp.dot(kvn, wk_ref[...], preferred_element_type=F32)
    for h in range(MLA_HEADS):
        k_ref[0, h, 0] = (kf[:, h * LANES:(h + 1) * LANES] + k_pe).astype(BF16)
    q_t = _dot_nt(wqt_ref[...], qn)
    v_t = _dot_nt(wvt_ref[...], kvn)
    for h in range(MLA_HEADS):
        qt_ref[0, h] = (_rope_t(q_t[h * LANES:(h + 1) * LANES], cos_t, sin_t) * scale).astype(BF16)
    ones = jnp.ones((V_ONES_ROWS, blk), F32)
    for h in range(0, MLA_HEADS, 2):
        rows = lambda g: v_t[g * V_HEAD_DIM:(g + 1) * V_HEAD_DIM]
        vt_ref[0, h // 2, 0] = jnp.concatenate([rows(h), ones, rows(h + 1), ones], axis=0).astype(BF16)


def _mla_prep(p_mla, tables, prm, batch, seq, blk):
    t = p_mla.shape[0]
    n = seq // blk
    tab_t = pl.BlockSpec((QK_ROPE_DIM // 2, blk), lambda i: (0, i))
    consts = [prm["q_norm"], prm["kv_norm"], prm["w_qt"], prm["w_k"], prm["w_vt"]]
    return pl.pallas_call(
        _mla_prep_kernel,
        grid=(t // blk,),
        in_specs=[pl.BlockSpec((blk, MLA_PAD_COLS), lambda i: (i, 0)), tab_t, tab_t]
        + [_const_spec(c.shape) for c in consts],
        out_specs=[pl.BlockSpec((1, MLA_HEADS, LANES, blk), lambda i: (i // n, 0, 0, i % n)),
                   pl.BlockSpec((1, MLA_HEADS, 1, blk, LANES), lambda i: (i // n, 0, i % n, 0, 0)),
                   pl.BlockSpec((1, MLA_HEADS // 2, 1, 2 * V_EXT_ROWS, blk), lambda i: (i // n, 0, i % n, 0, 0))],
        out_shape=[jax.ShapeDtypeStruct((batch, MLA_HEADS, LANES, seq), BF16),
                   jax.ShapeDtypeStruct((batch, MLA_HEADS, n, blk, LANES), BF16),
                   jax.ShapeDtypeStruct((batch, MLA_HEADS // 2, n, 2 * V_EXT_ROWS, blk), BF16)],
        compiler_params=_params("parallel"),
        name="mla_prep",
    )(p_mla, *tables, *consts)


def _attn_kernel(qt_ref, k_ref, vt_ref, o_ref, m_ref, acc_ref, s0_ref, s1_ref):
    n_blk, blk = k_ref.shape[2], k_ref.shape[3]
    units = [(i, j) for i in range(n_blk) for j in range(i + 1)]
    bufs = (s0_ref, s1_ref)

    def scores(unit, s_ref):
        i, j = unit
        for h in range(2):
            s_ref[h] = jnp.dot(k_ref[0, h, j], qt_ref[0, h, :, i * blk:(i + 1) * blk],
                               preferred_element_type=F32)

    def update(unit, s_ref):
        i, j = unit
        if j == 0:
            m_ref[...] = jnp.full_like(m_ref, -jnp.inf)
            acc_ref[...] = jnp.zeros_like(acc_ref)
        for h in range(2):
            vt = vt_ref[0, 0, j, h * V_EXT_ROWS:(h + 1) * V_EXT_ROWS, :]
            for c0 in range(0, blk, ATTN_Q_PIECE):
                cols = slice(c0, c0 + ATTN_Q_PIECE)
                rows = c0 + ATTN_Q_PIECE if j == i else blk
                s = s_ref[h, 0:rows, cols]
                if j == i:
                    k_pos = lax.broadcasted_iota(jnp.int32, s.shape, 0)
                    q_pos = lax.broadcasted_iota(jnp.int32, s.shape, 1) + c0
                    s = jnp.where(k_pos <= q_pos, s, NEG_INF)
                m_old = m_ref[h, :, cols]
                m_new = jnp.maximum(m_old, jnp.max(s, axis=0, keepdims=True))
                pv = jnp.dot(vt[:, 0:rows], jnp.exp2(s - m_new).astype(BF16), preferred_element_type=F32)
                acc_ref[h, :, cols] = jnp.exp2(m_old - m_new) * acc_ref[h, :, cols] + pv
                m_ref[h, :, cols] = m_new
        if j == i:
            o_t = jnp.concatenate([acc_ref[h, 0:V_HEAD_DIM] / acc_ref[h, V_HEAD_DIM:V_HEAD_DIM + 1]
                                   for h in range(2)], axis=0)
            o_ref[0, i * blk:(i + 1) * blk, :] = o_t.T.astype(BF16)

    scores(units[0], bufs[0])
    for n, unit in enumerate(units):
        if n + 1 < len(units):
            scores(units[n + 1], bufs[(n + 1) % 2])
        update(unit, bufs[n % 2])


def _mla_attn(qt, k, vt, batch, seq, blk):
    n = seq // blk
    hv = MLA_HEADS * V_HEAD_DIM
    o = pl.pallas_call(
        _attn_kernel,
        grid=(batch, MLA_HEADS // 2),
        in_specs=[pl.BlockSpec((1, 2, LANES, seq), lambda b, p: (b, p, 0, 0)),
                  pl.BlockSpec((1, 2, n, blk, LANES), lambda b, p: (b, p, 0, 0, 0)),
                  pl.BlockSpec((1, 1, n, 2 * V_EXT_ROWS, blk), lambda b, p: (b, p, 0, 0, 0))],
        out_specs=pl.BlockSpec((1, seq, LANES), lambda b, p: (b, 0, p)),
        out_shape=jax.ShapeDtypeStruct((batch, seq, hv), BF16),
        scratch_shapes=[pltpu.VMEM((2, 1, blk), F32), pltpu.VMEM((2, V_EXT_ROWS, blk), F32),
                        pltpu.VMEM((2, blk, blk), F32), pltpu.VMEM((2, blk, blk), F32)],
        compiler_params=_params("parallel", "parallel"),
        name="mla_attn",
    )(qt, k, vt)
    return o.reshape(batch * seq, hv)


def _merge_kernel(y_ref, bonus_ref, gate_ref, o_ref, g_ref, h_ref, lnw_ref, lnb_ref, ones_ref,
                  woa_ref, wob_ref, wout_ref, out_ref):
    ones = ones_ref[...]
    inv_n = 1.0 / RWKV_HEAD_DIM
    y = y_ref[...]
    mu = _head_sums(y, ones) * inv_n
    d = y - mu
    var = _head_sums(d * d, ones) * inv_n
    yn = (d * lax.rsqrt(var + LNX_EPS)) * lnw_ref[...] + lnb_ref[...]
    z = (yn + bonus_ref[...]) * gate_ref[...]
    y_a = _dot(z, woa_ref[...])
    y_b = jnp.dot(o_ref[...], wob_ref[...], preferred_element_type=F32)
    g = g_ref[...].astype(F32)
    merged = jax.nn.sigmoid(g[:, 0:D_MODEL]) * y_a + jax.nn.sigmoid(g[:, D_MODEL:2 * D_MODEL]) * y_b
    out_ref[...] = h_ref[...] + _dot(merged, wout_ref[...])


def _merge(y, bonus, gate, o_mla, gates, h, prm, tm):
    t = h.shape[0]
    row = lambda c: pl.BlockSpec((tm, c), lambda i: (i, 0))
    consts = [prm["lnx_w"], prm["lnx_b"], prm["head_ones"], prm["w_oa"], prm["w_ob"], prm["w_out"]]
    return pl.pallas_call(
        _merge_kernel,
        grid=(t // tm,),
        in_specs=[row(RWKV_DIM)] * 4 + [row(2 * D_MODEL), row(D_MODEL)]
        + [_const_spec(c.shape) for c in consts],
        out_specs=row(D_MODEL),
        out_shape=jax.ShapeDtypeStruct((t, D_MODEL), F32),
        compiler_params=_params("parallel"),
        name="merge",
    )(y, bonus, gate, o_mla, gates, h, *consts)


def _ffn_kernel(blocks_per_seq, final, h_ref, g_ref, wg_ref, wv_ref, cw_ref, cb_ref, wd_ref, fg_ref,
                out_ref, n_ref, acc_ref, carry_ref):
    i = pl.program_id(0)
    f = pl.program_id(1)
    tm = h_ref.shape[0]
    rows = min(FFN_SUB_ROWS, tm)

    @pl.when(f == 0)
    def _():
        n_ref[...] = _rms(h_ref[...], g_ref[...]).astype(BF16)
        acc_ref[...] = jnp.zeros_like(acc_ref)

    def up(sb):
        n = n_ref[sb * rows:(sb + 1) * rows, :]
        return (jnp.dot(n, wg_ref[...], preferred_element_type=F32),
                jnp.dot(n, wv_ref[...], preferred_element_type=F32))

    tail = carry_ref[f]
    tail = jnp.where(i % blocks_per_seq == 0, jnp.zeros_like(tail), tail)
    cw = cw_ref[0]
    row = lax.broadcasted_iota(jnp.int32, (rows, 1), 0)
    sqrt_half = np.sqrt(0.5).astype(np.float32)
    nxt = up(0)
    for sb in range(tm // rows):
        u_gate, u_val = nxt
        if sb + 1 < tm // rows:
            nxt = up(sb + 1)
        back1 = jnp.where(row == 0, tail[SUBLANES - 1:SUBLANES], pltpu.roll(u_gate, 1, axis=0))
        back2 = jnp.where(row == 0, tail[SUBLANES - 2:SUBLANES - 1],
                          jnp.where(row == 1, tail[SUBLANES - 1:SUBLANES], pltpu.roll(u_gate, 2, axis=0)))
        c = cb_ref[0] + cw[0:1] * back2
        c = c + cw[1:2] * back1
        c = c + cw[2:3] * u_gate
        tail = u_gate[rows - SUBLANES:rows]
        act = 0.5 * c * (1.0 + lax.erf(c * sqrt_half))
        down = jnp.dot((act * u_val).astype(BF16), wd_ref[...], preferred_element_type=F32)
        acc_ref[sb * rows:(sb + 1) * rows, :] += down
    carry_ref[f] = tail

    @pl.when(f == pl.num_programs(1) - 1)
    def _():
        out = h_ref[...] + acc_ref[...]
        out_ref[...] = _rms(out, fg_ref[...]) if final else out


def _conv_ffn(h, prm, final_gain, seq, tm, fc):
    t = h.shape[0]
    nf = FFN_DIM // fc
    row = pl.BlockSpec((tm, D_MODEL), lambda i, f: (i, 0))
    final = final_gain is not None
    fg = final_gain.reshape(1, D_MODEL) if final else prm["ffn_norm"]
    return pl.pallas_call(
        functools.partial(_ffn_kernel, seq // tm, final),
        grid=(t // tm, nf),
        in_specs=[row, _const_spec((1, D_MODEL)),
                  pl.BlockSpec((D_MODEL, fc), lambda i, f: (0, f)),
                  pl.BlockSpec((D_MODEL, fc), lambda i, f: (0, nf + f)),
                  pl.BlockSpec((1, SUBLANES, fc), lambda i, f: (f, 0, 0)),
                  pl.BlockSpec((1, 1, fc), lambda i, f: (f, 0, 0)),
                  pl.BlockSpec((fc, D_MODEL), lambda i, f: (f, 0)),
                  _const_spec((1, D_MODEL))],
        out_specs=row,
        out_shape=jax.ShapeDtypeStruct((t, D_MODEL), F32),
        scratch_shapes=[pltpu.VMEM((tm, D_MODEL), BF16), pltpu.VMEM((tm, D_MODEL), F32),
                        pltpu.VMEM((nf, SUBLANES, fc), F32)],
        compiler_params=_params("arbitrary", "arbitrary"),
        name="conv_ffn",
    )(h, prm["ffn_norm"], prm["w_up"], prm["w_up"], prm["conv_w"], prm["conv_b"], prm["w_down"], fg)


def _layer_params(l, fc, attn_norm, w_in, mu_shift, decay_base, w_decay_up, iclr_base, w_iclr_up,
                  w_gate_up, k_k, k_a, r_k, lnx_w, lnx_b, w_out_rwkv, q_norm, w_q_up, kv_norm,
                  w_kv_up, w_out_mla, w_out, ffn_norm, w_ffn_up, conv_w, conv_b, w_ffn_down):
    row = lambda x: x.reshape(1, -1)
    w = w_in[l]
    w_mla_src = w[:, SHIFT_COLS:SHIFT_COLS + MLA_COLS]
    lat = Q_LORA_RANK + KV_LORA_RANK
    w_mla = jnp.zeros((D_MODEL, MLA_PAD_COLS), F32)
    w_mla = w_mla.at[:, 0:lat].set(w_mla_src[:, 0:lat])
    w_mla = w_mla.at[:, lat + ROPE_LANE0:lat + ROPE_LANE0 + QK_ROPE_DIM].set(w_mla_src[:, lat:])
    w_lora = jnp.zeros((DECAY_LORA + ICLR_LORA, 2 * RWKV_DIM), F32)
    w_lora = w_lora.at[0:DECAY_LORA, 0:RWKV_DIM].set(w_decay_up[l])
    w_lora = w_lora.at[DECAY_LORA:, RWKV_DIM:].set(w_iclr_up[l])
    head = jnp.arange(MXU_WIDTH) // RWKV_HEAD_DIM
    head_ones = (head[:, None] == head[None, :]).astype(BF16)
    qk = QK_NOPE_DIM + QK_ROPE_DIM
    w_q = jnp.pad(w_q_up[l].reshape(Q_LORA_RANK, MLA_HEADS, qk), ((0, 0), (0, 0), (0, LANES - qk)))
    w_kv = w_kv_up[l].reshape(KV_LORA_RANK, MLA_HEADS, QK_NOPE_DIM + V_HEAD_DIM)
    w_k = jnp.pad(w_kv[:, :, :QK_NOPE_DIM], ((0, 0), (0, 0), (0, LANES - QK_NOPE_DIM)))
    nf = FFN_DIM // fc
    return dict(
        attn_norm=row(attn_norm[l]),
        w_rw=w[:, :SHIFT_COLS].astype(BF16),
        w_mla=w_mla.astype(BF16),
        w_g=w[:, SHIFT_COLS + MLA_COLS:].astype(BF16),
        mu=row(mu_shift[l]), w_lora=w_lora.astype(BF16), w_gate_up=w_gate_up[l].astype(BF16),
        decay_base=row(decay_base[l]), iclr_base=row(iclr_base[l]), k_k=row(k_k[l]), k_a=row(k_a[l]),
        r_k=row(r_k[l]), head_ones=head_ones, lnx_w=row(lnx_w[l]), lnx_b=row(lnx_b[l]),
        q_norm=row(q_norm[l]), kv_norm=row(kv_norm[l]),
        w_qt=w_q.reshape(Q_LORA_RANK, MLA_HEADS * LANES).T.astype(BF16),
        w_k=w_k.reshape(KV_LORA_RANK, MLA_HEADS * LANES).astype(BF16),
        w_vt=w_kv[:, :, QK_NOPE_DIM:].reshape(KV_LORA_RANK, MLA_HEADS * V_HEAD_DIM).T.astype(BF16),
        w_oa=w_out_rwkv[l].astype(BF16), w_ob=w_out_mla[l].astype(BF16), w_out=w_out[l].astype(BF16),
        ffn_norm=row(ffn_norm[l]), w_up=w_ffn_up[l].astype(BF16), w_down=w_ffn_down[l].astype(BF16),
        conv_w=jnp.pad(conv_w[l].reshape(3, nf, fc).transpose(1, 0, 2), ((0, 0), (0, SUBLANES - 3), (0, 0))),
        conv_b=conv_b[l].reshape(nf, 1, fc),
    )


def _tile_plan(seq):
    pick = lambda want: min(want, seq)
    return dict(proj=pick(512), attn=pick(512), merge=pick(512), ffn=pick(1024),
                ffn_cols=FFN_DIM // 2)


def kernel(x, positions, attn_norm, w_in, mu_shift, decay_base, w_decay_up, iclr_base, w_iclr_up, w_gate_up, k_k, k_a, r_k, lnx_w, lnx_b, w_out_rwkv, q_norm, w_q_up, kv_norm, w_kv_up, w_out_mla, w_out, ffn_norm, w_ffn_up, conv_w, conv_b, w_ffn_down, final_norm):
    batch, seq, _ = x.shape
    depth = w_in.shape[0]
    assert seq % WKV_CHUNK == 0 and x.shape[2] == D_MODEL
    plan = _tile_plan(seq)
    tables = _rope_tables(positions)
    h = x.reshape(batch * seq, D_MODEL)
    for l in range(depth):
        prm = _layer_params(l, plan["ffn_cols"], attn_norm, w_in, mu_shift, decay_base, w_decay_up,
                            iclr_base, w_iclr_up, w_gate_up, k_k, k_a, r_k, lnx_w, lnx_b, w_out_rwkv,
                            q_norm, w_q_up, kv_norm, w_kv_up, w_out_mla, w_out, ffn_norm, w_ffn_up,
                            conv_w, conv_b, w_ffn_down)
        p_mla, gates, r, lw, k, v, a, b, gate, bonus = _inproj(h, prm, seq, plan["proj"])
        y = _wkv(r, lw, k, v, a, b, batch, seq)
        q_h, k_h, v_h = _mla_prep(p_mla, tables, prm, batch, seq, plan["attn"])
        o_mla = _mla_attn(q_h, k_h, v_h, batch, seq, plan["attn"])
        h = _merge(y, bonus, gate, o_mla, gates, h, prm, plan["merge"])
        last = final_norm if l == depth - 1 else None
        h = _conv_ffn(h, prm, last, seq, plan["ffn"], plan["ffn_cols"])
    return h.reshape(batch, seq, D_MODEL)
```

```python
import functools

import jax
import jax.numpy as jnp
import numpy as np
from jax import lax
from jax.experimental import pallas as pl
from jax.experimental.pallas import tpu as pltpu

F32 = jnp.float32
BF16 = jnp.bfloat16

D_MODEL = 1024
RWKV_HEADS = 8
RWKV_HEAD_DIM = 64
RWKV_DIM = RWKV_HEADS * RWKV_HEAD_DIM
DECAY_LORA = 64
ICLR_LORA = 64
GATE_LORA = 128
LNX_EPS = 64e-5
MLA_HEADS = 8
QK_NOPE_DIM = 64
QK_ROPE_DIM = 32
V_HEAD_DIM = 64
Q_LORA_RANK = 256
KV_LORA_RANK = 128
ROPE_THETA = 10000.0
NEG_INF = -1e30
FFN_DIM = 2816
NORM_EPS = 1e-6
SHIFT_COLS = 3 * RWKV_DIM + DECAY_LORA + ICLR_LORA + GATE_LORA
MLA_COLS = Q_LORA_RANK + KV_LORA_RANK + QK_ROPE_DIM

LANES = 128
SUBLANES = 8
MXU_WIDTH = 256
VMEM_LIMIT_BYTES = 56 * 1024 * 1024

MLA_PAD_COLS = 4 * LANES
ROPE_LANE0 = QK_NOPE_DIM
WKV_CHUNK = 64
PAIR = 2 * RWKV_HEAD_DIM
LOG2_E = float(np.log2(np.e))
V_ONES_ROWS = 16
V_EXT_ROWS = V_HEAD_DIM + V_ONES_ROWS
ATTN_Q_PIECE = 256
FFN_SUB_ROWS = 256


def _params(*semantics, flags=None):
    return pltpu.CompilerParams(dimension_semantics=semantics, vmem_limit_bytes=VMEM_LIMIT_BYTES,
                                flags=flags)


def _dot(a, b):
    return jnp.dot(a.astype(BF16), b.astype(BF16), preferred_element_type=F32)


def _dot_nt(a, b):
    return lax.dot_general(a.astype(BF16), b.astype(BF16), (((1,), (1,)), ((), ())),
                           preferred_element_type=F32)


def _split_bf16(x, terms):
    parts = []
    rest = x
    for _ in range(terms):
        p = rest.astype(BF16)
        parts.append(p)
        rest = rest - p.astype(F32)
    return parts


def _dot_split_lhs(a, b_bf16, terms):
    out = None
    for p in _split_bf16(a, terms):
        d = jnp.dot(p, b_bf16, preferred_element_type=F32)
        out = d if out is None else out + d
    return out


def _head_sums(x, ones):
    w = ones.shape[0]
    return jnp.concatenate([_dot_split_lhs(x[:, c:c + w], ones, 2) for c in range(0, x.shape[1], w)],
                           axis=1)


def _rms(x, g):
    ms = jnp.mean(x * x, axis=-1, keepdims=True)
    return (x * lax.rsqrt(ms + NORM_EPS)) * g


def _const_spec(shape):
    return pl.BlockSpec(shape, lambda *_: (0,) * len(shape))


def _rope_tables_kernel(pos_ref, invf_ref, cos_ref, sin_ref):
    ang = pos_ref[...] * invf_ref[...]
    cos_ref[...] = jnp.cos(ang)
    sin_ref[...] = jnp.sin(ang)


def _rope_tables(positions):
    t = positions.size
    half = QK_ROPE_DIM // 2
    per_row = LANES // half
    inv_freq = jnp.power(ROPE_THETA, -jnp.arange(0, QK_ROPE_DIM, 2, dtype=F32) / QK_ROPE_DIM)
    invf = jnp.tile(inv_freq, per_row).reshape(1, LANES)
    posc = jnp.repeat(positions.reshape(t // per_row, per_row).astype(F32), half, axis=1)
    spec = pl.BlockSpec(posc.shape, lambda i: (0, 0))
    out = jax.ShapeDtypeStruct(posc.shape, F32)
    cos, sin = pl.pallas_call(
        _rope_tables_kernel,
        grid=(1,),
        in_specs=[spec, _const_spec((1, LANES))],
        out_specs=[spec, spec],
        out_shape=[out, out],
        compiler_params=_params("arbitrary"),
        name="rope_tables",
    )(posc, invf)
    return cos.reshape(t, half).T, sin.reshape(t, half).T


def _rope_t(tile, cos_t, sin_t):
    half = QK_ROPE_DIM // 2
    r0 = ROPE_LANE0
    x1 = tile[r0:r0 + half]
    x2 = tile[r0 + half:r0 + QK_ROPE_DIM]
    return jnp.concatenate([tile[0:r0], x1 * cos_t - x2 * sin_t, x1 * sin_t + x2 * cos_t,
                            tile[r0 + QK_ROPE_DIM:]], axis=0)


def _inproj_kernel(blocks_per_seq, x_ref, g_ref, wrw_ref, wmla_ref, wgate_ref, mu_ref, wl_ref,
                   wg_ref, dbase_ref, ibase_ref, kk_ref, ka_ref, rk_ref, ones_ref,
                   pmla_ref, pg_ref, r_ref, lw_ref, k_ref, v_ref, a_ref, b_ref, gate_ref, bonus_ref,
                   tail_ref):
    i = pl.program_id(0)
    n = _rms(x_ref[...], g_ref[...]).astype(BF16)
    x = jnp.dot(n, wrw_ref[...], preferred_element_type=F32)
    pmla_ref[...] = jnp.dot(n, wmla_ref[...], preferred_element_type=F32)
    pg_ref[...] = jnp.dot(n, wgate_ref[...], preferred_element_type=F32).astype(BF16)
    tm = x.shape[0]
    prev = tail_ref[SUBLANES - 1:SUBLANES, :]
    prev = jnp.where(i % blocks_per_seq == 0, jnp.zeros_like(prev), prev)
    tail_ref[...] = x[tm - SUBLANES:tm]
    row = lax.broadcasted_iota(jnp.int32, (tm, 1), 0)
    shifted = jnp.where(row == 0, prev, pltpu.roll(x, 1, axis=0))
    xs = x + (shifted - x) * mu_ref[...]

    c0 = 3 * RWKV_DIM
    p_r = xs[:, 0:RWKV_DIM]
    p_k = xs[:, RWKV_DIM:2 * RWKV_DIM]
    p_v = xs[:, 2 * RWKV_DIM:c0]
    p_wa = xs[:, c0:c0 + DECAY_LORA + ICLR_LORA]
    p_g = xs[:, c0 + DECAY_LORA + ICLR_LORA:SHIFT_COLS]

    lane = lax.broadcasted_iota(jnp.int32, p_wa.shape, 1)
    lora_in = jnp.where(lane < DECAY_LORA, jnp.tanh(p_wa), p_wa)
    lora = _dot(lora_in, wl_ref[...])
    log_w = -jax.nn.softplus(-(dbase_ref[...] + lora[:, 0:RWKV_DIM])) - 0.5
    iclr = jax.nn.sigmoid(ibase_ref[...] + lora[:, RWKV_DIM:2 * RWKV_DIM])
    gate = _dot(jax.nn.sigmoid(p_g), wg_ref[...])

    ones = ones_ref[...]
    kk = p_k * kk_ref[...]
    kk = kk * lax.rsqrt(_head_sums(kk * kk, ones) + 1e-12)
    k = p_k * (1.0 + (iclr - 1.0) * ka_ref[...])

    r_ref[...] = p_r
    lw_ref[...] = -jnp.exp(log_w)
    k_ref[...] = k
    v_ref[...] = p_v
    a_ref[...] = -kk
    b_ref[...] = kk * iclr
    gate_ref[...] = gate
    bonus_ref[...] = _head_sums(p_r * k * rk_ref[...], ones) * p_v


def _inproj(h, prm, seq, tm):
    t = h.shape[0]
    row = lambda c: pl.BlockSpec((tm, c), lambda i: (i, 0))
    out = lambda c, dt: jax.ShapeDtypeStruct((t, c), dt)
    consts = [prm["attn_norm"], prm["w_rw"], prm["w_mla"], prm["w_g"], prm["mu"], prm["w_lora"],
              prm["w_gate_up"], prm["decay_base"], prm["iclr_base"], prm["k_k"], prm["k_a"], prm["r_k"],
              prm["head_ones"]]
    return pl.pallas_call(
        functools.partial(_inproj_kernel, seq // tm),
        grid=(t // tm,),
        in_specs=[row(D_MODEL)] + [_const_spec(c.shape) for c in consts],
        out_specs=[row(MLA_PAD_COLS), row(2 * D_MODEL)] + [row(RWKV_DIM)] * 8,
        out_shape=[out(MLA_PAD_COLS, F32), out(2 * D_MODEL, BF16)] + [out(RWKV_DIM, F32)] * 8,
        scratch_shapes=[pltpu.VMEM((SUBLANES, SHIFT_COLS), F32)],
        compiler_params=_params("arbitrary"),
        name="inproj",
    )(h, *consts)


def _wkv_pair(load, state_ref, idx, store_y, zbuf, ltri, eye):
    c = WKV_CHUNK
    lane = lax.broadcasted_iota(jnp.int32, (1, PAIR), 1)
    lo = lane < RWKV_HEAD_DIM
    r, lw, k, v, a, b = load()

    cum = _dot_split_lhs_left(ltri, lw)
    yield
    cum_last = cum[c - 1:c, :]
    g_inv = jnp.exp(-cum)
    a_t = a * jnp.exp(cum - lw)
    r_t = r * jnp.exp(cum)
    zero = jnp.zeros_like(a_t)
    lhs = jnp.concatenate([jnp.where(lo, a_t, zero), jnp.where(lo, zero, a_t),
                           jnp.where(lo, r_t, zero), jnp.where(lo, zero, r_t)], axis=0)
    b_t = b * g_inv
    k_t = k * g_inv
    a_all = _dot_nt(lhs, jnp.concatenate([b_t, k_t], axis=0))
    yield
    swapped = pltpu.roll(a_all, c, axis=1)
    head0_rows = (lax.broadcasted_iota(jnp.int32, (4 * c, 1), 0) // c) % 2 == 0
    a_b = jnp.where(head0_rows, a_all, swapped)
    a_k = jnp.where(head0_rows, swapped, a_all)

    rr = lax.broadcasted_iota(jnp.int32, (2 * c, 2 * c), 0)
    cc = lax.broadcasted_iota(jnp.int32, (2 * c, 2 * c), 1)
    same_head = (rr // c) == (cc // c)
    strict = same_head & ((cc % c) < (rr % c))
    incl = same_head & ((cc % c) <= (rr % c))
    l_bd = jnp.where(strict, a_b[0:2 * c], 0.0)
    aak_bd = jnp.where(strict, a_k[0:2 * c], 0.0)
    arb_bd = jnp.where(incl, a_b[2 * c:4 * c], 0.0)
    ark_bd = jnp.where(incl, a_k[2 * c:4 * c], 0.0)
    v_st = jnp.concatenate([jnp.where(lo, v, zero), jnp.where(lo, zero, v)], axis=0)
    av = _dot(jnp.concatenate([aak_bd, ark_bd], axis=0), v_st)
    q_st = av[0:2 * c]
    arkv_st = av[2 * c:4 * c]

    t_bd = eye + l_bd
    p = _dot(l_bd, l_bd)
    yield
    n = 2
    while 2 * n < c:
        pp_pt = _dot(p, jnp.concatenate([p, t_bd], axis=1))
        yield
        p = pp_pt[:, 0:2 * c]
        t_bd = t_bd + pp_pt[:, 2 * c:4 * c]
        n *= 2
    t_bd = t_bd + _dot(p, t_bd)
    yield

    tq_ta = _dot(t_bd, jnp.concatenate([lhs[0:2 * c], q_st], axis=1))
    yield
    rq_ra = _dot(arb_bd, tq_ta)
    yield
    ta_st = tq_ta[:, 0:PAIR]
    tq_st = tq_ta[:, PAIR:2 * PAIR]
    rh_st = lhs[2 * c:4 * c] + rq_ra[:, 0:PAIR]
    yh_st = rq_ra[:, PAIR:2 * PAIR] + arkv_st
    unstack = lambda x: x[0:c] + x[c:2 * c]
    ta = unstack(ta_st)
    rh = unstack(rh_st)
    tq = unstack(tq_st)
    yh = unstack(yh_st)

    state = state_ref[idx]
    zbuf[...] = _dot_nt(state, jnp.concatenate([ta, rh], axis=0))
    yield
    z = zbuf[...] + jnp.concatenate([tq, yh], axis=0).T
    v_t = jnp.concatenate([v, v], axis=0).T
    lane2 = lax.broadcasted_iota(jnp.int32, (1, 2 * c), 1)
    g_out = jnp.exp(cum_last - cum)
    upd = _dot(jnp.where(lane2 < c, z, v_t), jnp.concatenate([b * g_out, k * g_out], axis=0))
    yield
    state_ref[idx] = state * jnp.exp(cum_last) + jnp.where(same_head, upd, 0.0)
    store_y(z.T[c:2 * c, :])


def _dot_split_lhs_left(m_bf16, x):
    out = None
    for p in _split_bf16(x, 3):
        d = jnp.dot(m_bf16, p, preferred_element_type=F32)
        out = d if out is None else out + d
    return out


def _wkv_kernel(r_ref, lw_ref, k_ref, v_ref, a_ref, b_ref, ltri_ref, eye_ref, y_ref, state_ref, z_ref):
    @pl.when(pl.program_id(0) == 0)
    def _():
        state_ref[...] = jnp.zeros_like(state_ref)

    ltri = ltri_ref[...]
    eye = eye_ref[...]
    pairs = RWKV_DIM // PAIR
    chains = []
    for bi in range(r_ref.shape[0]):
        for p in range(pairs):
            sl = slice(p * PAIR, (p + 1) * PAIR)

            def load(bi=bi, sl=sl):
                return tuple(ref[bi, :, sl] for ref in (r_ref, lw_ref, k_ref, v_ref, a_ref, b_ref))

            def store_y(y, bi=bi, sl=sl):
                y_ref[bi, :, sl] = y

            chain = bi * pairs + p
            chains.append(_wkv_pair(load, state_ref, chain, store_y, z_ref.at[chain], ltri, eye))
    while chains:
        chains = [ch for ch in chains if next(ch, StopIteration) is not StopIteration]


def _wkv(r, lw, k, v, a, b, batch, seq):
    c = WKV_CHUNK
    shp = (batch, seq, RWKV_DIM)
    args = [x.reshape(shp) for x in (r, lw, k, v, a, b)]
    ltri = jnp.tril(jnp.ones((c, c), F32)).astype(BF16)
    eye = jnp.eye(2 * c, dtype=F32)
    blk = pl.BlockSpec((batch, c, RWKV_DIM), lambda ci: (0, ci, 0))
    y = pl.pallas_call(
        _wkv_kernel,
        grid=(seq // c,),
        in_specs=[blk] * 6 + [_const_spec((c, c)), _const_spec((2 * c, 2 * c))],
        out_specs=blk,
        out_shape=jax.ShapeDtypeStruct(shp, F32),
        scratch_shapes=[pltpu.VMEM((batch * (RWKV_DIM // PAIR), PAIR, PAIR), F32)] * 2,
        compiler_params=_params("arbitrary"),
        name="wkv",
    )(*args, ltri, eye)
    return y.reshape(batch * seq, RWKV_DIM)


def _mla_prep_kernel(p_ref, cost_ref, sint_ref, qn_ref, kvn_ref, wqt_ref, wk_ref, wvt_ref,
                     qt_ref, k_ref, vt_ref):
    x = p_ref[...]
    blk = x.shape[0]
    cos_t, sin_t = cost_ref[...], sint_ref[...]
    scale = (QK_NOPE_DIM + QK_ROPE_DIM) ** -0.5 * LOG2_E
    qn = _rms(x[:, 0:Q_LORA_RANK], qn_ref[...]).astype(BF16)
    kvn = _rms(x[:, Q_LORA_RANK:Q_LORA_RANK + KV_LORA_RANK], kvn_ref[...]).astype(BF16)
    k_pe = _rope_t(x[:, Q_LORA_RANK + KV_LORA_RANK:MLA_PAD_COLS].T, cos_t, sin_t).T
    kf = jnp.dot(kvn, wk_ref[...], preferred_element_type=F32)
    for h in range(MLA_HEADS):
        k_ref[0, h, 0] = (kf[:, h * LANES:(h + 1) * LANES] + k_pe).astype(BF16)
    q_t = _dot_nt(wqt_ref[...], qn)
    v_t = _dot_nt(wvt_ref[...], kvn)
    for h in range(MLA_HEADS):
        qt_ref[0, h] = (_rope_t(q_t[h * LANES:(h + 1) * LANES], cos_t, sin_t) * scale).astype(BF16)
    ones = jnp.ones((V_ONES_ROWS, blk), F32)
    for h in range(0, MLA_HEADS, 2):
        rows = lambda g: v_t[g * V_HEAD_DIM:(g + 1) * V_HEAD_DIM]
        vt_ref[0, h // 2, 0] = jnp.concatenate([rows(h), ones, rows(h + 1), ones], axis=0).astype(BF16)


def _mla_prep(p_mla, tables, prm, batch, seq, blk):
    t = p_mla.shape[0]
    n = seq // blk
    tab_t = pl.BlockSpec((QK_ROPE_DIM // 2, blk), lambda i: (0, i))
    consts = [prm["q_norm"], prm["kv_norm"], prm["w_qt"], prm["w_k"], prm["w_vt"]]
    return pl.pallas_call(
        _mla_prep_kernel,
        grid=(t // blk,),
        in_specs=[pl.BlockSpec((blk, MLA_PAD_COLS), lambda i: (i, 0)), tab_t, tab_t]
        + [_const_spec(c.shape) for c in consts],
        out_specs=[pl.BlockSpec((1, MLA_HEADS, LANES, blk), lambda i: (i // n, 0, 0, i % n)),
                   pl.BlockSpec((1, MLA_HEADS, 1, blk, LANES), lambda i: (i // n, 0, i % n, 0, 0)),
                   pl.BlockSpec((1, MLA_HEADS // 2, 1, 2 * V_EXT_ROWS, blk), lambda i: (i // n, 0, i % n, 0, 0))],
        out_shape=[jax.ShapeDtypeStruct((batch, MLA_HEADS, LANES, seq), BF16),
                   jax.ShapeDtypeStruct((batch, MLA_HEADS, n, blk, LANES), BF16),
                   jax.ShapeDtypeStruct((batch, MLA_HEADS // 2, n, 2 * V_EXT_ROWS, blk), BF16)],
        compiler_params=_params("parallel"),
        name="mla_prep",
    )(p_mla, *tables, *consts)


def _attn_kernel(qt_ref, k_ref, vt_ref, o_ref, m_ref, acc_ref, s0_ref, s1_ref):
    n_blk, blk = k_ref.shape[2], k_ref.shape[3]
    units = [(i, j) for i in range(n_blk) for j in range(i + 1)]
    bufs = (s0_ref, s1_ref)

    def scores(unit, s_ref, h):
        i, j = unit
        s_ref[h] = jnp.dot(k_ref[0, h, j], qt_ref[0, h, :, i * blk:(i + 1) * blk],
                           preferred_element_type=F32)

    def update(unit, s_ref, nxt, nxt_ref):
        i, j = unit
        if j == 0:
            m_ref[...] = jnp.full_like(m_ref, -jnp.inf)
            acc_ref[...] = jnp.zeros_like(acc_ref)
        for h in range(2):
            if nxt is not None:
                scores(nxt, nxt_ref, h)
            vt = vt_ref[0, 0, j, h * V_EXT_ROWS:(h + 1) * V_EXT_ROWS, :]
            for c0 in range(0, blk, ATTN_Q_PIECE):
                cols = slice(c0, c0 + ATTN_Q_PIECE)
                rows = c0 + ATTN_Q_PIECE if j == i else blk
                s = s_ref[h, 0:rows, cols]
                if j == i:
                    k_pos = lax.broadcasted_iota(jnp.int32, s.shape, 0)
                    q_pos = lax.broadcasted_iota(jnp.int32, s.shape, 1) + c0
                    s = jnp.where(k_pos <= q_pos, s, NEG_INF)
                m_old = m_ref[h, :, cols]
                m_new = jnp.maximum(m_old, jnp.max(s, axis=0, keepdims=True))
                pv = jnp.dot(vt[:, 0:rows], jnp.exp2(s - m_new).astype(BF16), preferred_element_type=F32)
                acc_ref[h, :, cols] = jnp.exp2(m_old - m_new) * acc_ref[h, :, cols] + pv
                m_ref[h, :, cols] = m_new
        if j == i:
            o_t = jnp.concatenate([acc_ref[h, 0:V_HEAD_DIM] / acc_ref[h, V_HEAD_DIM:V_HEAD_DIM + 1]
                                   for h in range(2)], axis=0)
            o_ref[0, i * blk:(i + 1) * blk, :] = o_t.T.astype(BF16)

    for h in range(2):
        scores(units[0], bufs[0], h)
    for n, unit in enumerate(units):
        nxt = units[n + 1] if n + 1 < len(units) else None
        update(unit, bufs[n % 2], nxt, bufs[(n + 1) % 2])


def _mla_attn(qt, k, vt, batch, seq, blk):
    n = seq // blk
    hv = MLA_HEADS * V_HEAD_DIM
    o = pl.pallas_call(
        _attn_kernel,
        grid=(batch, MLA_HEADS // 2),
        in_specs=[pl.BlockSpec((1, 2, LANES, seq), lambda b, p: (b, p, 0, 0)),
                  pl.BlockSpec((1, 2, n, blk, LANES), lambda b, p: (b, p, 0, 0, 0)),
                  pl.BlockSpec((1, 1, n, 2 * V_EXT_ROWS, blk), lambda b, p: (b, p, 0, 0, 0))],
        out_specs=pl.BlockSpec((1, seq, LANES), lambda b, p: (b, 0, p)),
        out_shape=jax.ShapeDtypeStruct((batch, seq, hv), BF16),
        scratch_shapes=[pltpu.VMEM((2, 1, blk), F32), pltpu.VMEM((2, V_EXT_ROWS, blk), F32),
                        pltpu.VMEM((2, blk, blk), F32), pltpu.VMEM((2, blk, blk), F32)],
        compiler_params=_params("parallel", "parallel"),
        name="mla_attn",
    )(qt, k, vt)
    return o.reshape(batch * seq, hv)


def _merge_kernel(y_ref, bonus_ref, gate_ref, o_ref, g_ref, h_ref, lnw_ref, lnb_ref, ones_ref,
                  woa_ref, wob_ref, wout_ref, out_ref):
    ones = ones_ref[...]
    inv_n = 1.0 / RWKV_HEAD_DIM
    y = y_ref[...]
    mu = _head_sums(y, ones) * inv_n
    d = y - mu
    var = _head_sums(d * d, ones) * inv_n
    yn = (d * lax.rsqrt(var + LNX_EPS)) * lnw_ref[...] + lnb_ref[...]
    z = (yn + bonus_ref[...]) * gate_ref[...]
    y_a = _dot(z, woa_ref[...])
    y_b = jnp.dot(o_ref[...], wob_ref[...], preferred_element_type=F32)
    g = g_ref[...].astype(F32)
    merged = jax.nn.sigmoid(g[:, 0:D_MODEL]) * y_a + jax.nn.sigmoid(g[:, D_MODEL:2 * D_MODEL]) * y_b
    out_ref[...] = h_ref[...] + _dot(merged, wout_ref[...])


def _merge(y, bonus, gate, o_mla, gates, h, prm, tm):
    t = h.shape[0]
    row = lambda c: pl.BlockSpec((tm, c), lambda i: (i, 0))
    consts = [prm["lnx_w"], prm["lnx_b"], prm["head_ones"], prm["w_oa"], prm["w_ob"], prm["w_out"]]
    return pl.pallas_call(
        _merge_kernel,
        grid=(t // tm,),
        in_specs=[row(RWKV_DIM)] * 4 + [row(2 * D_MODEL), row(D_MODEL)]
        + [_const_spec(c.shape) for c in consts],
        out_specs=row(D_MODEL),
        out_shape=jax.ShapeDtypeStruct((t, D_MODEL), F32),
        compiler_params=_params("parallel"),
        name="merge",
    )(y, bonus, gate, o_mla, gates, h, *consts)


def _ffn_kernel(blocks_per_seq, final, h_ref, g_ref, wg_ref, wv_ref, cw_ref, cb_ref, wd_ref, fg_ref,
                out_ref, n_ref, acc_ref, carry_ref):
    i = pl.program_id(0)
    f = pl.program_id(1)
    tm = h_ref.shape[0]
    rows = min(FFN_SUB_ROWS, tm)

    @pl.when(f == 0)
    def _():
        n_ref[...] = _rms(h_ref[...], g_ref[...]).astype(BF16)
        acc_ref[...] = jnp.zeros_like(acc_ref)

    def up(sb):
        n = n_ref[sb * rows:(sb + 1) * rows, :]
        return (jnp.dot(n, wg_ref[...], preferred_element_type=F32),
                jnp.dot(n, wv_ref[...], preferred_element_type=F32))

    tail = carry_ref[f]
    tail = jnp.where(i % blocks_per_seq == 0, jnp.zeros_like(tail), tail)
    cw = cw_ref[0]
    row = lax.broadcasted_iota(jnp.int32, (rows, 1), 0)
    sqrt_half = np.sqrt(0.5).astype(np.float32)
    nxt = up(0)
    for sb in range(tm // rows):
        u_gate, u_val = nxt
        if sb + 1 < tm // rows:
            nxt = up(sb + 1)
        back1 = jnp.where(row == 0, tail[SUBLANES - 1:SUBLANES], pltpu.roll(u_gate, 1, axis=0))
        back2 = jnp.where(row == 0, tail[SUBLANES - 2:SUBLANES - 1],
                          jnp.where(row == 1, tail[SUBLANES - 1:SUBLANES], pltpu.roll(u_gate, 2, axis=0)))
        c = cb_ref[0] + cw[0:1] * back2
        c = c + cw[1:2] * back1
        c = c + cw[2:3] * u_gate
        tail = u_gate[rows - SUBLANES:rows]
        act = 0.5 * c * (1.0 + lax.erf(c * sqrt_half))
        down = jnp.dot((act * u_val).astype(BF16), wd_ref[...], preferred_element_type=F32)
        acc_ref[sb * rows:(sb + 1) * rows, :] += down
    carry_ref[f] = tail

    @pl.when(f == pl.num_programs(1) - 1)
    def _():
        out = h_ref[...] + acc_ref[...]
        out_ref[...] = _rms(out, fg_ref[...]) if final else out


def _conv_ffn(h, prm, final_gain, seq, tm, fc):
    t = h.shape[0]
    nf = FFN_DIM // fc
    row = pl.BlockSpec((tm, D_MODEL), lambda i, f: (i, 0))
    final = final_gain is not None
    fg = final_gain.reshape(1, D_MODEL) if final else prm["ffn_norm"]
    return pl.pallas_call(
        functools.partial(_ffn_kernel, seq // tm, final),
        grid=(t // tm, nf),
        in_specs=[row, _const_spec((1, D_MODEL)),
                  pl.BlockSpec((D_MODEL, fc), lambda i, f: (0, f)),
                  pl.BlockSpec((D_MODEL, fc), lambda i, f: (0, nf + f)),
                  pl.BlockSpec((1, SUBLANES, fc), lambda i, f: (f, 0, 0)),
                  pl.BlockSpec((1, 1, fc), lambda i, f: (f, 0, 0)),
                  pl.BlockSpec((fc, D_MODEL), lambda i, f: (f, 0)),
                  _const_spec((1, D_MODEL))],
        out_specs=row,
        out_shape=jax.ShapeDtypeStruct((t, D_MODEL), F32),
        scratch_shapes=[pltpu.VMEM((tm, D_MODEL), BF16), pltpu.VMEM((tm, D_MODEL), F32),
                        pltpu.VMEM((nf, SUBLANES, fc), F32)],
        compiler_params=_params("arbitrary", "arbitrary"),
        name="conv_ffn",
    )(h, prm["ffn_norm"], prm["w_up"], prm["w_up"], prm["conv_w"], prm["conv_b"], prm["w_down"], fg)


def _layer_params(l, fc, attn_norm, w_in, mu_shift, decay_base, w_decay_up, iclr_base, w_iclr_up,
                  w_gate_up, k_k, k_a, r_k, lnx_w, lnx_b, w_out_rwkv, q_norm, w_q_up, kv_norm,
                  w_kv_up, w_out_mla, w_out, ffn_norm, w_ffn_up, conv_w, conv_b, w_ffn_down):
    row = lambda x: x.reshape(1, -1)
    w = w_in[l]
    w_mla_src = w[:, SHIFT_COLS:SHIFT_COLS + MLA_COLS]
    lat = Q_LORA_RANK + KV_LORA_RANK
    w_mla = jnp.zeros((D_MODEL, MLA_PAD_COLS), F32)
    w_mla = w_mla.at[:, 0:lat].set(w_mla_src[:, 0:lat])
    w_mla = w_mla.at[:, lat + ROPE_LANE0:lat + ROPE_LANE0 + QK_ROPE_DIM].set(w_mla_src[:, lat:])
    w_lora = jnp.zeros((DECAY_LORA + ICLR_LORA, 2 * RWKV_DIM), F32)
    w_lora = w_lora.at[0:DECAY_LORA, 0:RWKV_DIM].set(w_decay_up[l])
    w_lora = w_lora.at[DECAY_LORA:, RWKV_DIM:].set(w_iclr_up[l])
    head = jnp.arange(MXU_WIDTH) // RWKV_HEAD_DIM
    head_ones = (head[:, None] == head[None, :]).astype(BF16)
    qk = QK_NOPE_DIM + QK_ROPE_DIM
    w_q = jnp.pad(w_q_up[l].reshape(Q_LORA_RANK, MLA_HEADS, qk), ((0, 0), (0, 0), (0, LANES - qk)))
    w_kv = w_kv_up[l].reshape(KV_LORA_RANK, MLA_HEADS, QK_NOPE_DIM + V_HEAD_DIM)
    w_k = jnp.pad(w_kv[:, :, :QK_NOPE_DIM], ((0, 0), (0, 0), (0, LANES - QK_NOPE_DIM)))
    nf = FFN_DIM // fc
    return dict(
        attn_norm=row(attn_norm[l]),
        w_rw=w[:, :SHIFT_COLS].astype(BF16),
        w_mla=w_mla.astype(BF16),
        w_g=w[:, SHIFT_COLS + MLA_COLS:].astype(BF16),
        mu=row(mu_shift[l]), w_lora=w_lora.astype(BF16), w_gate_up=w_gate_up[l].astype(BF16),
        decay_base=row(decay_base[l]), iclr_base=row(iclr_base[l]), k_k=row(k_k[l]), k_a=row(k_a[l]),
        r_k=row(r_k[l]), head_ones=head_ones, lnx_w=row(lnx_w[l]), lnx_b=row(lnx_b[l]),
        q_norm=row(q_norm[l]), kv_norm=row(kv_norm[l]),
        w_qt=w_q.reshape(Q_LORA_RANK, MLA_HEADS * LANES).T.astype(BF16),
        w_k=w_k.reshape(KV_LORA_RANK, MLA_HEADS * LANES).astype(BF16),
        w_vt=w_kv[:, :, QK_NOPE_DIM:].reshape(KV_LORA_RANK, MLA_HEADS * V_HEAD_DIM).T.astype(BF16),
        w_oa=w_out_rwkv[l].astype(BF16), w_ob=w_out_mla[l].astype(BF16), w_out=w_out[l].astype(BF16),
        ffn_norm=row(ffn_norm[l]), w_up=w_ffn_up[l].astype(BF16), w_down=w_ffn_down[l].astype(BF16),
        conv_w=jnp.pad(conv_w[l].reshape(3, nf, fc).transpose(1, 0, 2), ((0, 0), (0, SUBLANES - 3), (0, 0))),
        conv_b=conv_b[l].reshape(nf, 1, fc),
    )


def _tile_plan(seq):
    pick = lambda want: min(want, seq)
    return dict(proj=pick(512), attn=pick(512), merge=pick(512), ffn=pick(1024),
                ffn_cols=FFN_DIM // 2)


def kernel(x, positions, attn_norm, w_in, mu_shift, decay_base, w_decay_up, iclr_base, w_iclr_up, w_gate_up, k_k, k_a, r_k, lnx_w, lnx_b, w_out_rwkv, q_norm, w_q_up, kv_norm, w_kv_up, w_out_mla, w_out, ffn_norm, w_ffn_up, conv_w, conv_b, w_ffn_down, final_norm):
    batch, seq, _ = x.shape
    depth = w_in.shape[0]
    assert seq % WKV_CHUNK == 0 and x.shape[2] == D_MODEL
    plan = _tile_plan(seq)
    tables = _rope_tables(positions)
    h = x.reshape(batch * seq, D_MODEL)
    for l in range(depth):
        prm = _layer_params(l, plan["ffn_cols"], attn_norm, w_in, mu_shift, decay_base, w_decay_up,
                            iclr_base, w_iclr_up, w_gate_up, k_k, k_a, r_k, lnx_w, lnx_b, w_out_rwkv,
                            q_norm, w_q_up, kv_norm, w_kv_up, w_out_mla, w_out, ffn_norm, w_ffn_up,
                            conv_w, conv_b, w_ffn_down)
        p_mla, gates, r, lw, k, v, a, b, gate, bonus = _inproj(h, prm, seq, plan["proj"])
        y = _wkv(r, lw, k, v, a, b, batch, seq)
        q_h, k_h, v_h = _mla_prep(p_mla, tables, prm, batch, seq, plan["attn"])
        o_mla = _mla_attn(q_h, k_h, v_h, batch, seq, plan["attn"])
        h = _merge(y, bonus, gate, o_mla, gates, h, prm, plan["merge"])
        last = final_norm if l == depth - 1 else None
        h = _conv_ffn(h, prm, last, seq, plan["ffn"], plan["ffn_cols"])
    return h.reshape(batch, seq, D_MODEL)
```

```python
import functools

import jax
import jax.numpy as jnp
import numpy as np
from jax import lax
from jax.experimental import pallas as pl
from jax.experimental.pallas import tpu as pltpu

F32 = jnp.float32
BF16 = jnp.bfloat16

D_MODEL = 1024
RWKV_HEADS = 8
RWKV_HEAD_DIM = 64
RWKV_DIM = RWKV_HEADS * RWKV_HEAD_DIM
DECAY_LORA = 64
ICLR_LORA = 64
GATE_LORA = 128
LNX_EPS = 64e-5
MLA_HEADS = 8
QK_NOPE_DIM = 64
QK_ROPE_DIM = 32
V_HEAD_DIM = 64
Q_LORA_RANK = 256
KV_LORA_RANK = 128
ROPE_THETA = 10000.0
NEG_INF = -1e30
FFN_DIM = 2816
NORM_EPS = 1e-6
SHIFT_COLS = 3 * RWKV_DIM + DECAY_LORA + ICLR_LORA + GATE_LORA
MLA_COLS = Q_LORA_RANK + KV_LORA_RANK + QK_ROPE_DIM

LANES = 128
SUBLANES = 8
MXU_WIDTH = 256
VMEM_LIMIT_BYTES = 56 * 1024 * 1024

MLA_PAD_COLS = 4 * LANES
ROPE_LANE0 = QK_NOPE_DIM
WKV_CHUNK = 64
PAIR = 2 * RWKV_HEAD_DIM
LOG2_E = float(np.log2(np.e))
V_ONES_ROWS = 16
V_EXT_ROWS = V_HEAD_DIM + V_ONES_ROWS
ATTN_Q_PIECE = 256
FFN_SUB_ROWS = 256


def _params(*semantics, flags=None):
    return pltpu.CompilerParams(dimension_semantics=semantics, vmem_limit_bytes=VMEM_LIMIT_BYTES,
                                flags=flags)


def _dot(a, b):
    return jnp.dot(a.astype(BF16), b.astype(BF16), preferred_element_type=F32)


def _dot_nt(a, b):
    return lax.dot_general(a.astype(BF16), b.astype(BF16), (((1,), (1,)), ((), ())),
                           preferred_element_type=F32)


def _split_bf16(x, terms):
    parts = []
    rest = x
    for _ in range(terms):
        p = rest.astype(BF16)
        parts.append(p)
        rest = rest - p.astype(F32)
    return parts


def _dot_split_lhs(a, b_bf16, terms):
    out = None
    for p in _split_bf16(a, terms):
        d = jnp.dot(p, b_bf16, preferred_element_type=F32)
        out = d if out is None else out + d
    return out


def _head_sums(x, ones):
    w = ones.shape[0]
    return jnp.concatenate([_dot_split_lhs(x[:, c:c + w], ones, 2) for c in range(0, x.shape[1], w)],
                           axis=1)


def _rms(x, g):
    ms = jnp.mean(x * x, axis=-1, keepdims=True)
    return (x * lax.rsqrt(ms + NORM_EPS)) * g


def _const_spec(shape):
    return pl.BlockSpec(shape, lambda *_: (0,) * len(shape))


def _rope_tables_kernel(pos_ref, invf_ref, cos_ref, sin_ref):
    ang = pos_ref[...] * invf_ref[...]
    cos_ref[...] = jnp.cos(ang)
    sin_ref[...] = jnp.sin(ang)


def _rope_tables(positions):
    t = positions.size
    half = QK_ROPE_DIM // 2
    per_row = LANES // half
    inv_freq = jnp.power(ROPE_THETA, -jnp.arange(0, QK_ROPE_DIM, 2, dtype=F32) / QK_ROPE_DIM)
    invf = jnp.tile(inv_freq, per_row).reshape(1, LANES)
    posc = jnp.repeat(positions.reshape(t // per_row, per_row).astype(F32), half, axis=1)
    spec = pl.BlockSpec(posc.shape, lambda i: (0, 0))
    out = jax.ShapeDtypeStruct(posc.shape, F32)
    cos, sin = pl.pallas_call(
        _rope_tables_kernel,
        grid=(1,),
        in_specs=[spec, _const_spec((1, LANES))],
        out_specs=[spec, spec],
        out_shape=[out, out],
        compiler_params=_params("arbitrary"),
        name="rope_tables",
    )(posc, invf)
    return cos.reshape(t, half).T, sin.reshape(t, half).T


def _rope_t(tile, cos_t, sin_t):
    half = QK_ROPE_DIM // 2
    r0 = ROPE_LANE0
    x1 = tile[r0:r0 + half]
    x2 = tile[r0 + half:r0 + QK_ROPE_DIM]
    return jnp.concatenate([tile[0:r0], x1 * cos_t - x2 * sin_t, x1 * sin_t + x2 * cos_t,
                            tile[r0 + QK_ROPE_DIM:]], axis=0)


def _inproj_kernel(blocks_per_seq, x_ref, g_ref, wrw_ref, wmla_ref, wgate_ref, mu_ref, wl_ref,
                   wg_ref, dbase_ref, ibase_ref, kk_ref, ka_ref, rk_ref, ones_ref,
                   pmla_ref, pg_ref, r_ref, lw_ref, k_ref, v_ref, a_ref, b_ref, gate_ref, bonus_ref,
                   tail_ref):
    i = pl.program_id(0)
    n = _rms(x_ref[...], g_ref[...]).astype(BF16)
    x = jnp.dot(n, wrw_ref[...], preferred_element_type=F32)
    pmla_ref[...] = jnp.dot(n, wmla_ref[...], preferred_element_type=F32)
    pg_ref[...] = jnp.dot(n, wgate_ref[...], preferred_element_type=F32).astype(BF16)
    tm = x.shape[0]
    prev = tail_ref[SUBLANES - 1:SUBLANES, :]
    prev = jnp.where(i % blocks_per_seq == 0, jnp.zeros_like(prev), prev)
    tail_ref[...] = x[tm - SUBLANES:tm]
    row = lax.broadcasted_iota(jnp.int32, (tm, 1), 0)
    shifted = jnp.where(row == 0, prev, pltpu.roll(x, 1, axis=0))
    xs = x + (shifted - x) * mu_ref[...]

    c0 = 3 * RWKV_DIM
    p_r = xs[:, 0:RWKV_DIM]
    p_k = xs[:, RWKV_DIM:2 * RWKV_DIM]
    p_v = xs[:, 2 * RWKV_DIM:c0]
    p_wa = xs[:, c0:c0 + DECAY_LORA + ICLR_LORA]
    p_g = xs[:, c0 + DECAY_LORA + ICLR_LORA:SHIFT_COLS]

    lane = lax.broadcasted_iota(jnp.int32, p_wa.shape, 1)
    lora_in = jnp.where(lane < DECAY_LORA, jnp.tanh(p_wa), p_wa)
    lora = _dot(lora_in, wl_ref[...])
    log_w = -jax.nn.softplus(-(dbase_ref[...] + lora[:, 0:RWKV_DIM])) - 0.5
    iclr = jax.nn.sigmoid(ibase_ref[...] + lora[:, RWKV_DIM:2 * RWKV_DIM])
    gate = _dot(jax.nn.sigmoid(p_g), wg_ref[...])

    ones = ones_ref[...]
    kk = p_k * kk_ref[...]
    kk = kk * lax.rsqrt(_head_sums(kk * kk, ones) + 1e-12)
    k = p_k * (1.0 + (iclr - 1.0) * ka_ref[...])

    r_ref[...] = p_r
    lw_ref[...] = -jnp.exp(log_w)
    k_ref[...] = k
    v_ref[...] = p_v
    a_ref[...] = -kk
    b_ref[...] = kk * iclr
    gate_ref[...] = gate
    bonus_ref[...] = _head_sums(p_r * k * rk_ref[...], ones) * p_v


def _inproj(h, prm, seq, tm):
    t = h.shape[0]
    row = lambda c: pl.BlockSpec((tm, c), lambda i: (i, 0))
    out = lambda c, dt: jax.ShapeDtypeStruct((t, c), dt)
    consts = [prm["attn_norm"], prm["w_rw"], prm["w_mla"], prm["w_g"], prm["mu"], prm["w_lora"],
              prm["w_gate_up"], prm["decay_base"], prm["iclr_base"], prm["k_k"], prm["k_a"], prm["r_k"],
              prm["head_ones"]]
    return pl.pallas_call(
        functools.partial(_inproj_kernel, seq // tm),
        grid=(t // tm,),
        in_specs=[row(D_MODEL)] + [_const_spec(c.shape) for c in consts],
        out_specs=[row(MLA_PAD_COLS), row(2 * D_MODEL)] + [row(RWKV_DIM)] * 8,
        out_shape=[out(MLA_PAD_COLS, F32), out(2 * D_MODEL, BF16)] + [out(RWKV_DIM, F32)] * 8,
        scratch_shapes=[pltpu.VMEM((SUBLANES, SHIFT_COLS), F32)],
        compiler_params=_params("arbitrary"),
        name="inproj",
    )(h, *consts)


def _wkv_pair(load, state_ref, idx, store_y, zbuf, ltri, eye):
    c = WKV_CHUNK
    lane = lax.broadcasted_iota(jnp.int32, (1, PAIR), 1)
    lo = lane < RWKV_HEAD_DIM
    r, lw, k, v, a, b = load()

    cum = _dot_split_lhs_left(ltri, lw)
    yield
    cum_last = cum[c - 1:c, :]
    g_inv = jnp.exp(-cum)
    a_t = a * jnp.exp(cum - lw)
    r_t = r * jnp.exp(cum)
    zero = jnp.zeros_like(a_t)
    lhs = jnp.concatenate([jnp.where(lo, a_t, zero), jnp.where(lo, zero, a_t),
                           jnp.where(lo, r_t, zero), jnp.where(lo, zero, r_t)], axis=0)
    b_t = b * g_inv
    k_t = k * g_inv
    a_all = _dot_nt(lhs, jnp.concatenate([b_t, k_t], axis=0))
    yield
    swapped = pltpu.roll(a_all, c, axis=1)
    head0_rows = (lax.broadcasted_iota(jnp.int32, (4 * c, 1), 0) // c) % 2 == 0
    a_b = jnp.where(head0_rows, a_all, swapped)
    a_k = jnp.where(head0_rows, swapped, a_all)

    rr = lax.broadcasted_iota(jnp.int32, (2 * c, 2 * c), 0)
    cc = lax.broadcasted_iota(jnp.int32, (2 * c, 2 * c), 1)
    same_head = (rr // c) == (cc // c)
    strict = same_head & ((cc % c) < (rr % c))
    incl = same_head & ((cc % c) <= (rr % c))
    l_bd = jnp.where(strict, a_b[0:2 * c], 0.0)
    aak_bd = jnp.where(strict, a_k[0:2 * c], 0.0)
    arb_bd = jnp.where(incl, a_b[2 * c:4 * c], 0.0)
    ark_bd = jnp.where(incl, a_k[2 * c:4 * c], 0.0)
    v_st = jnp.concatenate([jnp.where(lo, v, zero), jnp.where(lo, zero, v)], axis=0)
    av = _dot(jnp.concatenate([aak_bd, ark_bd], axis=0), v_st)
    q_st = av[0:2 * c]
    arkv_st = av[2 * c:4 * c]

    t_bd = eye + l_bd
    p = _dot(l_bd, l_bd)
    yield
    n = 2
    while 2 * n < c:
        pp_pt = _dot(p, jnp.concatenate([p, t_bd], axis=1))
        yield
        p = pp_pt[:, 0:2 * c]
        t_bd = t_bd + pp_pt[:, 2 * c:4 * c]
        n *= 2
    t_bd = t_bd + _dot(p, t_bd)
    yield

    tq_ta = _dot(t_bd, jnp.concatenate([lhs[0:2 * c], q_st], axis=1))
    yield
    rq_ra = _dot(arb_bd, tq_ta)
    yield
    ta_st = tq_ta[:, 0:PAIR]
    tq_st = tq_ta[:, PAIR:2 * PAIR]
    rh_st = lhs[2 * c:4 * c] + rq_ra[:, 0:PAIR]
    yh_st = rq_ra[:, PAIR:2 * PAIR] + arkv_st
    unstack = lambda x: x[0:c] + x[c:2 * c]
    ta = unstack(ta_st)
    rh = unstack(rh_st)
    tq = unstack(tq_st)
    yh = unstack(yh_st)

    state = state_ref[idx]
    zbuf[...] = _dot_nt(state, jnp.concatenate([ta, rh], axis=0))
    yield
    z = zbuf[...] + jnp.concatenate([tq, yh], axis=0).T
    v_t = jnp.concatenate([v, v], axis=0).T
    lane2 = lax.broadcasted_iota(jnp.int32, (1, 2 * c), 1)
    g_out = jnp.exp(cum_last - cum)
    upd = _dot(jnp.where(lane2 < c, z, v_t), jnp.concatenate([b * g_out, k * g_out], axis=0))
    yield
    state_ref[idx] = state * jnp.exp(cum_last) + jnp.where(same_head, upd, 0.0)
    store_y(z.T[c:2 * c, :])


def _dot_split_lhs_left(m_bf16, x):
    out = None
    for p in _split_bf16(x, 3):
        d = jnp.dot(m_bf16, p, preferred_element_type=F32)
        out = d if out is None else out + d
    return out


def _wkv_kernel(r_ref, lw_ref, k_ref, v_ref, a_ref, b_ref, ltri_ref, eye_ref, y_ref, state_ref, z_ref):
    @pl.when(pl.program_id(0) == 0)
    def _():
        state_ref[...] = jnp.zeros_like(state_ref)

    ltri = ltri_ref[...]
    eye = eye_ref[...]
    pairs = RWKV_DIM // PAIR
    chains = []
    for bi in range(r_ref.shape[0]):
        for p in range(pairs):
            sl = slice(p * PAIR, (p + 1) * PAIR)

            def load(bi=bi, sl=sl):
                return tuple(ref[bi, :, sl] for ref in (r_ref, lw_ref, k_ref, v_ref, a_ref, b_ref))

            def store_y(y, bi=bi, sl=sl):
                y_ref[bi, :, sl] = y

            chain = bi * pairs + p
            chains.append(_wkv_pair(load, state_ref, chain, store_y, z_ref.at[chain], ltri, eye))
    while chains:
        chains = [ch for ch in chains if next(ch, StopIteration) is not StopIteration]


def _wkv(r, lw, k, v, a, b, batch, seq):
    c = WKV_CHUNK
    shp = (batch, seq, RWKV_DIM)
    args = [x.reshape(shp) for x in (r, lw, k, v, a, b)]
    ltri = jnp.tril(jnp.ones((c, c), F32)).astype(BF16)
    eye = jnp.eye(2 * c, dtype=F32)
    blk = pl.BlockSpec((batch, c, RWKV_DIM), lambda ci: (0, ci, 0))
    y = pl.pallas_call(
        _wkv_kernel,
        grid=(seq // c,),
        in_specs=[blk] * 6 + [_const_spec((c, c)), _const_spec((2 * c, 2 * c))],
        out_specs=blk,
        out_shape=jax.ShapeDtypeStruct(shp, F32),
        scratch_shapes=[pltpu.VMEM((batch * (RWKV_DIM // PAIR), PAIR, PAIR), F32)] * 2,
        compiler_params=_params("arbitrary"),
        name="wkv",
    )(*args, ltri, eye)
    return y.reshape(batch * seq, RWKV_DIM)


def _mla_prep_kernel(p_ref, cost_ref, sint_ref, qn_ref, kvn_ref, wqt_ref, wk_ref, wvt_ref,
                     qt_ref, k_ref, vt_ref):
    x = p_ref[...]
    blk = x.shape[0]
    cos_t, sin_t = cost_ref[...], sint_ref[...]
    scale = (QK_NOPE_DIM + QK_ROPE_DIM) ** -0.5 * LOG2_E
    qn = _rms(x[:, 0:Q_LORA_RANK], qn_ref[...]).astype(BF16)
    kvn = _rms(x[:, Q_LORA_RANK:Q_LORA_RANK + KV_LORA_RANK], kvn_ref[...]).astype(BF16)
    k_pe = _rope_t(x[:, Q_LORA_RANK + KV_LORA_RANK:MLA_PAD_COLS].T, cos_t, sin_t).T
    kf = jnp.dot(kvn, wk_ref[...], preferred_element_type=F32)
    for h in range(MLA_HEADS):
        k_ref[0, h, 0] = (kf[:, h * LANES:(h + 1) * LANES] + k_pe).astype(BF16)
    q_t = _dot_nt(wqt_ref[...], qn)
    v_t = _dot_nt(wvt_ref[...], kvn)
    for h in range(MLA_HEADS):
        qt_ref[0, h] = (_rope_t(q_t[h * LANES:(h + 1) * LANES], cos_t, sin_t) * scale).astype(BF16)
    ones = jnp.ones((V_ONES_ROWS, blk), F32)
    for h in range(0, MLA_HEADS, 2):
        rows = lambda g: v_t[g * V_HEAD_DIM:(g + 1) * V_HEAD_DIM]
        vt_ref[0, h // 2, 0] = jnp.concatenate([rows(h), ones, rows(h + 1), ones], axis=0).astype(BF16)


def _mla_prep(p_mla, tables, prm, batch, seq, blk):
    t = p_mla.shape[0]
    n = seq // blk
    tab_t = pl.BlockSpec((QK_ROPE_DIM // 2, blk), lambda i: (0, i))
    consts = [prm["q_norm"], prm["kv_norm"], prm["w_qt"], prm["w_k"], prm["w_vt"]]
    return pl.pallas_call(
        _mla_prep_kernel,
        grid=(t // blk,),
        in_specs=[pl.BlockSpec((blk, MLA_PAD_COLS), lambda i: (i, 0)), tab_t, tab_t]
        + [_const_spec(c.shape) for c in consts],
        out_specs=[pl.BlockSpec((1, MLA_HEADS, LANES, blk), lambda i: (i // n, 0, 0, i % n)),
                   pl.BlockSpec((1, MLA_HEADS, 1, blk, LANES), lambda i: (i // n, 0, i % n, 0, 0)),
                   pl.BlockSpec((1, MLA_HEADS // 2, 1, 2 * V_EXT_ROWS, blk), lambda i: (i // n, 0, i % n, 0, 0))],
        out_shape=[jax.ShapeDtypeStruct((batch, MLA_HEADS, LANES, seq), BF16),
                   jax.ShapeDtypeStruct((batch, MLA_HEADS, n, blk, LANES), BF16),
                   jax.ShapeDtypeStruct((batch, MLA_HEADS // 2, n, 2 * V_EXT_ROWS, blk), BF16)],
        compiler_params=_params("parallel"),
        name="mla_prep",
    )(p_mla, *tables, *consts)


def _attn_kernel(qt_ref, k_ref, vt_ref, o_ref, m_ref, acc_ref, s0_ref, s1_ref):
    n_blk, blk = k_ref.shape[2], k_ref.shape[3]
    units = [(i, j) for i in range(n_blk) for j in range(i + 1)]
    bufs = (s0_ref, s1_ref)

    def scores(unit, s_ref):
        i, j = unit
        for h in range(2):
            s_ref[h] = jnp.dot(k_ref[0, h, j], qt_ref[0, h, :, i * blk:(i + 1) * blk],
                               preferred_element_type=F32)

    def update(unit, s_ref):
        i, j = unit
        if j == 0:
            m_ref[...] = jnp.full_like(m_ref, -jnp.inf)
            acc_ref[...] = jnp.zeros_like(acc_ref)
        for h in range(2):
            vt = vt_ref[0, 0, j, h * V_EXT_ROWS:(h + 1) * V_EXT_ROWS, :]
            for c0 in range(0, blk, ATTN_Q_PIECE):
                cols = slice(c0, c0 + ATTN_Q_PIECE)
                rows = c0 + ATTN_Q_PIECE if j == i else blk
                s = s_ref[h, 0:rows, cols]
                if j == i:
                    k_pos = lax.broadcasted_iota(jnp.int32, s.shape, 0)
                    q_pos = lax.broadcasted_iota(jnp.int32, s.shape, 1) + c0
                    s = jnp.where(k_pos <= q_pos, s, NEG_INF)
                m_old = m_ref[h, :, cols]
                m_new = jnp.maximum(m_old, jnp.max(s, axis=0, keepdims=True))
                pv = jnp.dot(vt[:, 0:rows], jnp.exp2(s - m_new).astype(BF16), preferred_element_type=F32)
                acc_ref[h, :, cols] = jnp.exp2(m_old - m_new) * acc_ref[h, :, cols] + pv
                m_ref[h, :, cols] = m_new
        if j == i:
            o_t = jnp.concatenate([acc_ref[h, 0:V_HEAD_DIM] / acc_ref[h, V_HEAD_DIM:V_HEAD_DIM + 1]
                                   for h in range(2)], axis=0)
            o_ref[0, i * blk:(i + 1) * blk, :] = o_t.T.astype(BF16)

    scores(units[0], bufs[0])
    for n, unit in enumerate(units):
        if n + 1 < len(units):
            scores(units[n + 1], bufs[(n + 1) % 2])
        update(unit, bufs[n % 2])


def _mla_attn(qt, k, vt, batch, seq, blk):
    n = seq // blk
    hv = MLA_HEADS * V_HEAD_DIM
    o = pl.pallas_call(
        _attn_kernel,
        grid=(batch, MLA_HEADS // 2),
        in_specs=[pl.BlockSpec((1, 2, LANES, seq), lambda b, p: (b, p, 0, 0)),
                  pl.BlockSpec((1, 2, n, blk, LANES), lambda b, p: (b, p, 0, 0, 0)),
                  pl.BlockSpec((1, 1, n, 2 * V_EXT_ROWS, blk), lambda b, p: (b, p, 0, 0, 0))],
        out_specs=pl.BlockSpec((1, seq, LANES), lambda b, p: (b, 0, p)),
        out_shape=jax.ShapeDtypeStruct((batch, seq, hv), BF16),
        scratch_shapes=[pltpu.VMEM((2, 1, blk), F32), pltpu.VMEM((2, V_EXT_ROWS, blk), F32),
                        pltpu.VMEM((2, blk, blk), F32), pltpu.VMEM((2, blk, blk), F32)],
        compiler_params=_params("parallel", "parallel"),
        name="mla_attn",
    )(qt, k, vt)
    return o.reshape(batch * seq, hv)


def _merge_kernel(y_ref, bonus_ref, gate_ref, o_ref, g_ref, h_ref, lnw_ref, lnb_ref, ones_ref,
                  woa_ref, wob_ref, wout_ref, out_ref):
    ones = ones_ref[...]
    inv_n = 1.0 / RWKV_HEAD_DIM
    y = y_ref[...]
    mu = _head_sums(y, ones) * inv_n
    d = y - mu
    var = _head_sums(d * d, ones) * inv_n
    yn = (d * lax.rsqrt(var + LNX_EPS)) * lnw_ref[...] + lnb_ref[...]
    z = (yn + bonus_ref[...]) * gate_ref[...]
    y_a = _dot(z, woa_ref[...])
    y_b = jnp.dot(o_ref[...], wob_ref[...], preferred_element_type=F32)
    g = g_ref[...].astype(F32)
    merged = jax.nn.sigmoid(g[:, 0:D_MODEL]) * y_a + jax.nn.sigmoid(g[:, D_MODEL:2 * D_MODEL]) * y_b
    out_ref[...] = h_ref[...] + _dot(merged, wout_ref[...])


def _merge(y, bonus, gate, o_mla, gates, h, prm, tm):
    t = h.shape[0]
    row = lambda c: pl.BlockSpec((tm, c), lambda i: (i, 0))
    consts = [prm["lnx_w"], prm["lnx_b"], prm["head_ones"], prm["w_oa"], prm["w_ob"], prm["w_out"]]
    return pl.pallas_call(
        _merge_kernel,
        grid=(t // tm,),
        in_specs=[row(RWKV_DIM)] * 4 + [row(2 * D_MODEL), row(D_MODEL)]
        + [_const_spec(c.shape) for c in consts],
        out_specs=row(D_MODEL),
        out_shape=jax.ShapeDtypeStruct((t, D_MODEL), F32),
        compiler_params=_params("parallel"),
        name="merge",
    )(y, bonus, gate, o_mla, gates, h, *consts)


def _ffn_kernel(blocks_per_seq, final, h_ref, g_ref, wg_ref, wv_ref, cw_ref, cb_ref, wd_ref, fg_ref,
                out_ref, n_ref, acc_ref, carry_ref):
    i = pl.program_id(0)
    f = pl.program_id(1)
    tm = h_ref.shape[0]
    rows = min(FFN_SUB_ROWS, tm)

    @pl.when(f == 0)
    def _():
        n_ref[...] = _rms(h_ref[...], g_ref[...]).astype(BF16)
        acc_ref[...] = jnp.zeros_like(acc_ref)

    def up(sb):
        n = n_ref[sb * rows:(sb + 1) * rows, :]
        return (jnp.dot(n, wg_ref[...], preferred_element_type=F32),
                jnp.dot(n, wv_ref[...], preferred_element_type=F32))

    tail = carry_ref[f]
    tail = jnp.where(i % blocks_per_seq == 0, jnp.zeros_like(tail), tail)
    cw = cw_ref[0]
    row = lax.broadcasted_iota(jnp.int32, (rows, 1), 0)
    sqrt_half = np.sqrt(0.5).astype(np.float32)
    nxt = up(0)
    for sb in range(tm // rows):
        u_gate, u_val = nxt
        if sb + 1 < tm // rows:
            nxt = up(sb + 1)
        back1 = jnp.where(row == 0, tail[SUBLANES - 1:SUBLANES], pltpu.roll(u_gate, 1, axis=0))
        back2 = jnp.where(row == 0, tail[SUBLANES - 2:SUBLANES - 1],
                          jnp.where(row == 1, tail[SUBLANES - 1:SUBLANES], pltpu.roll(u_gate, 2, axis=0)))
        c = cb_ref[0] + cw[0:1] * back2
        c = c + cw[1:2] * back1
        c = c + cw[2:3] * u_gate
        tail = u_gate[rows - SUBLANES:rows]
        act = 0.5 * c * (1.0 + lax.erf(c * sqrt_half))
        down = jnp.dot((act * u_val).astype(BF16), wd_ref[...], preferred_element_type=F32)
        acc_ref[sb * rows:(sb + 1) * rows, :] += down
    carry_ref[f] = tail

    @pl.when(f == pl.num_programs(1) - 1)
    def _():
        out = h_ref[...] + acc_ref[...]
        out_ref[...] = _rms(out, fg_ref[...]) if final else out


def _conv_ffn(h, prm, final_gain, seq, tm, fc):
    t = h.shape[0]
    nf = FFN_DIM // fc
    row = pl.BlockSpec((tm, D_MODEL), lambda i, f: (i, 0))
    final = final_gain is not None
    fg = final_gain.reshape(1, D_MODEL) if final else prm["ffn_norm"]
    return pl.pallas_call(
        functools.partial(_ffn_kernel, seq // tm, final),
        grid=(t // tm, nf),
        in_specs=[row, _const_spec((1, D_MODEL)),
                  pl.BlockSpec((D_MODEL, fc), lambda i, f: (0, f), pipeline_mode=pl.Buffered(1)),
                  pl.BlockSpec((D_MODEL, fc), lambda i, f: (0, nf + f), pipeline_mode=pl.Buffered(1)),
                  pl.BlockSpec((1, SUBLANES, fc), lambda i, f: (f, 0, 0)),
                  pl.BlockSpec((1, 1, fc), lambda i, f: (f, 0, 0)),
                  pl.BlockSpec((fc, D_MODEL), lambda i, f: (f, 0), pipeline_mode=pl.Buffered(1)),
                  _const_spec((1, D_MODEL))],
        out_specs=row,
        out_shape=jax.ShapeDtypeStruct((t, D_MODEL), F32),
        scratch_shapes=[pltpu.VMEM((tm, D_MODEL), BF16), pltpu.VMEM((tm, D_MODEL), F32),
                        pltpu.VMEM((nf, SUBLANES, fc), F32)],
        compiler_params=_params("arbitrary", "arbitrary"),
        name="conv_ffn",
    )(h, prm["ffn_norm"], prm["w_up"], prm["w_up"], prm["conv_w"], prm["conv_b"], prm["w_down"], fg)


def _layer_params(l, fc, attn_norm, w_in, mu_shift, decay_base, w_decay_up, iclr_base, w_iclr_up,
                  w_gate_up, k_k, k_a, r_k, lnx_w, lnx_b, w_out_rwkv, q_norm, w_q_up, kv_norm,
                  w_kv_up, w_out_mla, w_out, ffn_norm, w_ffn_up, conv_w, conv_b, w_ffn_down):
    row = lambda x: x.reshape(1, -1)
    w = w_in[l]
    w_mla_src = w[:, SHIFT_COLS:SHIFT_COLS + MLA_COLS]
    lat = Q_LORA_RANK + KV_LORA_RANK
    w_mla = jnp.zeros((D_MODEL, MLA_PAD_COLS), F32)
    w_mla = w_mla.at[:, 0:lat].set(w_mla_src[:, 0:lat])
    w_mla = w_mla.at[:, lat + ROPE_LANE0:lat + ROPE_LANE0 + QK_ROPE_DIM].set(w_mla_src[:, lat:])
    w_lora = jnp.zeros((DECAY_LORA + ICLR_LORA, 2 * RWKV_DIM), F32)
    w_lora = w_lora.at[0:DECAY_LORA, 0:RWKV_DIM].set(w_decay_up[l])
    w_lora = w_lora.at[DECAY_LORA:, RWKV_DIM:].set(w_iclr_up[l])
    head = jnp.arange(MXU_WIDTH) // RWKV_HEAD_DIM
    head_ones = (head[:, None] == head[None, :]).astype(BF16)
    qk = QK_NOPE_DIM + QK_ROPE_DIM
    w_q = jnp.pad(w_q_up[l].reshape(Q_LORA_RANK, MLA_HEADS, qk), ((0, 0), (0, 0), (0, LANES - qk)))
    w_kv = w_kv_up[l].reshape(KV_LORA_RANK, MLA_HEADS, QK_NOPE_DIM + V_HEAD_DIM)
    w_k = jnp.pad(w_kv[:, :, :QK_NOPE_DIM], ((0, 0), (0, 0), (0, LANES - QK_NOPE_DIM)))
    nf = FFN_DIM // fc
    return dict(
        attn_norm=row(attn_norm[l]),
        w_rw=w[:, :SHIFT_COLS].astype(BF16),
        w_mla=w_mla.astype(BF16),
        w_g=w[:, SHIFT_COLS + MLA_COLS:].astype(BF16),
        mu=row(mu_shift[l]), w_lora=w_lora.astype(BF16), w_gate_up=w_gate_up[l].astype(BF16),
        decay_base=row(decay_base[l]), iclr_base=row(iclr_base[l]), k_k=row(k_k[l]), k_a=row(k_a[l]),
        r_k=row(r_k[l]), head_ones=head_ones, lnx_w=row(lnx_w[l]), lnx_b=row(lnx_b[l]),
        q_norm=row(q_norm[l]), kv_norm=row(kv_norm[l]),
        w_qt=w_q.reshape(Q_LORA_RANK, MLA_HEADS * LANES).T.astype(BF16),
        w_k=w_k.reshape(KV_LORA_RANK, MLA_HEADS * LANES).astype(BF16),
        w_vt=w_kv[:, :, QK_NOPE_DIM:].reshape(KV_LORA_RANK, MLA_HEADS * V_HEAD_DIM).T.astype(BF16),
        w_oa=w_out_rwkv[l].astype(BF16), w_ob=w_out_mla[l].astype(BF16), w_out=w_out[l].astype(BF16),
        ffn_norm=row(ffn_norm[l]), w_up=w_ffn_up[l].astype(BF16), w_down=w_ffn_down[l].astype(BF16),
        conv_w=jnp.pad(conv_w[l].reshape(3, nf, fc).transpose(1, 0, 2), ((0, 0), (0, SUBLANES - 3), (0, 0))),
        conv_b=conv_b[l].reshape(nf, 1, fc),
    )


def _tile_plan(seq):
    pick = lambda want: min(want, seq)
    return dict(proj=pick(512), attn=pick(512), merge=pick(512), ffn=pick(1024),
                ffn_cols=FFN_DIM)


def kernel(x, positions, attn_norm, w_in, mu_shift, decay_base, w_decay_up, iclr_base, w_iclr_up, w_gate_up, k_k, k_a, r_k, lnx_w, lnx_b, w_out_rwkv, q_norm, w_q_up, kv_norm, w_kv_up, w_out_mla, w_out, ffn_norm, w_ffn_up, conv_w, conv_b, w_ffn_down, final_norm):
    batch, seq, _ = x.shape
    depth = w_in.shape[0]
    assert seq % WKV_CHUNK == 0 and x.shape[2] == D_MODEL
    plan = _tile_plan(seq)
    tables = _rope_tables(positions)
    h = x.reshape(batch * seq, D_MODEL)
    for l in range(depth):
        prm = _layer_params(l, plan["ffn_cols"], attn_norm, w_in, mu_shift, decay_base, w_decay_up,
                            iclr_base, w_iclr_up, w_gate_up, k_k, k_a, r_k, lnx_w, lnx_b, w_out_rwkv,
                            q_norm, w_q_up, kv_norm, w_kv_up, w_out_mla, w_out, ffn_norm, w_ffn_up,
                            conv_w, conv_b, w_ffn_down)
        p_mla, gates, r, lw, k, v, a, b, gate, bonus = _inproj(h, prm, seq, plan["proj"])
        y = _wkv(r, lw, k, v, a, b, batch, seq)
        q_h, k_h, v_h = _mla_prep(p_mla, tables, prm, batch, seq, plan["attn"])
        o_mla = _mla_attn(q_h, k_h, v_h, batch, seq, plan["attn"])
        h = _merge(y, bonus, gate, o_mla, gates, h, prm, plan["merge"])
        last = final_norm if l == depth - 1 else None
        h = _conv_ffn(h, prm, last, seq, plan["ffn"], plan["ffn_cols"])
    return h.reshape(batch, seq, D_MODEL)
```
